```python
import jax
import jax.numpy as jnp
from jax import lax
import numpy as np

D_MODEL = 1024
BATCH = 4
SEQ = 4096
DEPTH = 1
DEC_BATCH = 2
DEC_SEQ = 8192
PAST_LEN = 128

HEAD_DIM = 64
N_HEADS_MLA = 8
N_HEADS_DIL = 8
MLA_WIDTH = N_HEADS_MLA * HEAD_DIM
DIL_WIDTH = N_HEADS_DIL * HEAD_DIM
Q_LORA = 256
KV_LORA = 128
QK_NOPE = 64
QK_ROPE = 32
V_DIM = 64
ROPE_BASE = 10000.0
Q_BLOCK = 128
DIL_PATTERNS = ((128, 1), (512, 4), (2048, 16))
N_IN = Q_LORA + KV_LORA + QK_ROPE + 3 * DIL_WIDTH
N_EXPERTS = 256
TOP_K = 8
N_EXPERT_GROUPS = 8
TOPK_GROUPS = 4
D_EXPERT = 256
D_SHARED = 256
ROUTE_SCALE = 2.5
EXPERT_BLOCK = 128
EPS = 1e-6
NEG = -1e30

kernel_name = 'hybrid_mla_dilated_moe_encoder'


def rmsnorm(x, g):
    xf = x.astype(jnp.float32)
    y = xf * lax.rsqrt(jnp.mean(xf * xf, axis=-1, keepdims=True) + EPS)
    return (y * g.astype(jnp.float32)).astype(x.dtype)


def apply_rope(x):
    S = x.shape[1]
    half = QK_ROPE // 2
    inv = ROPE_BASE ** (-jnp.arange(half, dtype=jnp.float32) / half)
    ang = jnp.arange(S, dtype=jnp.float32)[:, None] * inv[None, :]
    shape = (1, S) + (1,) * (x.ndim - 3) + (half,)
    cos = jnp.cos(ang).reshape(shape)
    sin = jnp.sin(ang).reshape(shape)
    x1 = x[..., :half].astype(jnp.float32)
    x2 = x[..., half:].astype(jnp.float32)
    out = jnp.concatenate([x1 * cos - x2 * sin, x1 * sin + x2 * cos], axis=-1)
    return out.astype(x.dtype)


def alibi_slopes(n):
    return 2.0 ** (-8.0 * jnp.arange(1, n + 1, dtype=jnp.float32) / n)


def swiglu(x, wg, wu, wd):
    return (jax.nn.silu(x @ wg) * (x @ wu)) @ wd


def mla_attention(c_q, c_kv, k_rope, g_qa, w_uq, g_kva, w_ukv):
    B, S, _ = c_q.shape
    q = (rmsnorm(c_q, g_qa) @ w_uq).reshape(B, S, N_HEADS_MLA, QK_NOPE + QK_ROPE)
    q_nope = q[..., :QK_NOPE]
    q_rope = apply_rope(q[..., QK_NOPE:])
    kv = (rmsnorm(c_kv, g_kva) @ w_ukv).reshape(B, S, N_HEADS_MLA, QK_NOPE + V_DIM)
    k_nope = kv[..., :QK_NOPE]
    v = kv[..., QK_NOPE:]
    k_r = apply_rope(k_rope)
    scale = (QK_NOPE + QK_ROPE) ** -0.5
    nq = S // Q_BLOCK
    qn_b = q_nope.reshape(B, nq, Q_BLOCK, N_HEADS_MLA, QK_NOPE).transpose(1, 0, 2, 3, 4)
    qr_b = q_rope.reshape(B, nq, Q_BLOCK, N_HEADS_MLA, QK_ROPE).transpose(1, 0, 2, 3, 4)

    def block(args):
        qn, qr = args
        s = jnp.einsum('bqhd,bkhd->bhqk', qn, k_nope) + jnp.einsum('bqhr,bkr->bhqk', qr, k_r)
        p = jax.nn.softmax(s.astype(jnp.float32) * scale, axis=-1)
        return jnp.einsum('bhqk,bkhd->bqhd', p.astype(v.dtype), v)

    o = lax.map(block, (qn_b, qr_b))
    return o.transpose(1, 0, 2, 3, 4).reshape(B, S, MLA_WIDTH)


def band_attention(q, k, v, half, dil, slopes):
    B, S, H, Dh = q.shape
    L = S // dil
    nb = -(-L // half)
    Lp = nb * half

    def split(t):
        return t.reshape(B, L, dil, H, Dh).transpose(0, 2, 1, 3, 4)

    qs = jnp.pad(split(q), ((0, 0), (0, 0), (0, Lp - L), (0, 0), (0, 0))).reshape(B, dil, nb, half, H, Dh)
    pad_kv = ((0, 0), (0, 0), (half, Lp - L + half), (0, 0), (0, 0))

    def band(t):
        tb = jnp.pad(split(t), pad_kv).reshape(B, dil, nb + 2, half, H, Dh)
        return jnp.concatenate([tb[:, :, :-2], tb[:, :, 1:-1], tb[:, :, 2:]], axis=3)

    kb = band(k)
    vb = band(v)
    s = jnp.einsum('brnqhd,brnkhd->brnhqk', qs, kb).astype(jnp.float32) * (HEAD_DIM ** -0.5)
    qpos = jnp.arange(nb)[:, None] * half + jnp.arange(half)[None, :]
    kpos = jnp.arange(nb)[:, None] * half + jnp.arange(3 * half)[None, :] - half
    off = kpos[:, None, :] - qpos[:, :, None]
    dist = jnp.abs(off)
    valid = (dist <= half) & (kpos[:, None, :] >= 0) & (kpos[:, None, :] < L)
    alibi = slopes[None, None, None, :, None, None] * (dil * dist).astype(jnp.float32)[None, None, :, None]
    s = jnp.where(valid[None, None, :, None], s - alibi, NEG)
    lse = jax.nn.logsumexp(s, axis=-1)
    p = jnp.exp(s - lse[..., None])
    o = jnp.einsum('brnhqk,brnkhd->brnqhd', p.astype(vb.dtype), vb).astype(jnp.float32)
    o = o.reshape(B, dil, Lp, H, Dh)[:, :, :L].transpose(0, 2, 1, 3, 4).reshape(B, S, H, Dh)
    lse = lse.transpose(0, 1, 2, 4, 3).reshape(B, dil, Lp, H)[:, :, :L].transpose(0, 2, 1, 3).reshape(B, S, H)
    return o, lse


def dilated_attention(q, k, v):
    B, S, H, Dh = q.shape
    slopes = alibi_slopes(N_HEADS_DIL)
    outs = []
    lses = []
    for window, dil in DIL_PATTERNS:
        o, lse = band_attention(q, k, v, window // (2 * dil), dil, slopes)
        outs.append(o)
        lses.append(lse)
    wts = jax.nn.softmax(jnp.stack(lses, axis=0), axis=0)
    o = jnp.sum(wts[..., None] * jnp.stack(outs, axis=0), axis=0)
    return o.reshape(B, S, DIL_WIDTH)


def moe_ffn(h, router_w, router_bias, w_exp_gate, w_exp_up, w_exp_down, w_sh_gate, w_sh_up, w_sh_down):
    B, S, D = h.shape
    T = B * S
    xt = h.reshape(T, D)
    scores = jax.nn.sigmoid(xt.astype(jnp.float32) @ router_w.astype(jnp.float32))
    biased = scores + router_bias.astype(jnp.float32)
    grouped = biased.reshape(T, N_EXPERT_GROUPS, N_EXPERTS // N_EXPERT_GROUPS)
    group_score = jnp.sum(lax.top_k(grouped, 2)[0], axis=-1)
    _, top_groups = lax.top_k(group_score, TOPK_GROUPS)
    gmask = jnp.any(top_groups[:, :, None] == jnp.arange(N_EXPERT_GROUPS)[None, None, :], axis=1)
    masked = jnp.where(gmask[:, :, None], grouped, -jnp.inf).reshape(T, N_EXPERTS)
    _, idx = lax.top_k(masked, TOP_K)
    gates = jnp.take_along_axis(scores, idx, axis=-1)
    gates = gates / jnp.sum(gates, axis=-1, keepdims=True) * ROUTE_SCALE
    A = T * TOP_K
    e_flat = idx.reshape(A)
    tok_flat = jnp.arange(A, dtype=jnp.int32) // TOP_K
    order = jnp.argsort(e_flat)
    e_sorted = e_flat[order]
    tok_sorted = tok_flat[order]
    gate_sorted = gates.reshape(A)[order]
    counts = jnp.bincount(e_flat, length=N_EXPERTS)
    padded = (counts + EXPERT_BLOCK - 1) // EXPERT_BLOCK * EXPERT_BLOCK
    pad_end = jnp.cumsum(padded)
    pad_start = pad_end - padded
    start = jnp.cumsum(counts) - counts
    dest = pad_start[e_sorted] + jnp.arange(A, dtype=jnp.int32) - start[e_sorted]
    n_blocks = -(-(A + N_EXPERTS * (EXPERT_BLOCK - 1)) // EXPERT_BLOCK)
    P = n_blocks * EXPERT_BLOCK
    slot_tok = jnp.full((P,), T, jnp.int32).at[dest].set(tok_sorted)
    block_expert = jnp.minimum(jnp.searchsorted(pad_end, jnp.arange(n_blocks) * EXPERT_BLOCK, side='right'), N_EXPERTS - 1)
    x_slots = jnp.concatenate([xt, jnp.zeros((1, D), xt.dtype)], axis=0)[slot_tok].reshape(n_blocks, EXPERT_BLOCK, D)

    def expert_block(args):
        xb, e = args
        return swiglu(xb, w_exp_gate[e], w_exp_up[e], w_exp_down[e])

    y_slots = lax.map(expert_block, (x_slots, block_expert)).reshape(P, D)
    routed = jax.ops.segment_sum(y_slots[dest] * gate_sorted[:, None].astype(y_slots.dtype), tok_sorted, num_segments=T)
    shared = swiglu(xt, w_sh_gate, w_sh_up, w_sh_down)
    return (routed + shared).reshape(B, S, D)


def encoder_layer(x, c, w_ada, b_ada, g_pre_mix, w_in, g_qa, w_uq, g_kva, w_ukv, g_out_a, g_out_b, w_o, g_post_mix, g_pre_ffn, router_w, router_bias, w_exp_gate, w_exp_up, w_exp_down, w_sh_gate, w_sh_up, w_sh_down, g_post_ffn):
    B, S, _ = x.shape
    ada = jax.nn.silu(c) @ w_ada + b_ada
    sh1, sc1, gt1, sh2, sc2, gt2 = [a[:, None, :] for a in jnp.split(ada, 6, axis=-1)]
    h = rmsnorm(x, g_pre_mix) * (1 + sc1) + sh1
    z = h @ w_in
    c_q, c_kv, k_rope, qkv = jnp.split(z, [Q_LORA, Q_LORA + KV_LORA, Q_LORA + KV_LORA + QK_ROPE], axis=-1)
    o_a = mla_attention(c_q, c_kv, k_rope, g_qa, w_uq, g_kva, w_ukv)
    q, k, v = [t.reshape(B, S, N_HEADS_DIL, HEAD_DIM) for t in jnp.split(qkv, 3, axis=-1)]
    o_b = dilated_attention(q, k, v).astype(x.dtype)
    o = jnp.concatenate([rmsnorm(o_a, g_out_a), rmsnorm(o_b, g_out_b)], axis=-1) @ w_o
    x = x + gt1 * rmsnorm(o, g_post_mix)
    h = rmsnorm(x, g_pre_ffn) * (1 + sc2) + sh2
    y = moe_ffn(h, router_w, router_bias, w_exp_gate, w_exp_up, w_exp_down, w_sh_gate, w_sh_up, w_sh_down)
    return x + gt2 * rmsnorm(y, g_post_ffn)


def trunk(x, c, params):
    for layer in range(DEPTH):
        x = encoder_layer(x, c, *[p[layer] for p in params])
    return x


def setup_inputs(seed: int = 0) -> dict:
    key = jax.random.key(seed)
    ks = jax.random.split(key, 26)
    f32 = jnp.float32
    D = D_MODEL

    def nrm(k, shape, scale):
        return jax.random.normal(k, shape, f32) * scale

    def gain(k, shape):
        return 1.0 + 0.05 * jax.random.normal(k, shape, f32)

    return {
        'x_prompt': nrm(ks[0], (BATCH, SEQ, D), 1.0),
        'x_sample': nrm(ks[1], (DEC_BATCH, DEC_SEQ, D), 1.0),
        'c_prompt': nrm(ks[2], (BATCH, D), 1.0),
        'c_sample': nrm(ks[3], (DEC_BATCH, D), 1.0),
        'w_ada': nrm(ks[4], (DEPTH, D, 6 * D), 0.5 * D ** -0.5),
        'b_ada': nrm(ks[5], (DEPTH, 6 * D), 0.01),
        'g_pre_mix': gain(ks[6], (DEPTH, D)),
        'w_in': nrm(ks[7], (DEPTH, D, N_IN), D ** -0.5),
        'g_qa': gain(ks[8], (DEPTH, Q_LORA)),
        'w_uq': nrm(ks[9], (DEPTH, Q_LORA, N_HEADS_MLA * (QK_NOPE + QK_ROPE)), Q_LORA ** -0.5),
        'g_kva': gain(ks[10], (DEPTH, KV_LORA)),
        'w_ukv': nrm(ks[11], (DEPTH, KV_LORA, N_HEADS_MLA * (QK_NOPE + V_DIM)), KV_LORA ** -0.5),
        'g_out_a': gain(ks[12], (DEPTH, MLA_WIDTH)),
        'g_out_b': gain(ks[13], (DEPTH, DIL_WIDTH)),
        'w_o': nrm(ks[14], (DEPTH, MLA_WIDTH + DIL_WIDTH, D), (MLA_WIDTH + DIL_WIDTH) ** -0.5),
        'g_post_mix': gain(ks[15], (DEPTH, D)),
        'g_pre_ffn': gain(ks[16], (DEPTH, D)),
        'router_w': nrm(ks[17], (DEPTH, D, N_EXPERTS), D ** -0.5),
        'router_bias': nrm(ks[18], (DEPTH, N_EXPERTS), 0.01),
        'w_exp_gate': nrm(ks[19], (DEPTH, N_EXPERTS, D, D_EXPERT), D ** -0.5),
        'w_exp_up': nrm(ks[20], (DEPTH, N_EXPERTS, D, D_EXPERT), D ** -0.5),
        'w_exp_down': nrm(ks[21], (DEPTH, N_EXPERTS, D_EXPERT, D), D_EXPERT ** -0.5),
        'w_sh_gate': nrm(ks[22], (DEPTH, D, D_SHARED), D ** -0.5),
        'w_sh_up': nrm(ks[23], (DEPTH, D, D_SHARED), D ** -0.5),
        'w_sh_down': nrm(ks[24], (DEPTH, D_SHARED, D), D_SHARED ** -0.5),
        'g_post_ffn': gain(ks[25], (DEPTH, D)),
    }


def reference(x_prompt, x_sample, c_prompt, c_sample, w_ada, b_ada, g_pre_mix, w_in, g_qa, w_uq, g_kva, w_ukv, g_out_a, g_out_b, w_o, g_post_mix, g_pre_ffn, router_w, router_bias, w_exp_gate, w_exp_up, w_exp_down, w_sh_gate, w_sh_up, w_sh_down, g_post_ffn):
    params = (w_ada, b_ada, g_pre_mix, w_in, g_qa, w_uq, g_kva, w_ukv, g_out_a, g_out_b, w_o, g_post_mix, g_pre_ffn, router_w, router_bias, w_exp_gate, w_exp_up, w_exp_down, w_sh_gate, w_sh_up, w_sh_down, g_post_ffn)
    y_prompt = trunk(x_prompt, c_prompt, params)
    y_sample = trunk(x_sample, c_sample, params)
    return (y_prompt, y_sample)
```

```python
import functools
import math

import jax
import jax.numpy as jnp
from jax import lax
from jax.experimental import pallas as pl
from jax.experimental.pallas import tpu as pltpu

F32 = jnp.float32
BF16 = jnp.bfloat16

D_MODEL = 1024
HEAD_DIM = 64
N_HEADS = 8
Q_LORA = 256
KV_LORA = 128
QK_NOPE = 64
QK_ROPE = 32
ROPE_BASE = 10000.0
DIL_PATTERNS = ((128, 1), (512, 4), (2048, 16))
DIL_HALF = 64
N_EXPERTS = 256
TOP_K = 8
N_GROUPS = 8
GROUP_SIZE = N_EXPERTS // N_GROUPS
TOPK_GROUPS = 4
D_EXPERT = 256
ROUTE_SCALE = 2.5
EPS = 1e-6
NEG = -1e30

LANES = 128
HEAD_PAD = 128

TM_PRE = 512
TM_POST = 512
TM_COMB = 512
TQ_MLA = 256
TK_MLA = 512
TQ_DIL = 128
EXPERT_BLOCK = 256
VMEM_LIMIT = 56 * 1024 * 1024


def _cparams(sem):
    return pltpu.CompilerParams(dimension_semantics=sem, vmem_limit_bytes=VMEM_LIMIT)


def _rms(x, g):
    return x * lax.rsqrt(jnp.mean(x * x, axis=-1, keepdims=True) + EPS) * g


def _sigmoid(x):
    return 1.0 / (1.0 + jnp.exp(-x))


def _dot(a, b):
    return jnp.dot(a, b, preferred_element_type=F32)


def _dot_nt(a, b, precision=None):
    return lax.dot_general(a, b, (((1,), (1,)), ((), ())), preferred_element_type=F32, precision=precision)


def _ada_kernel(c_ref, w_ref, b_ref, o_ref):
    c = c_ref[...]
    s = c * _sigmoid(c)
    o_ref[...] = jnp.dot(s, w_ref[...], preferred_element_type=F32, precision=lax.Precision.HIGHEST) + b_ref[...]


def _ada(c_all, w_ada, b_ada):
    nb, d = c_all.shape
    n_out = w_ada.shape[1]
    tn = 1024
    return pl.pallas_call(
        _ada_kernel,
        out_shape=jax.ShapeDtypeStruct((nb, n_out), F32),
        grid=(n_out // tn,),
        in_specs=[pl.BlockSpec((nb, d), lambda j: (0, 0)),
                  pl.BlockSpec((d, tn), lambda j: (0, j)),
                  pl.BlockSpec((1, tn), lambda j: (0, j))],
        out_specs=pl.BlockSpec((nb, tn), lambda j: (0, j)),
        compiler_params=_cparams(("arbitrary",)),
        name="ada",
    )(c_all, w_ada, b_ada.reshape(1, n_out))


def _premix_kernel(x_ref, ada_ref, g_ref, win_ref, gqa_ref, wuq_ref, gkva_ref, wukv_ref, tab_ref,
                   qa_ref, ka_ref, va_ref, qd_ref, kd_ref, vd_ref):
    x = x_ref[...]
    sh1 = ada_ref[0, 0:1, :]
    sc1 = ada_ref[0, 1:2, :]
    h = _rms(x, g_ref[...]) * (1.0 + sc1) + sh1
    z = _dot(h.astype(BF16), win_ref[...])
    tab = tab_ref[...]
    cosq, sinq = tab[:, 0:128], tab[:, 128:256]
    cosk, sink = tab[:, 256:384], tab[:, 384:512]
    nh = N_HEADS * HEAD_PAD

    cq = _rms(z[:, 0:Q_LORA], gqa_ref[...]).astype(BF16)
    qq = _dot(cq, wuq_ref[...])
    for h_i in range(N_HEADS):
        lo = h_i * HEAD_PAD
        qa_ref[:, lo:lo + HEAD_PAD] = (qq[:, lo:lo + HEAD_PAD] * cosq
                                       + qq[:, nh + lo:nh + lo + HEAD_PAD] * sinq).astype(BF16)

    ckv = _rms(z[:, 256:384], gkva_ref[...]).astype(BF16)
    kk = _dot(ckv, wukv_ref[...])
    rr = z[:, 384:512] * cosk + z[:, 512:640] * sink
    for h_i in range(N_HEADS):
        lo = h_i * HEAD_PAD
        ka_ref[:, lo:lo + HEAD_PAD] = (kk[:, lo:lo + HEAD_PAD] + rr).astype(BF16)
    va_ref[...] = kk[:, nh:nh + 512].astype(BF16)

    lane = lax.broadcasted_iota(jnp.int32, (x.shape[0], LANES), 1)
    low = lane < HEAD_DIM
    for j in range(N_HEADS // 2):
        blk = z[:, 640 + j * LANES:640 + (j + 1) * LANES] * (HEAD_DIM ** -0.5)
        qd_ref[:, (2 * j) * LANES:(2 * j + 1) * LANES] = jnp.where(low, blk, 0.0).astype(BF16)
        qd_ref[:, (2 * j + 1) * LANES:(2 * j + 2) * LANES] = jnp.where(low, 0.0, blk).astype(BF16)
    kd_ref[...] = z[:, 1152:1664].astype(BF16)
    vd_ref[...] = z[:, 1664:2176].astype(BF16)


def _premix(x_all, ada8, chunk, g_pre, win2, g_qa, wuq2, g_kva, wukv2, tab, n1, s1, s2):
    t_all = x_all.shape[0]
    tm = TM_PRE
    nt1 = n1 // tm
    tb1, tb2 = s1 // tm, s2 // tm

    def tab_map(i):
        return (jnp.where(i < nt1, i % tb1, (i - nt1) % tb2), 0)

    row = lambda i: (i, 0)
    const = lambda i: (0, 0)
    outs = [jax.ShapeDtypeStruct((t_all, w), BF16) for w in (1024, 1024, 512, 1024, 512, 512)]
    return pl.pallas_call(
        _premix_kernel,
        out_shape=outs,
        grid=(t_all // tm,),
        in_specs=[pl.BlockSpec((tm, D_MODEL), row),
                  pl.BlockSpec((1, 6, D_MODEL), lambda i: (i * tm // chunk, 0, 0)),
                  pl.BlockSpec((1, D_MODEL), const),
                  pl.BlockSpec(win2.shape, const),
                  pl.BlockSpec((1, Q_LORA), const),
                  pl.BlockSpec(wuq2.shape, const),
                  pl.BlockSpec((1, KV_LORA), const),
                  pl.BlockSpec(wukv2.shape, const),
                  pl.BlockSpec((tm, 512), tab_map)],
        out_specs=[pl.BlockSpec((tm, w), row) for w in (1024, 1024, 512, 1024, 512, 512)],
        compiler_params=_cparams(("parallel",)),
        name="premix",
    )(x_all, ada8, g_pre, win2, g_qa, wuq2, g_kva, wukv2, tab)


def _mla_kernel(q_ref, k_ref, v_ref, o_ref, *, seq, tk):
    tq = q_ref.shape[0]
    lane = lax.broadcasted_iota(jnp.int32, (tq, LANES), 1)
    outs = []
    for hh in range(2):
        q = q_ref[:, hh * HEAD_PAD:(hh + 1) * HEAD_PAD]

        def body(c, carry, hh=hh, q=q):
            m, l, acc = carry
            off = pl.multiple_of(c * tk, tk)
            k = k_ref[pl.ds(off, tk), hh * HEAD_PAD:(hh + 1) * HEAD_PAD]
            v = v_ref[pl.ds(off, tk), :]
            s = _dot_nt(q, k)
            m_new = jnp.maximum(m, jnp.max(s, axis=-1, keepdims=True))
            p = jnp.exp(s - m_new)
            alpha = jnp.exp(m - m_new)
            l = alpha * l + jnp.sum(p, axis=-1, keepdims=True)
            acc = alpha * acc + _dot(p.astype(BF16), v)
            return m_new, l, acc

        init = (jnp.full((tq, 1), -jnp.inf, F32), jnp.zeros((tq, 1), F32), jnp.zeros((tq, LANES), F32))
        _, l, acc = lax.fori_loop(0, seq // tk, body, init)
        outs.append(acc / l)
    o_ref[...] = jnp.where(lane < HEAD_DIM, outs[0], outs[1]).astype(o_ref.dtype)


def _mla(qa, ka, va, row_off, batch, seq):
    tq = TQ_MLA
    nq = seq // tq
    qoff = row_off // tq
    soff = row_off // seq
    return pl.pallas_call(
        functools.partial(_mla_kernel, seq=seq, tk=TK_MLA),
        out_shape=jax.ShapeDtypeStruct((batch * seq, N_HEADS * HEAD_DIM), BF16),
        grid=(batch, N_HEADS // 2, nq),
        in_specs=[pl.BlockSpec((tq, 2 * HEAD_PAD), lambda b, j, qi: (qoff + b * nq + qi, j)),
                  pl.BlockSpec((seq, 2 * HEAD_PAD), lambda b, j, qi: (soff + b, j)),
                  pl.BlockSpec((seq, LANES), lambda b, j, qi: (soff + b, j))],
        out_specs=pl.BlockSpec((tq, LANES), lambda b, j, qi: (b * nq + qi, j)),
        compiler_params=_cparams(("parallel", "parallel", "parallel")),
        name="mla_attention",
    )(qa, ka, va)


def _dil_kernel(q_ref, kp_ref, kc_ref, kn_ref, vp_ref, vc_ref, vn_ref, o_ref, lse_ref, *, dil, n1_rows, len1, len2):
    tq = q_ref.shape[0]
    half = DIL_HALF
    w = tq + 2 * half
    row0 = pl.program_id(0) * tq
    in1 = row0 < n1_rows
    seq_len = jnp.where(in1, len1, len2)
    q0 = jnp.where(in1, row0 % len1, (row0 - n1_rows) % len2)

    kw = jnp.concatenate([kp_ref[tq - half:tq, :], kc_ref[...], kn_ref[0:half, :]], axis=0)
    vw = jnp.concatenate([vp_ref[tq - half:tq, :], vc_ref[...], vn_ref[0:half, :]], axis=0)
    col = lax.broadcasted_iota(jnp.int32, (tq, w), 1)
    rw = lax.broadcasted_iota(jnp.int32, (tq, w), 0)
    absd = jnp.abs(col - half - rw)
    kpos = q0 + col - half
    valid = (absd <= half) & (kpos >= 0) & (kpos < seq_len)
    dist = (absd * dil).astype(F32)
    lane = lax.broadcasted_iota(jnp.int32, (tq, LANES), 1)
    low = lane < HEAD_DIM
    for j in range(N_HEADS // 2):
        kpair = kw[:, j * LANES:(j + 1) * LANES]
        vpair = vw[:, j * LANES:(j + 1) * LANES]
        res = []
        for hh in range(2):
            h_i = 2 * j + hh
            slope = 2.0 ** (-8.0 * (h_i + 1) / N_HEADS)
            s = _dot_nt(q_ref[:, h_i * LANES:(h_i + 1) * LANES], kpair)
            s = jnp.where(valid, s - slope * dist, NEG)
            m = jnp.max(s, axis=-1, keepdims=True)
            p = jnp.exp(s - m)
            l = jnp.sum(p, axis=-1, keepdims=True)
            res.append((_dot(p.astype(BF16), vpair) / l, m + jnp.log(l)))
        o_ref[:, j * LANES:(j + 1) * LANES] = jnp.where(low, res[0][0], res[1][0]).astype(o_ref.dtype)
        lse_ref[:, j * LANES:(j + 1) * LANES] = jnp.where(low, res[0][1], res[1][1])


def _dilated(qd, kd, vd, dil, n1, s1, s2):
    t_all = qd.shape[0]
    tq = TQ_DIL
    rows = t_all // dil
    nt = rows // tq
    wq = N_HEADS * LANES
    wk = N_HEADS * HEAD_DIM
    qv = qd.reshape(rows, dil * wq)
    kv = kd.reshape(rows, dil * wk)
    vv = vd.reshape(rows, dil * wk)
    cur = lambda i, r: (i, r)
    prv = lambda i, r: (jnp.maximum(i - 1, 0), r)
    nxt = lambda i, r: (jnp.minimum(i + 1, nt - 1), r)
    kspecs = [pl.BlockSpec((tq, wk), f) for f in (prv, cur, nxt)]
    o, lse = pl.pallas_call(
        functools.partial(_dil_kernel, dil=dil, n1_rows=n1 // dil, len1=s1 // dil, len2=s2 // dil),
        out_shape=[jax.ShapeDtypeStruct((rows, dil * wk), BF16), jax.ShapeDtypeStruct((rows, dil * wk), F32)],
        grid=(nt, dil),
        in_specs=[pl.BlockSpec((tq, wq), cur)] + kspecs + kspecs,
        out_specs=[pl.BlockSpec((tq, wk), cur), pl.BlockSpec((tq, wk), cur)],
        compiler_params=_cparams(("parallel", "parallel")),
        name="dilated_attention_d%d" % dil,
    )(qv, kv, kv, kv, vv, vv, vv)
    return o.reshape(t_all, wk), lse.reshape(t_all, wk)


def _postmix_kernel(x_ref, oa_ref, o1_ref, o2_ref, o3_ref, l1_ref, l2_ref, l3_ref, ada_ref, goa_ref, gob_ref,
                    wo_ref, gpm_ref, gpf_ref, rwt_ref, rb_ref, wsg_ref, wsu_ref, wsd_ref,
                    x1_ref, h2_ref, sh_ref, idx_ref, gate_ref, rank_ref, cnt_ref, carry_ref):
    tm = x_ref.shape[0]

    @pl.when(pl.program_id(0) == 0)
    def _():
        carry_ref[...] = jnp.zeros_like(carry_ref)

    gt1, sh2, sc2 = ada_ref[0, 2:3, :], ada_ref[0, 3:4, :], ada_ref[0, 4:5, :]

    la, lb, lc = l1_ref[...], l2_ref[...], l3_ref[...]
    mx = jnp.maximum(jnp.maximum(la, lb), lc)
    ea, eb, ec = jnp.exp(la - mx), jnp.exp(lb - mx), jnp.exp(lc - mx)
    ob = (ea * o1_ref[...].astype(F32) + eb * o2_ref[...].astype(F32) + ec * o3_ref[...].astype(F32)) / (ea + eb + ec)

    na = _rms(oa_ref[...].astype(F32), goa_ref[...]).astype(BF16)
    nb = _rms(ob, gob_ref[...]).astype(BF16)
    half_w = N_HEADS * HEAD_DIM
    o = _dot(na, wo_ref[0:half_w, :]) + _dot(nb, wo_ref[half_w:2 * half_w, :])
    x1 = x_ref[...] + gt1 * _rms(o, gpm_ref[...])
    x1_ref[...] = x1
    h2 = _rms(x1, gpf_ref[...]) * (1.0 + sc2) + sh2
    h2b = h2.astype(BF16)
    h2_ref[...] = h2b

    g = _dot(h2b, wsg_ref[...])
    u = _dot(h2b, wsu_ref[...])
    sh_ref[...] = _dot((g * _sigmoid(g) * u).astype(BF16), wsd_ref[...])

    scores = _sigmoid(_dot_nt(rwt_ref[...], h2, precision=lax.Precision.HIGHEST))
    biased = scores + rb_ref[...]
    ninf = -jnp.inf
    row = lax.broadcasted_iota(jnp.int32, (N_EXPERTS, tm), 0)
    rwg = lax.broadcasted_iota(jnp.int32, (GROUP_SIZE, tm), 0)
    gsc = []
    for gi in range(N_GROUPS):
        blk = biased[gi * GROUP_SIZE:(gi + 1) * GROUP_SIZE]
        m1 = jnp.max(blk, axis=0, keepdims=True)
        i1 = jnp.min(jnp.where(blk == m1, rwg, N_EXPERTS), axis=0, keepdims=True)
        m2 = jnp.max(jnp.where(rwg == i1, ninf, blk), axis=0, keepdims=True)
        gsc.append(m1 + m2)
    gsel = [jnp.zeros((1, tm), F32) for _ in range(N_GROUPS)]
    for _ in range(TOPK_GROUPS):
        m = functools.reduce(jnp.maximum, gsc)
        free = jnp.ones((1, tm), F32)
        for gi in range(N_GROUPS):
            hit = jnp.where(gsc[gi] == m, free, 0.0)
            free = free - hit
            gsel[gi] = gsel[gi] + hit
            gsc[gi] = jnp.where(hit > 0.0, ninf, gsc[gi])
    masked = jnp.concatenate(
        [jnp.where(gsel[gi] > 0.0, biased[gi * GROUP_SIZE:(gi + 1) * GROUP_SIZE], ninf)
         for gi in range(N_GROUPS)], axis=0)
    cur = masked
    idxs, gts = [], []
    for _ in range(TOP_K):
        m = jnp.max(cur, axis=0, keepdims=True)
        ik = jnp.min(jnp.where(cur == m, row, N_EXPERTS), axis=0, keepdims=True)
        hit = row == ik
        idxs.append(ik)
        gts.append(jnp.sum(jnp.where(hit, scores, 0.0), axis=0, keepdims=True))
        cur = jnp.where(hit, ninf, cur)
    idx = jnp.concatenate(idxs, axis=0)
    gates = jnp.concatenate(gts, axis=0)
    idx_ref[...] = idx
    gate_ref[...] = gates / jnp.sum(gates, axis=0, keepdims=True) * ROUTE_SCALE

    sel = jnp.where(cur != masked, 1.0, 0.0)
    tr = lax.broadcasted_iota(jnp.int32, (tm, tm), 0)
    tc = lax.broadcasted_iota(jnp.int32, (tm, tm), 1)
    before = jnp.where(tr < tc, 1.0, 0.0).astype(BF16)
    pos = _dot(sel.astype(BF16), before) + carry_ref[...]
    rank_ref[...] = jnp.concatenate(
        [jnp.sum(jnp.where(row == idxs[k], pos, 0.0), axis=0, keepdims=True) for k in range(TOP_K)],
        axis=0).astype(jnp.int32)
    carry = carry_ref[...] + jnp.sum(sel, axis=1, keepdims=True)
    carry_ref[...] = carry
    cnt_ref[...] = jnp.broadcast_to(carry, cnt_ref.shape)


def _postmix(x_all, oa, os_, ls_, ada8, chunk, g_oa, g_ob, wo, g_pm, g_pf, rwt, rb, wsg, wsu, wsd):
    t_all = x_all.shape[0]
    tm = TM_POST
    row = lambda i: (i, 0)
    col = lambda i: (0, i)
    const = lambda i: (0, 0)
    hw = N_HEADS * HEAD_DIM
    in_specs = ([pl.BlockSpec((tm, D_MODEL), row), pl.BlockSpec((tm, hw), row)]
                + [pl.BlockSpec((tm, hw), row)] * 6
                + [pl.BlockSpec((1, 6, D_MODEL), lambda i: (i * tm // chunk, 0, 0)),
                   pl.BlockSpec((1, hw), const), pl.BlockSpec((1, hw), const),
                   pl.BlockSpec(wo.shape, const),
                   pl.BlockSpec((1, D_MODEL), const), pl.BlockSpec((1, D_MODEL), const),
                   pl.BlockSpec(rwt.shape, const), pl.BlockSpec((N_EXPERTS, 1), const),
                   pl.BlockSpec(wsg.shape, const), pl.BlockSpec(wsu.shape, const), pl.BlockSpec(wsd.shape, const)])
    out_shape = [jax.ShapeDtypeStruct((t_all, D_MODEL), F32),
                 jax.ShapeDtypeStruct((t_all, D_MODEL), BF16),
                 jax.ShapeDtypeStruct((t_all, D_MODEL), F32),
                 jax.ShapeDtypeStruct((TOP_K, t_all), jnp.int32),
                 jax.ShapeDtypeStruct((TOP_K, t_all), F32),
                 jax.ShapeDtypeStruct((TOP_K, t_all), jnp.int32),
                 jax.ShapeDtypeStruct((N_EXPERTS, LANES), F32)]
    out_specs = [pl.BlockSpec((tm, D_MODEL), row), pl.BlockSpec((tm, D_MODEL), row), pl.BlockSpec((tm, D_MODEL), row),
                 pl.BlockSpec((TOP_K, tm), col), pl.BlockSpec((TOP_K, tm), col), pl.BlockSpec((TOP_K, tm), col),
                 pl.BlockSpec((N_EXPERTS, LANES), const)]
    return pl.pallas_call(
        _postmix_kernel,
        out_shape=out_shape,
        grid=(t_all // tm,),
        in_specs=in_specs,
        out_specs=out_specs,
        scratch_shapes=[pltpu.VMEM((N_EXPERTS, 1), F32)],
        compiler_params=_cparams(("arbitrary",)),
        name="postmix_router",
    )(x_all, oa, *os_, *ls_, ada8, g_oa, g_ob, wo, g_pm, g_pf, rwt, rb, wsg, wsu, wsd)


def _expert_kernel(be_ref, nu_ref, x_ref, wg_ref, wu_ref, wd_ref, y_ref, wgu_s, wd_s):
    i = pl.program_id(0)
    used = i < nu_ref[0]
    new_expert = (i == 0) | (be_ref[i] != be_ref[jnp.maximum(i - 1, 0)])

    @pl.when(used & new_expert)
    def _():
        wgu_s[:, 0:D_EXPERT] = wg_ref[0].astype(BF16)
        wgu_s[:, D_EXPERT:2 * D_EXPERT] = wu_ref[0].astype(BF16)
        wd_s[...] = wd_ref[0].astype(BF16)

    @pl.when(used)
    def _():
        gu = _dot(x_ref[...], wgu_s[...])
        g, u = gu[:, 0:D_EXPERT], gu[:, D_EXPERT:2 * D_EXPERT]
        y_ref[...] = _dot((g * _sigmoid(g) * u).astype(BF16), wd_s[...]).astype(y_ref.dtype)

    @pl.when(jnp.logical_not(used))
    def _():
        y_ref[...] = jnp.zeros_like(y_ref)


def _experts(block_expert, n_used, x_slots, wg, wu, wd):
    p_rows = x_slots.shape[0]
    blk = EXPERT_BLOCK
    wmap = lambda i, be, nu: (be[i], 0, 0)
    grid_spec = pltpu.PrefetchScalarGridSpec(
        num_scalar_prefetch=2,
        grid=(p_rows // blk,),
        in_specs=[pl.BlockSpec((blk, D_MODEL), lambda i, be, nu: (jnp.minimum(i, nu[0] - 1), 0)),
                  pl.BlockSpec((1, D_MODEL, D_EXPERT), wmap),
                  pl.BlockSpec((1, D_MODEL, D_EXPERT), wmap),
                  pl.BlockSpec((1, D_EXPERT, D_MODEL), wmap)],
        out_specs=pl.BlockSpec((blk, D_MODEL), lambda i, be, nu: (i, 0)),
        scratch_shapes=[pltpu.VMEM((D_MODEL, 2 * D_EXPERT), BF16), pltpu.VMEM((D_EXPERT, D_MODEL), BF16)])
    return pl.pallas_call(
        _expert_kernel,
        out_shape=jax.ShapeDtypeStruct((p_rows, D_MODEL), BF16),
        grid_spec=grid_spec,
        compiler_params=_cparams(("arbitrary",)),
        name="expert_ffn",
    )(block_expert, n_used, x_slots, wg, wu, wd)


def _combine_kernel(yg_ref, gate_ref, sh_ref, x1_ref, ada_ref, g_ref, y_ref):
    gates = gate_ref[...]
    acc = sh_ref[...]
    for k in range(TOP_K):
        acc = acc + gates[:, k:k + 1] * yg_ref[k].astype(F32)
    gt2 = ada_ref[0, 5:6, :]
    y_ref[...] = x1_ref[...] + gt2 * _rms(acc, g_ref[...])


def _combine(yg, gates_t, sh, x1, ada8, chunk, g_post):
    t_all = x1.shape[0]
    tm = TM_COMB
    row = lambda i: (i, 0)
    return pl.pallas_call(
        _combine_kernel,
        out_shape=jax.ShapeDtypeStruct((t_all, D_MODEL), F32),
        grid=(t_all // tm,),
        in_specs=[pl.BlockSpec((TOP_K, tm, D_MODEL), lambda i: (0, i, 0)),
                  pl.BlockSpec((tm, TOP_K), row),
                  pl.BlockSpec((tm, D_MODEL), row),
                  pl.BlockSpec((tm, D_MODEL), row),
                  pl.BlockSpec((1, 6, D_MODEL), lambda i: (i * tm // chunk, 0, 0)),
                  pl.BlockSpec((1, D_MODEL), lambda i: (0, 0))],
        out_specs=pl.BlockSpec((tm, D_MODEL), row),
        compiler_params=_cparams(("parallel",)),
        name="moe_combine",
    )(yg, gates_t, sh, x1, ada8, g_post)


def _rope_partner(w):
    half = QK_ROPE // 2
    return jnp.concatenate([-w[..., half:], w[..., :half]], axis=-1)


def _prep_weights(w_in, w_uq, w_ukv):
    d = w_in.shape[0]
    zeros = lambda r, c: jnp.zeros((r, c), F32)
    kr = w_in[:, 384:416]
    r_main = jnp.concatenate([zeros(d, QK_NOPE), kr, zeros(d, HEAD_PAD - QK_NOPE - QK_ROPE)], axis=1)
    r_part = jnp.concatenate([zeros(d, QK_NOPE), _rope_partner(kr), zeros(d, HEAD_PAD - QK_NOPE - QK_ROPE)], axis=1)
    win2 = jnp.concatenate([w_in[:, :384], r_main, r_part, w_in[:, 416:]], axis=1).astype(BF16)

    wq = w_uq.reshape(Q_LORA, N_HEADS, QK_NOPE + QK_ROPE)
    zq = jnp.zeros((Q_LORA, N_HEADS, HEAD_PAD - QK_NOPE - QK_ROPE), F32)
    q_main = jnp.concatenate([wq, zq], axis=-1).reshape(Q_LORA, N_HEADS * HEAD_PAD)
    q_part = jnp.concatenate([jnp.zeros((Q_LORA, N_HEADS, QK_NOPE), F32), _rope_partner(wq[..., QK_NOPE:]), zq],
                             axis=-1).reshape(Q_LORA, N_HEADS * HEAD_PAD)
    wuq2 = jnp.concatenate([q_main, q_part], axis=1).astype(BF16)

    wkv = w_ukv.reshape(KV_LORA, N_HEADS, QK_NOPE + HEAD_DIM)
    k_pad = jnp.concatenate([wkv[..., :QK_NOPE], jnp.zeros((KV_LORA, N_HEADS, HEAD_PAD - QK_NOPE), F32)],
                            axis=-1).reshape(KV_LORA, N_HEADS * HEAD_PAD)
    v_cols = wkv[..., QK_NOPE:].reshape(KV_LORA, N_HEADS * HEAD_DIM)
    wukv2 = jnp.concatenate([k_pad, v_cols], axis=1).astype(BF16)
    return win2, wuq2, wukv2


def _rope_table(s_max):
    half = QK_ROPE // 2
    inv = ROPE_BASE ** (-jnp.arange(half, dtype=F32) / half)
    ang = jnp.arange(s_max, dtype=F32)[:, None] * inv[None, :]
    cos = jnp.concatenate([jnp.cos(ang), jnp.cos(ang)], axis=1)
    sin = jnp.concatenate([jnp.sin(ang), jnp.sin(ang)], axis=1)
    scale = (QK_NOPE + QK_ROPE) ** -0.5
    pad = jnp.zeros((s_max, HEAD_PAD - QK_NOPE - QK_ROPE), F32)
    zn = jnp.zeros((s_max, QK_NOPE), F32)
    cosq = jnp.concatenate([jnp.full((s_max, QK_NOPE), scale, F32), cos * scale, pad], axis=1)
    sinq = jnp.concatenate([zn, sin * scale, pad], axis=1)
    cosk = jnp.concatenate([zn, cos, pad], axis=1)
    sink = jnp.concatenate([zn, sin, pad], axis=1)
    return jnp.concatenate([cosq, sinq, cosk, sink], axis=1)


def _layer(x_prompt, x_sample, c_prompt, c_sample, w_ada, b_ada, g_pre_mix, w_in, g_qa, w_uq, g_kva, w_ukv,
           g_out_a, g_out_b, w_o, g_post_mix, g_pre_ffn, router_w, router_bias, w_exp_gate, w_exp_up, w_exp_down,
           w_sh_gate, w_sh_up, w_sh_down, g_post_ffn):
    b1, s1, d = x_prompt.shape
    b2, s2, _ = x_sample.shape
    n1, n2 = b1 * s1, b2 * s2
    t_all = n1 + n2
    chunk = math.gcd(s1, s2)
    assert n1 % s2 == 0 and chunk % max(TM_PRE, TM_POST, TM_COMB) == 0

    x_all = jnp.concatenate([x_prompt.reshape(n1, d), x_sample.reshape(n2, d)], axis=0)
    c_all = jnp.concatenate([c_prompt, c_sample, jnp.zeros((8 - b1 - b2, d), F32)], axis=0)
    ada = _ada(c_all, w_ada, b_ada)
    chunk_batch = jnp.concatenate([jnp.repeat(jnp.arange(b1), s1 // chunk),
                                   b1 + jnp.repeat(jnp.arange(b2), s2 // chunk)])
    ada8 = ada.reshape(8, 6, d)[chunk_batch]

    win2, wuq2, wukv2 = _prep_weights(w_in, w_uq, w_ukv)
    tab = _rope_table(max(s1, s2))
    r2 = lambda g: g.reshape(1, -1)
    qa, ka, va, qd, kd, vd = _premix(x_all, ada8, chunk, r2(g_pre_mix), win2, r2(g_qa), wuq2, r2(g_kva), wukv2, tab,
                                     n1, s1, s2)

    oa = jnp.concatenate([_mla(qa, ka, va, 0, b1, s1), _mla(qa, ka, va, n1, b2, s2)], axis=0)
    os_, ls_ = [], []
    for _, dil in DIL_PATTERNS:
        o, lse = _dilated(qd, kd, vd, dil, n1, s1, s2)
        os_.append(o)
        ls_.append(lse)

    x1, h2b, sh, idx_t, gate_t, rank_t, cnt = _postmix(
        x_all, oa, os_, ls_, ada8, chunk, r2(g_out_a), r2(g_out_b), w_o.astype(BF16), r2(g_post_mix), r2(g_pre_ffn),
        router_w.T, router_bias.reshape(N_EXPERTS, 1), w_sh_gate.astype(BF16), w_sh_up.astype(BF16),
        w_sh_down.astype(BF16))

    blk = EXPERT_BLOCK
    n_assign = t_all * TOP_K
    n_blocks = -(-(n_assign + N_EXPERTS * (blk - 1)) // blk)
    counts = cnt[:, 0].astype(jnp.int32)
    padded = (counts + blk - 1) // blk * blk
    pad_end = jnp.cumsum(padded)
    pad_start = pad_end - padded
    dest = pad_start[idx_t] + rank_t
    tok = jnp.broadcast_to(jnp.arange(t_all, dtype=jnp.int32)[None, :], (TOP_K, t_all))
    slot_tok = jnp.zeros((n_blocks * blk,), jnp.int32).at[dest.reshape(-1)].set(tok.reshape(-1))
    block_expert = jnp.minimum(jnp.searchsorted(pad_end, jnp.arange(n_blocks, dtype=jnp.int32) * blk, side='right'),
                               N_EXPERTS - 1).astype(jnp.int32)
    n_used = (pad_end[-1] // blk).astype(jnp.int32).reshape(1)
    x_slots = h2b[slot_tok]
    y_slots = _experts(block_expert, n_used, x_slots, w_exp_gate, w_exp_up, w_exp_down)
    yg = y_slots[dest]

    y = _combine(yg, gate_t.T, sh, x1, ada8, chunk, r2(g_post_ffn))
    return y[:n1].reshape(b1, s1, d), y[n1:].reshape(b2, s2, d)


def kernel(x_prompt, x_sample, c_prompt, c_sample, w_ada, b_ada, g_pre_mix, w_in, g_qa, w_uq, g_kva, w_ukv, g_out_a, g_out_b, w_o, g_post_mix, g_pre_ffn, router_w, router_bias, w_exp_gate, w_exp_up, w_exp_down, w_sh_gate, w_sh_up, w_sh_down, g_post_ffn):
    layer = [p[0] for p in (w_ada, b_ada, g_pre_mix, w_in, g_qa, w_uq, g_kva, w_ukv, g_out_a, g_out_b, w_o,
                            g_post_mix, g_pre_ffn, router_w, router_bias, w_exp_gate, w_exp_up, w_exp_down,
                            w_sh_gate, w_sh_up, w_sh_down, g_post_ffn)]
    return _layer(x_prompt, x_sample, c_prompt, c_sample, *layer)
```

```python
import functools
import math

import jax
import jax.numpy as jnp
from jax import lax
from jax.experimental import pallas as pl
from jax.experimental.pallas import tpu as pltpu

F32 = jnp.float32
BF16 = jnp.bfloat16

D_MODEL = 1024
HEAD_DIM = 64
N_HEADS = 8
Q_LORA = 256
KV_LORA = 128
QK_NOPE = 64
QK_ROPE = 32
ROPE_BASE = 10000.0
DIL_PATTERNS = ((128, 1), (512, 4), (2048, 16))
DIL_HALF = 64
N_EXPERTS = 256
TOP_K = 8
N_GROUPS = 8
GROUP_SIZE = N_EXPERTS // N_GROUPS
TOPK_GROUPS = 4
D_EXPERT = 256
ROUTE_SCALE = 2.5
EPS = 1e-6
NEG = -1e30

LANES = 128
HEAD_PAD = 128

TM_PRE = 512
TM_POST = 512
TM_COMB = 512
TQ_MLA = 256
RB_MLA = 64
TM_DEST = 2048
TQ_DIL = 128
EXPERT_BLOCK = 256
VMEM_LIMIT = 56 * 1024 * 1024


def _cparams(sem):
    return pltpu.CompilerParams(dimension_semantics=sem, vmem_limit_bytes=VMEM_LIMIT)


def _rms(x, g):
    return x * lax.rsqrt(jnp.mean(x * x, axis=-1, keepdims=True) + EPS) * g


def _sigmoid(x):
    return 1.0 / (1.0 + jnp.exp(-x))


def _dot(a, b):
    return jnp.dot(a, b, preferred_element_type=F32)


def _dot_nt(a, b, precision=None):
    return lax.dot_general(a, b, (((1,), (1,)), ((), ())), preferred_element_type=F32, precision=precision)


def _ada_kernel(c_ref, w_ref, b_ref, o_ref):
    c = c_ref[...]
    s = c * _sigmoid(c)
    o_ref[...] = jnp.dot(s, w_ref[...], preferred_element_type=F32, precision=lax.Precision.HIGHEST) + b_ref[...]


def _ada(c_all, w_ada, b_ada):
    nb, d = c_all.shape
    n_out = w_ada.shape[1]
    tn = 1024
    return pl.pallas_call(
        _ada_kernel,
        out_shape=jax.ShapeDtypeStruct((nb, n_out), F32),
        grid=(n_out // tn,),
        in_specs=[pl.BlockSpec((nb, d), lambda j: (0, 0)),
                  pl.BlockSpec((d, tn), lambda j: (0, j)),
                  pl.BlockSpec((1, tn), lambda j: (0, j))],
        out_specs=pl.BlockSpec((nb, tn), lambda j: (0, j)),
        compiler_params=_cparams(("arbitrary",)),
        name="ada",
    )(c_all, w_ada, b_ada.reshape(1, n_out))


def _premix_kernel(x_ref, ada_ref, g_ref, win_ref, gqa_ref, wuq_ref, gkva_ref, wukv_ref, tab_ref,
                   qa_ref, ka_ref, va_ref, qd_ref, kd_ref, vd_ref):
    x = x_ref[...]
    sh1 = ada_ref[0, 0:1, :]
    sc1 = ada_ref[0, 1:2, :]
    h = _rms(x, g_ref[...]) * (1.0 + sc1) + sh1
    z = _dot(h.astype(BF16), win_ref[...])
    tab = tab_ref[...]
    cosq, sinq = tab[:, 0:128], tab[:, 128:256]
    cosk, sink = tab[:, 256:384], tab[:, 384:512]
    nh = N_HEADS * HEAD_PAD

    cq = _rms(z[:, 0:Q_LORA], gqa_ref[...]).astype(BF16)
    qq = _dot(cq, wuq_ref[...])
    for h_i in range(N_HEADS):
        lo = h_i * HEAD_PAD
        qa_ref[:, lo:lo + HEAD_PAD] = (qq[:, lo:lo + HEAD_PAD] * cosq
                                       + qq[:, nh + lo:nh + lo + HEAD_PAD] * sinq).astype(BF16)

    ckv = _rms(z[:, 256:384], gkva_ref[...]).astype(BF16)
    kk = _dot(ckv, wukv_ref[...])
    rr = z[:, 384:512] * cosk + z[:, 512:640] * sink
    for h_i in range(N_HEADS):
        lo = h_i * HEAD_PAD
        ka_ref[:, lo:lo + HEAD_PAD] = (kk[:, lo:lo + HEAD_PAD] + rr).astype(BF16)
    va_ref[...] = kk[:, nh:nh + 512].astype(BF16)

    lane = lax.broadcasted_iota(jnp.int32, (x.shape[0], LANES), 1)
    low = lane < HEAD_DIM
    for j in range(N_HEADS // 2):
        blk = z[:, 640 + j * LANES:640 + (j + 1) * LANES] * (HEAD_DIM ** -0.5)
        qd_ref[:, (2 * j) * LANES:(2 * j + 1) * LANES] = jnp.where(low, blk, 0.0).astype(BF16)
        qd_ref[:, (2 * j + 1) * LANES:(2 * j + 2) * LANES] = jnp.where(low, 0.0, blk).astype(BF16)
    kd_ref[...] = z[:, 1152:1664].astype(BF16)
    vd_ref[...] = z[:, 1664:2176].astype(BF16)


def _premix(x_all, ada8, chunk, g_pre, win2, g_qa, wuq2, g_kva, wukv2, tab, n1, s1, s2):
    t_all = x_all.shape[0]
    tm = TM_PRE
    nt1 = n1 // tm
    tb1, tb2 = s1 // tm, s2 // tm

    def tab_map(i):
        return (jnp.where(i < nt1, i % tb1, (i - nt1) % tb2), 0)

    row = lambda i: (i, 0)
    const = lambda i: (0, 0)
    outs = [jax.ShapeDtypeStruct((t_all, w), BF16) for w in (1024, 1024, 512, 1024, 512, 512)]
    return pl.pallas_call(
        _premix_kernel,
        out_shape=outs,
        grid=(t_all // tm,),
        in_specs=[pl.BlockSpec((tm, D_MODEL), row),
                  pl.BlockSpec((1, 6, D_MODEL), lambda i: (i * tm // chunk, 0, 0)),
                  pl.BlockSpec((1, D_MODEL), const),
                  pl.BlockSpec(win2.shape, const),
                  pl.BlockSpec((1, Q_LORA), const),
                  pl.BlockSpec(wuq2.shape, const),
                  pl.BlockSpec((1, KV_LORA), const),
                  pl.BlockSpec(wukv2.shape, const),
                  pl.BlockSpec((tm, 512), tab_map)],
        out_specs=[pl.BlockSpec((tm, w), row) for w in (1024, 1024, 512, 1024, 512, 512)],
        compiler_params=_cparams(("parallel",)),
        name="premix",
    )(x_all, ada8, g_pre, win2, g_qa, wuq2, g_kva, wukv2, tab)


def _mla_kernel(q_ref, k_ref, v_ref, o_ref, s_ref, p_ref, m_ref, l_ref, *, rb):
    tq = q_ref.shape[0]
    lane = lax.broadcasted_iota(jnp.int32, (tq, LANES), 1)

    def scores(hh):
        s = _dot_nt(q_ref[:, hh * HEAD_PAD:(hh + 1) * HEAD_PAD], k_ref[:, hh * HEAD_PAD:(hh + 1) * HEAD_PAD])
        s_ref[hh] = s
        m_ref[hh] = jnp.max(s, axis=-1, keepdims=True)

    def numerators(hh):
        for r0 in range(0, tq, rb):
            p = jnp.exp2(s_ref[hh, r0:r0 + rb, :] - m_ref[hh, r0:r0 + rb, :])
            l_ref[hh, r0:r0 + rb, :] = jnp.sum(p, axis=-1, keepdims=True)
            p_ref[hh, r0:r0 + rb, :] = p.astype(BF16)

    def values(hh):
        return _dot(p_ref[hh], v_ref[...]) / l_ref[hh]

    scores(0)
    scores(1)
    numerators(0)
    o0 = values(0)
    numerators(1)
    o1 = values(1)
    o_ref[...] = jnp.where(lane < HEAD_DIM, o0, o1).astype(o_ref.dtype)


def _mla(qa, ka, va, row_off, batch, seq):
    tq = TQ_MLA
    nq = seq // tq
    qoff = row_off // tq
    soff = row_off // seq
    return pl.pallas_call(
        functools.partial(_mla_kernel, rb=RB_MLA),
        out_shape=jax.ShapeDtypeStruct((batch * seq, N_HEADS * HEAD_DIM), BF16),
        grid=(batch, N_HEADS // 2, nq),
        in_specs=[pl.BlockSpec((tq, 2 * HEAD_PAD), lambda b, j, qi: (qoff + b * nq + qi, j)),
                  pl.BlockSpec((seq, 2 * HEAD_PAD), lambda b, j, qi: (soff + b, j)),
                  pl.BlockSpec((seq, LANES), lambda b, j, qi: (soff + b, j))],
        out_specs=pl.BlockSpec((tq, LANES), lambda b, j, qi: (b * nq + qi, j)),
        scratch_shapes=[pltpu.VMEM((2, tq, seq), F32), pltpu.VMEM((2, tq, seq), BF16),
                        pltpu.VMEM((2, tq, 1), F32), pltpu.VMEM((2, tq, 1), F32)],
        compiler_params=_cparams(("parallel", "parallel", "parallel")),
        name="mla_attention",
    )(qa, ka, va)


def _dil_kernel(q_ref, kp_ref, kc_ref, kn_ref, vp_ref, vc_ref, vn_ref, o_ref, lse_ref, *, dil, n1_rows, len1, len2):
    tq = q_ref.shape[0]
    half = DIL_HALF
    w = tq + 2 * half
    row0 = pl.program_id(0) * tq
    in1 = row0 < n1_rows
    seq_len = jnp.where(in1, len1, len2)
    q0 = jnp.where(in1, row0 % len1, (row0 - n1_rows) % len2)

    kw = jnp.concatenate([kp_ref[tq - half:tq, :], kc_ref[...], kn_ref[0:half, :]], axis=0)
    vw = jnp.concatenate([vp_ref[tq - half:tq, :], vc_ref[...], vn_ref[0:half, :]], axis=0)
    col = lax.broadcasted_iota(jnp.int32, (tq, w), 1)
    rw = lax.broadcasted_iota(jnp.int32, (tq, w), 0)
    absd = jnp.abs(col - half - rw)
    kpos = q0 + col - half
    valid = (absd <= half) & (kpos >= 0) & (kpos < seq_len)
    dist = (absd * dil).astype(F32)
    lane = lax.broadcasted_iota(jnp.int32, (tq, LANES), 1)
    low = lane < HEAD_DIM
    for j in range(N_HEADS // 2):
        kpair = kw[:, j * LANES:(j + 1) * LANES]
        vpair = vw[:, j * LANES:(j + 1) * LANES]
        res = []
        for hh in range(2):
            h_i = 2 * j + hh
            slope = 2.0 ** (-8.0 * (h_i + 1) / N_HEADS)
            s = _dot_nt(q_ref[:, h_i * LANES:(h_i + 1) * LANES], kpair)
            s = jnp.where(valid, s - slope * dist, NEG)
            m = jnp.max(s, axis=-1, keepdims=True)
            p = jnp.exp(s - m)
            l = jnp.sum(p, axis=-1, keepdims=True)
            res.append((_dot(p.astype(BF16), vpair) / l, m + jnp.log(l)))
        o_ref[:, j * LANES:(j + 1) * LANES] = jnp.where(low, res[0][0], res[1][0]).astype(o_ref.dtype)
        lse_ref[:, j * LANES:(j + 1) * LANES] = jnp.where(low, res[0][1], res[1][1])


def _dilated(qd, kd, vd, dil, n1, s1, s2):
    t_all = qd.shape[0]
    tq = TQ_DIL
    rows = t_all // dil
    nt = rows // tq
    wq = N_HEADS * LANES
    wk = N_HEADS * HEAD_DIM
    qv = qd.reshape(rows, dil * wq)
    kv = kd.reshape(rows, dil * wk)
    vv = vd.reshape(rows, dil * wk)
    cur = lambda i, r: (i, r)
    prv = lambda i, r: (jnp.maximum(i - 1, 0), r)
    nxt = lambda i, r: (jnp.minimum(i + 1, nt - 1), r)
    kspecs = [pl.BlockSpec((tq, wk), f) for f in (prv, cur, nxt)]
    o, lse = pl.pallas_call(
        functools.partial(_dil_kernel, dil=dil, n1_rows=n1 // dil, len1=s1 // dil, len2=s2 // dil),
        out_shape=[jax.ShapeDtypeStruct((rows, dil * wk), BF16), jax.ShapeDtypeStruct((rows, dil * wk), F32)],
        grid=(nt, dil),
        in_specs=[pl.BlockSpec((tq, wq), cur)] + kspecs + kspecs,
        out_specs=[pl.BlockSpec((tq, wk), cur), pl.BlockSpec((tq, wk), cur)],
        compiler_params=_cparams(("parallel", "parallel")),
        name="dilated_attention_d%d" % dil,
    )(qv, kv, kv, kv, vv, vv, vv)
    return o.reshape(t_all, wk), lse.reshape(t_all, wk)


def _postmix_kernel(x_ref, oa_ref, o1_ref, o2_ref, o3_ref, l1_ref, l2_ref, l3_ref, ada_ref, goa_ref, gob_ref,
                    wo_ref, gpm_ref, gpf_ref, rwt_ref, rb_ref, wsg_ref, wsu_ref, wsd_ref,
                    x1_ref, h2_ref, sh_ref, idx_ref, gate_ref, rank_ref, cnt_ref, carry_ref):
    tm = x_ref.shape[0]

    @pl.when(pl.program_id(0) == 0)
    def _():
        carry_ref[...] = jnp.zeros_like(carry_ref)

    gt1, sh2, sc2 = ada_ref[0, 2:3, :], ada_ref[0, 3:4, :], ada_ref[0, 4:5, :]

    la, lb, lc = l1_ref[...], l2_ref[...], l3_ref[...]
    mx = jnp.maximum(jnp.maximum(la, lb), lc)
    ea, eb, ec = jnp.exp(la - mx), jnp.exp(lb - mx), jnp.exp(lc - mx)
    ob = (ea * o1_ref[...].astype(F32) + eb * o2_ref[...].astype(F32) + ec * o3_ref[...].astype(F32)) / (ea + eb + ec)

    na = _rms(oa_ref[...].astype(F32), goa_ref[...]).astype(BF16)
    nb = _rms(ob, gob_ref[...]).astype(BF16)
    half_w = N_HEADS * HEAD_DIM
    o = _dot(na, wo_ref[0:half_w, :]) + _dot(nb, wo_ref[half_w:2 * half_w, :])
    x1 = x_ref[...] + gt1 * _rms(o, gpm_ref[...])
    x1_ref[...] = x1
    h2 = _rms(x1, gpf_ref[...]) * (1.0 + sc2) + sh2
    h2b = h2.astype(BF16)
    h2_ref[...] = h2b

    g = _dot(h2b, wsg_ref[...])
    u = _dot(h2b, wsu_ref[...])
    sh_ref[...] = _dot((g * _sigmoid(g) * u).astype(BF16), wsd_ref[...])

    scores = _sigmoid(_dot_nt(rwt_ref[...], h2, precision=lax.Precision.HIGHEST))
    biased = scores + rb_ref[...]
    ninf = -jnp.inf
    row = lax.broadcasted_iota(jnp.int32, (N_EXPERTS, tm), 0)
    rwg = lax.broadcasted_iota(jnp.int32, (GROUP_SIZE, tm), 0)
    gsc = []
    for gi in range(N_GROUPS):
        blk = biased[gi * GROUP_SIZE:(gi + 1) * GROUP_SIZE]
        m1 = jnp.max(blk, axis=0, keepdims=True)
        i1 = jnp.min(jnp.where(blk == m1, rwg, N_EXPERTS), axis=0, keepdims=True)
        m2 = jnp.max(jnp.where(rwg == i1, ninf, blk), axis=0, keepdims=True)
        gsc.append(m1 + m2)
    gsel = [jnp.zeros((1, tm), F32) for _ in range(N_GROUPS)]
    for _ in range(TOPK_GROUPS):
        m = functools.reduce(jnp.maximum, gsc)
        free = jnp.ones((1, tm), F32)
        for gi in range(N_GROUPS):
            hit = jnp.where(gsc[gi] == m, free, 0.0)
            free = free - hit
            gsel[gi] = gsel[gi] + hit
            gsc[gi] = jnp.where(hit > 0.0, ninf, gsc[gi])
    masked = jnp.concatenate(
        [jnp.where(gsel[gi] > 0.0, biased[gi * GROUP_SIZE:(gi + 1) * GROUP_SIZE], ninf)
         for gi in range(N_GROUPS)], axis=0)
    cur = masked
    idxs, gts = [], []
    for _ in range(TOP_K):
        m = jnp.max(cur, axis=0, keepdims=True)
        ik = jnp.min(jnp.where(cur == m, row, N_EXPERTS), axis=0, keepdims=True)
        hit = row == ik
        idxs.append(ik)
        gts.append(jnp.sum(jnp.where(hit, scores, 0.0), axis=0, keepdims=True))
        cur = jnp.where(hit, ninf, cur)
    idx = jnp.concatenate(idxs, axis=0)
    gates = jnp.concatenate(gts, axis=0)
    idx_ref[...] = idx
    gate_ref[...] = gates / jnp.sum(gates, axis=0, keepdims=True) * ROUTE_SCALE

    sel = jnp.where(cur != masked, 1.0, 0.0)
    tr = lax.broadcasted_iota(jnp.int32, (tm, tm), 0)
    tc = lax.broadcasted_iota(jnp.int32, (tm, tm), 1)
    before = jnp.where(tr < tc, 1.0, 0.0).astype(BF16)
    pos = _dot(sel.astype(BF16), before) + carry_ref[...]
    rank_ref[...] = jnp.concatenate(
        [jnp.sum(jnp.where(row == idxs[k], pos, 0.0), axis=0, keepdims=True) for k in range(TOP_K)],
        axis=0).astype(jnp.int32)
    carry = carry_ref[...] + jnp.sum(sel, axis=1, keepdims=True)
    carry_ref[...] = carry
    cnt_ref[...] = jnp.broadcast_to(carry, cnt_ref.shape)


def _postmix(x_all, oa, os_, ls_, ada8, chunk, g_oa, g_ob, wo, g_pm, g_pf, rwt, rb, wsg, wsu, wsd):
    t_all = x_all.shape[0]
    tm = TM_POST
    row = lambda i: (i, 0)
    col = lambda i: (0, i)
    const = lambda i: (0, 0)
    hw = N_HEADS * HEAD_DIM
    in_specs = ([pl.BlockSpec((tm, D_MODEL), row), pl.BlockSpec((tm, hw), row)]
                + [pl.BlockSpec((tm, hw), row)] * 6
                + [pl.BlockSpec((1, 6, D_MODEL), lambda i: (i * tm // chunk, 0, 0)),
                   pl.BlockSpec((1, hw), const), pl.BlockSpec((1, hw), const),
                   pl.BlockSpec(wo.shape, const),
                   pl.BlockSpec((1, D_MODEL), const), pl.BlockSpec((1, D_MODEL), const),
                   pl.BlockSpec(rwt.shape, const), pl.BlockSpec((N_EXPERTS, 1), const),
                   pl.BlockSpec(wsg.shape, const), pl.BlockSpec(wsu.shape, const), pl.BlockSpec(wsd.shape, const)])
    out_shape = [jax.ShapeDtypeStruct((t_all, D_MODEL), F32),
                 jax.ShapeDtypeStruct((t_all, D_MODEL), BF16),
                 jax.ShapeDtypeStruct((t_all, D_MODEL), F32),
                 jax.ShapeDtypeStruct((TOP_K, t_all), jnp.int32),
                 jax.ShapeDtypeStruct((TOP_K, t_all), F32),
                 jax.ShapeDtypeStruct((TOP_K, t_all), jnp.int32),
                 jax.ShapeDtypeStruct((N_EXPERTS, LANES), F32)]
    out_specs = [pl.BlockSpec((tm, D_MODEL), row), pl.BlockSpec((tm, D_MODEL), row), pl.BlockSpec((tm, D_MODEL), row),
                 pl.BlockSpec((TOP_K, tm), col), pl.BlockSpec((TOP_K, tm), col), pl.BlockSpec((TOP_K, tm), col),
                 pl.BlockSpec((N_EXPERTS, LANES), const)]
    return pl.pallas_call(
        _postmix_kernel,
        out_shape=out_shape,
        grid=(t_all // tm,),
        in_specs=in_specs,
        out_specs=out_specs,
        scratch_shapes=[pltpu.VMEM((N_EXPERTS, 1), F32)],
        compiler_params=_cparams(("arbitrary",)),
        name="postmix_router",
    )(x_all, oa, *os_, *ls_, ada8, g_oa, g_ob, wo, g_pm, g_pf, rwt, rb, wsg, wsu, wsd)


def _dest_kernel(idx_ref, rank_ref, start_ref, dest_ref):
    tm = idx_ref.shape[1]
    row = lax.broadcasted_iota(jnp.int32, (N_EXPERTS, tm), 0)
    start = start_ref[...]
    base = [jnp.sum(jnp.where(row == idx_ref[k:k + 1, :], start, 0), axis=0, keepdims=True) for k in range(TOP_K)]
    dest_ref[...] = jnp.concatenate(base, axis=0) + rank_ref[...]


def _dest(idx_t, rank_t, pad_start):
    t_all = idx_t.shape[1]
    tm = TM_DEST
    col = lambda i: (0, i)
    return pl.pallas_call(
        _dest_kernel,
        out_shape=jax.ShapeDtypeStruct((TOP_K, t_all), jnp.int32),
        grid=(t_all // tm,),
        in_specs=[pl.BlockSpec((TOP_K, tm), col), pl.BlockSpec((TOP_K, tm), col),
                  pl.BlockSpec((N_EXPERTS, 1), lambda i: (0, 0))],
        out_specs=pl.BlockSpec((TOP_K, tm), col),
        compiler_params=_cparams(("parallel",)),
        name="slot_index",
    )(idx_t, rank_t, pad_start.reshape(N_EXPERTS, 1))


def _expert_kernel(be_ref, nu_ref, x_ref, wg_ref, wu_ref, wd_ref, y_ref, wgu_s, wd_s):
    i = pl.program_id(0)
    used = i < nu_ref[0]
    new_expert = (i == 0) | (be_ref[i] != be_ref[jnp.maximum(i - 1, 0)])

    @pl.when(used & new_expert)
    def _():
        wgu_s[:, 0:D_EXPERT] = wg_ref[0].astype(BF16)
        wgu_s[:, D_EXPERT:2 * D_EXPERT] = wu_ref[0].astype(BF16)
        wd_s[...] = wd_ref[0].astype(BF16)

    @pl.when(used)
    def _():
        gu = _dot(x_ref[...], wgu_s[...])
        g, u = gu[:, 0:D_EXPERT], gu[:, D_EXPERT:2 * D_EXPERT]
        y_ref[...] = _dot((g * _sigmoid(g) * u).astype(BF16), wd_s[...]).astype(y_ref.dtype)

    @pl.when(jnp.logical_not(used))
    def _():
        y_ref[...] = jnp.zeros_like(y_ref)


def _experts(block_expert, n_used, x_slots, wg, wu, wd):
    p_rows = x_slots.shape[0]
    blk = EXPERT_BLOCK
    wmap = lambda i, be, nu: (be[i], 0, 0)
    grid_spec = pltpu.PrefetchScalarGridSpec(
        num_scalar_prefetch=2,
        grid=(p_rows // blk,),
        in_specs=[pl.BlockSpec((blk, D_MODEL), lambda i, be, nu: (jnp.minimum(i, nu[0] - 1), 0)),
                  pl.BlockSpec((1, D_MODEL, D_EXPERT), wmap),
                  pl.BlockSpec((1, D_MODEL, D_EXPERT), wmap),
                  pl.BlockSpec((1, D_EXPERT, D_MODEL), wmap)],
        out_specs=pl.BlockSpec((blk, D_MODEL), lambda i, be, nu: (i, 0)),
        scratch_shapes=[pltpu.VMEM((D_MODEL, 2 * D_EXPERT), BF16), pltpu.VMEM((D_EXPERT, D_MODEL), BF16)])
    return pl.pallas_call(
        _expert_kernel,
        out_shape=jax.ShapeDtypeStruct((p_rows, D_MODEL), BF16),
        grid_spec=grid_spec,
        compiler_params=_cparams(("arbitrary",)),
        name="expert_ffn",
    )(block_expert, n_used, x_slots, wg, wu, wd)


def _combine_kernel(yg_ref, gate_ref, sh_ref, x1_ref, ada_ref, g_ref, y_ref):
    gates = gate_ref[...]
    acc = sh_ref[...]
    for k in range(TOP_K):
        acc = acc + gates[:, k:k + 1] * yg_ref[k].astype(F32)
    gt2 = ada_ref[0, 5:6, :]
    y_ref[...] = x1_ref[...] + gt2 * _rms(acc, g_ref[...])


def _combine(yg, gates_t, sh, x1, ada8, chunk, g_post):
    t_all = x1.shape[0]
    tm = TM_COMB
    row = lambda i: (i, 0)
    return pl.pallas_call(
        _combine_kernel,
        out_shape=jax.ShapeDtypeStruct((t_all, D_MODEL), F32),
        grid=(t_all // tm,),
        in_specs=[pl.BlockSpec((TOP_K, tm, D_MODEL), lambda i: (0, i, 0)),
                  pl.BlockSpec((tm, TOP_K), row),
                  pl.BlockSpec((tm, D_MODEL), row),
                  pl.BlockSpec((tm, D_MODEL), row),
                  pl.BlockSpec((1, 6, D_MODEL), lambda i: (i * tm // chunk, 0, 0)),
                  pl.BlockSpec((1, D_MODEL), lambda i: (0, 0))],
        out_specs=pl.BlockSpec((tm, D_MODEL), row),
        compiler_params=_cparams(("parallel",)),
        name="moe_combine",
    )(yg, gates_t, sh, x1, ada8, g_post)


def _rope_partner(w):
    half = QK_ROPE // 2
    return jnp.concatenate([-w[..., half:], w[..., :half]], axis=-1)


def _prep_weights(w_in, w_uq, w_ukv):
    d = w_in.shape[0]
    zeros = lambda r, c: jnp.zeros((r, c), F32)
    kr = w_in[:, 384:416]
    r_main = jnp.concatenate([zeros(d, QK_NOPE), kr, zeros(d, HEAD_PAD - QK_NOPE - QK_ROPE)], axis=1)
    r_part = jnp.concatenate([zeros(d, QK_NOPE), _rope_partner(kr), zeros(d, HEAD_PAD - QK_NOPE - QK_ROPE)], axis=1)
    win2 = jnp.concatenate([w_in[:, :384], r_main, r_part, w_in[:, 416:]], axis=1).astype(BF16)

    wq = w_uq.reshape(Q_LORA, N_HEADS, QK_NOPE + QK_ROPE)
    zq = jnp.zeros((Q_LORA, N_HEADS, HEAD_PAD - QK_NOPE - QK_ROPE), F32)
    q_main = jnp.concatenate([wq, zq], axis=-1).reshape(Q_LORA, N_HEADS * HEAD_PAD)
    q_part = jnp.concatenate([jnp.zeros((Q_LORA, N_HEADS, QK_NOPE), F32), _rope_partner(wq[..., QK_NOPE:]), zq],
                             axis=-1).reshape(Q_LORA, N_HEADS * HEAD_PAD)
    wuq2 = jnp.concatenate([q_main, q_part], axis=1).astype(BF16)

    wkv = w_ukv.reshape(KV_LORA, N_HEADS, QK_NOPE + HEAD_DIM)
    k_pad = jnp.concatenate([wkv[..., :QK_NOPE], jnp.zeros((KV_LORA, N_HEADS, HEAD_PAD - QK_NOPE), F32)],
                            axis=-1).reshape(KV_LORA, N_HEADS * HEAD_PAD)
    v_cols = wkv[..., QK_NOPE:].reshape(KV_LORA, N_HEADS * HEAD_DIM)
    wukv2 = jnp.concatenate([k_pad, v_cols], axis=1).astype(BF16)
    return win2, wuq2, wukv2


def _rope_table(s_max):
    half = QK_ROPE // 2
    inv = ROPE_BASE ** (-jnp.arange(half, dtype=F32) / half)
    ang = jnp.arange(s_max, dtype=F32)[:, None] * inv[None, :]
    cos = jnp.concatenate([jnp.cos(ang), jnp.cos(ang)], axis=1)
    sin = jnp.concatenate([jnp.sin(ang), jnp.sin(ang)], axis=1)
    scale = (QK_NOPE + QK_ROPE) ** -0.5 * math.log2(math.e)
    pad = jnp.zeros((s_max, HEAD_PAD - QK_NOPE - QK_ROPE), F32)
    zn = jnp.zeros((s_max, QK_NOPE), F32)
    cosq = jnp.concatenate([jnp.full((s_max, QK_NOPE), scale, F32), cos * scale, pad], axis=1)
    sinq = jnp.concatenate([zn, sin * scale, pad], axis=1)
    cosk = jnp.concatenate([zn, cos, pad], axis=1)
    sink = jnp.concatenate([zn, sin, pad], axis=1)
    return jnp.concatenate([cosq, sinq, cosk, sink], axis=1)


def _layer(x_prompt, x_sample, c_prompt, c_sample, w_ada, b_ada, g_pre_mix, w_in, g_qa, w_uq, g_kva, w_ukv,
           g_out_a, g_out_b, w_o, g_post_mix, g_pre_ffn, router_w, router_bias, w_exp_gate, w_exp_up, w_exp_down,
           w_sh_gate, w_sh_up, w_sh_down, g_post_ffn):
    b1, s1, d = x_prompt.shape
    b2, s2, _ = x_sample.shape
    n1, n2 = b1 * s1, b2 * s2
    t_all = n1 + n2
    chunk = math.gcd(s1, s2)
    assert n1 % s2 == 0 and chunk % max(TM_PRE, TM_POST, TM_COMB) == 0

    x_all = jnp.concatenate([x_prompt.reshape(n1, d), x_sample.reshape(n2, d)], axis=0)
    c_all = jnp.concatenate([c_prompt, c_sample, jnp.zeros((8 - b1 - b2, d), F32)], axis=0)
    ada = _ada(c_all, w_ada, b_ada)
    chunk_batch = jnp.concatenate([jnp.repeat(jnp.arange(b1), s1 // chunk),
                                   b1 + jnp.repeat(jnp.arange(b2), s2 // chunk)])
    ada8 = ada.reshape(8, 6, d)[chunk_batch]

    win2, wuq2, wukv2 = _prep_weights(w_in, w_uq, w_ukv)
    tab = _rope_table(max(s1, s2))
    r2 = lambda g: g.reshape(1, -1)
    qa, ka, va, qd, kd, vd = _premix(x_all, ada8, chunk, r2(g_pre_mix), win2, r2(g_qa), wuq2, r2(g_kva), wukv2, tab,
                                     n1, s1, s2)

    oa = jnp.concatenate([_mla(qa, ka, va, 0, b1, s1), _mla(qa, ka, va, n1, b2, s2)], axis=0)
    os_, ls_ = [], []
    for _, dil in DIL_PATTERNS:
        o, lse = _dilated(qd, kd, vd, dil, n1, s1, s2)
        os_.append(o)
        ls_.append(lse)

    x1, h2b, sh, idx_t, gate_t, rank_t, cnt = _postmix(
        x_all, oa, os_, ls_, ada8, chunk, r2(g_out_a), r2(g_out_b), w_o.astype(BF16), r2(g_post_mix), r2(g_pre_ffn),
        router_w.T, router_bias.reshape(N_EXPERTS, 1), w_sh_gate.astype(BF16), w_sh_up.astype(BF16),
        w_sh_down.astype(BF16))

    blk = EXPERT_BLOCK
    n_assign = t_all * TOP_K
    n_blocks = -(-(n_assign + N_EXPERTS * (blk - 1)) // blk)
    counts = cnt[:, 0].astype(jnp.int32)
    padded = (counts + blk - 1) // blk * blk
    pad_end = jnp.cumsum(padded)
    pad_start = pad_end - padded
    dest = _dest(idx_t, rank_t, pad_start)
    tok = jnp.broadcast_to(jnp.arange(t_all, dtype=jnp.int32)[None, :], (TOP_K, t_all))
    slot_tok = (jnp.arange(n_blocks * blk, dtype=jnp.int32) % t_all).at[dest.reshape(-1)].set(tok.reshape(-1))
    block_first = jnp.arange(n_blocks, dtype=jnp.int32) * blk
    block_expert = jnp.minimum(jnp.sum((pad_end[None, :] <= block_first[:, None]).astype(jnp.int32), axis=1),
                               N_EXPERTS - 1)
    n_used = (pad_end[-1] // blk).astype(jnp.int32).reshape(1)
    x_slots = h2b[slot_tok]
    y_slots = _experts(block_expert, n_used, x_slots, w_exp_gate, w_exp_up, w_exp_down)
    yg = y_slots[dest]

    y = _combine(yg, gate_t.T, sh, x1, ada8, chunk, r2(g_post_ffn))
    return y[:n1].reshape(b1, s1, d), y[n1:].reshape(b2, s2, d)


def kernel(x_prompt, x_sample, c_prompt, c_sample, w_ada, b_ada, g_pre_mix, w_in, g_qa, w_uq, g_kva, w_ukv, g_out_a, g_out_b, w_o, g_post_mix, g_pre_ffn, router_w, router_bias, w_exp_gate, w_exp_up, w_exp_down, w_sh_gate, w_sh_up, w_sh_down, g_post_ffn):
    layer = [p[0] for p in (w_ada, b_ada, g_pre_mix, w_in, g_qa, w_uq, g_kva, w_ukv, g_out_a, g_out_b, w_o,
                            g_post_mix, g_pre_ffn, router_w, router_bias, w_exp_gate, w_exp_up, w_exp_down,
                            w_sh_gate, w_sh_up, w_sh_down, g_post_ffn)]
    return _layer(x_prompt, x_sample, c_prompt, c_sample, *layer)
```

```python
import functools
import math

import jax
import jax.numpy as jnp
from jax import lax
from jax.experimental import pallas as pl
from jax.experimental.pallas import tpu as pltpu
from jax.experimental.pallas import tpu_sc as plsc

F32 = jnp.float32
BF16 = jnp.bfloat16

D_MODEL = 1024
HEAD_DIM = 64
N_HEADS = 8
Q_LORA = 256
KV_LORA = 128
QK_NOPE = 64
QK_ROPE = 32
ROPE_BASE = 10000.0
DIL_PATTERNS = ((128, 1), (512, 4), (2048, 16))
DIL_HALF = 64
N_EXPERTS = 256
TOP_K = 8
N_GROUPS = 8
GROUP_SIZE = N_EXPERTS // N_GROUPS
TOPK_GROUPS = 4
D_EXPERT = 256
ROUTE_SCALE = 2.5
EPS = 1e-6
NEG = -1e30

LANES = 128
HEAD_PAD = 128

TM_PRE = 512
TM_POST = 512
TM_COMB = 512
TQ_MLA = 256
RB_MLA = 64
TM_DEST = 2048
SC_WINDOW = 128
TQ_DIL = 128
EXPERT_BLOCK = 256
VMEM_LIMIT = 56 * 1024 * 1024


def _cparams(sem):
    return pltpu.CompilerParams(dimension_semantics=sem, vmem_limit_bytes=VMEM_LIMIT)


def _rms(x, g):
    return x * lax.rsqrt(jnp.mean(x * x, axis=-1, keepdims=True) + EPS) * g


def _sigmoid(x):
    return 1.0 / (1.0 + jnp.exp(-x))


def _dot(a, b):
    return jnp.dot(a, b, preferred_element_type=F32)


def _pack_rows(x):
    n = x.shape[1] // 2
    bits = lax.bitcast_convert_type(x.astype(BF16).astype(F32), jnp.int32)
    return bits[:, :n] | lax.shift_right_logical(bits[:, n:], 16)


def _unpack_rows(u):
    hi = lax.bitcast_convert_type(u & jnp.int32(-65536), F32)
    lo = lax.bitcast_convert_type(lax.shift_left(u, 16), F32)
    return hi, lo


def _dot_nt(a, b, precision=None):
    return lax.dot_general(a, b, (((1,), (1,)), ((), ())), preferred_element_type=F32, precision=precision)


def _ada_kernel(c_ref, w_ref, b_ref, o_ref):
    c = c_ref[...]
    s = c * _sigmoid(c)
    o_ref[...] = jnp.dot(s, w_ref[...], preferred_element_type=F32, precision=lax.Precision.HIGHEST) + b_ref[...]


def _ada(c_all, w_ada, b_ada):
    nb, d = c_all.shape
    n_out = w_ada.shape[1]
    tn = 1024
    return pl.pallas_call(
        _ada_kernel,
        out_shape=jax.ShapeDtypeStruct((nb, n_out), F32),
        grid=(n_out // tn,),
        in_specs=[pl.BlockSpec((nb, d), lambda j: (0, 0)),
                  pl.BlockSpec((d, tn), lambda j: (0, j)),
                  pl.BlockSpec((1, tn), lambda j: (0, j))],
        out_specs=pl.BlockSpec((nb, tn), lambda j: (0, j)),
        compiler_params=_cparams(("arbitrary",)),
        name="ada",
    )(c_all, w_ada, b_ada.reshape(1, n_out))


def _premix_kernel(x_ref, ada_ref, g_ref, win_ref, gqa_ref, wuq_ref, gkva_ref, wukv_ref, tab_ref,
                   qa_ref, ka_ref, va_ref, qd_ref, kd_ref, vd_ref):
    x = x_ref[...]
    sh1 = ada_ref[0, 0:1, :]
    sc1 = ada_ref[0, 1:2, :]
    h = _rms(x, g_ref[...]) * (1.0 + sc1) + sh1
    z = _dot(h.astype(BF16), win_ref[...])
    tab = tab_ref[...]
    cosq, sinq = tab[:, 0:128], tab[:, 128:256]
    cosk, sink = tab[:, 256:384], tab[:, 384:512]
    nh = N_HEADS * HEAD_PAD

    cq = _rms(z[:, 0:Q_LORA], gqa_ref[...]).astype(BF16)
    qq = _dot(cq, wuq_ref[...])
    for h_i in range(N_HEADS):
        lo = h_i * HEAD_PAD
        qa_ref[:, lo:lo + HEAD_PAD] = (qq[:, lo:lo + HEAD_PAD] * cosq
                                       + qq[:, nh + lo:nh + lo + HEAD_PAD] * sinq).astype(BF16)

    ckv = _rms(z[:, 256:384], gkva_ref[...]).astype(BF16)
    kk = _dot(ckv, wukv_ref[...])
    rr = z[:, 384:512] * cosk + z[:, 512:640] * sink
    for h_i in range(N_HEADS):
        lo = h_i * HEAD_PAD
        ka_ref[:, lo:lo + HEAD_PAD] = (kk[:, lo:lo + HEAD_PAD] + rr).astype(BF16)
    va_ref[...] = kk[:, nh:nh + 512].astype(BF16)

    lane = lax.broadcasted_iota(jnp.int32, (x.shape[0], LANES), 1)
    low = lane < HEAD_DIM
    for j in range(N_HEADS // 2):
        blk = z[:, 640 + j * LANES:640 + (j + 1) * LANES] * (HEAD_DIM ** -0.5)
        qd_ref[:, (2 * j) * LANES:(2 * j + 1) * LANES] = jnp.where(low, blk, 0.0).astype(BF16)
        qd_ref[:, (2 * j + 1) * LANES:(2 * j + 2) * LANES] = jnp.where(low, 0.0, blk).astype(BF16)
    kd_ref[...] = z[:, 1152:1664].astype(BF16)
    vd_ref[...] = z[:, 1664:2176].astype(BF16)


def _premix(x_all, ada8, chunk, g_pre, win2, g_qa, wuq2, g_kva, wukv2, tab, n1, s1, s2):
    t_all = x_all.shape[0]
    tm = TM_PRE
    nt1 = n1 // tm
    tb1, tb2 = s1 // tm, s2 // tm

    def tab_map(i):
        return (jnp.where(i < nt1, i % tb1, (i - nt1) % tb2), 0)

    row = lambda i: (i, 0)
    const = lambda i: (0, 0)
    outs = [jax.ShapeDtypeStruct((t_all, w), BF16) for w in (1024, 1024, 512, 1024, 512, 512)]
    return pl.pallas_call(
        _premix_kernel,
        out_shape=outs,
        grid=(t_all // tm,),
        in_specs=[pl.BlockSpec((tm, D_MODEL), row),
                  pl.BlockSpec((1, 6, D_MODEL), lambda i: (i * tm // chunk, 0, 0)),
                  pl.BlockSpec((1, D_MODEL), const),
                  pl.BlockSpec(win2.shape, const),
                  pl.BlockSpec((1, Q_LORA), const),
                  pl.BlockSpec(wuq2.shape, const),
                  pl.BlockSpec((1, KV_LORA), const),
                  pl.BlockSpec(wukv2.shape, const),
                  pl.BlockSpec((tm, 512), tab_map)],
        out_specs=[pl.BlockSpec((tm, w), row) for w in (1024, 1024, 512, 1024, 512, 512)],
        compiler_params=_cparams(("parallel",)),
        name="premix",
    )(x_all, ada8, g_pre, win2, g_qa, wuq2, g_kva, wukv2, tab)


def _mla_kernel(q_ref, k_ref, v_ref, o_ref, s_ref, p_ref, m_ref, l_ref, *, rb):
    tq = q_ref.shape[0]
    lane = lax.broadcasted_iota(jnp.int32, (tq, LANES), 1)

    def scores(hh):
        s = _dot_nt(q_ref[:, hh * HEAD_PAD:(hh + 1) * HEAD_PAD], k_ref[:, hh * HEAD_PAD:(hh + 1) * HEAD_PAD])
        s_ref[hh] = s
        m_ref[hh] = jnp.max(s, axis=-1, keepdims=True)

    def numerators(hh):
        for r0 in range(0, tq, rb):
            p = jnp.exp2(s_ref[hh, r0:r0 + rb, :] - m_ref[hh, r0:r0 + rb, :])
            l_ref[hh, r0:r0 + rb, :] = jnp.sum(p, axis=-1, keepdims=True)
            p_ref[hh, r0:r0 + rb, :] = p.astype(BF16)

    def values(hh):
        return _dot(p_ref[hh], v_ref[...]) / l_ref[hh]

    scores(0)
    scores(1)
    numerators(0)
    o0 = values(0)
    numerators(1)
    o1 = values(1)
    o_ref[...] = jnp.where(lane < HEAD_DIM, o0, o1).astype(o_ref.dtype)


def _mla(qa, ka, va, row_off, batch, seq):
    tq = TQ_MLA
    nq = seq // tq
    qoff = row_off // tq
    soff = row_off // seq
    return pl.pallas_call(
        functools.partial(_mla_kernel, rb=RB_MLA),
        out_shape=jax.ShapeDtypeStruct((batch * seq, N_HEADS * HEAD_DIM), BF16),
        grid=(batch, N_HEADS // 2, nq),
        in_specs=[pl.BlockSpec((tq, 2 * HEAD_PAD), lambda b, j, qi: (qoff + b * nq + qi, j)),
                  pl.BlockSpec((seq, 2 * HEAD_PAD), lambda b, j, qi: (soff + b, j)),
                  pl.BlockSpec((seq, LANES), lambda b, j, qi: (soff + b, j))],
        out_specs=pl.BlockSpec((tq, LANES), lambda b, j, qi: (b * nq + qi, j)),
        scratch_shapes=[pltpu.VMEM((2, tq, seq), F32), pltpu.VMEM((2, tq, seq), BF16),
                        pltpu.VMEM((2, tq, 1), F32), pltpu.VMEM((2, tq, 1), F32)],
        compiler_params=_cparams(("parallel", "parallel", "parallel")),
        name="mla_attention",
    )(qa, ka, va)


def _dil_kernel(q_ref, kp_ref, kc_ref, kn_ref, vp_ref, vc_ref, vn_ref, o_ref, lse_ref, *, dil, n1_rows, len1, len2):
    tq = q_ref.shape[0]
    half = DIL_HALF
    w = tq + 2 * half
    row0 = pl.program_id(0) * tq
    in1 = row0 < n1_rows
    seq_len = jnp.where(in1, len1, len2)
    q0 = jnp.where(in1, row0 % len1, (row0 - n1_rows) % len2)

    kw = jnp.concatenate([kp_ref[tq - half:tq, :], kc_ref[...], kn_ref[0:half, :]], axis=0)
    vw = jnp.concatenate([vp_ref[tq - half:tq, :], vc_ref[...], vn_ref[0:half, :]], axis=0)
    col = lax.broadcasted_iota(jnp.int32, (tq, w), 1)
    rw = lax.broadcasted_iota(jnp.int32, (tq, w), 0)
    absd = jnp.abs(col - half - rw)
    kpos = q0 + col - half
    valid = (absd <= half) & (kpos >= 0) & (kpos < seq_len)
    dist = (absd * dil).astype(F32)
    lane = lax.broadcasted_iota(jnp.int32, (tq, LANES), 1)
    low = lane < HEAD_DIM
    for j in range(N_HEADS // 2):
        kpair = kw[:, j * LANES:(j + 1) * LANES]
        vpair = vw[:, j * LANES:(j + 1) * LANES]
        res = []
        for hh in range(2):
            h_i = 2 * j + hh
            slope = 2.0 ** (-8.0 * (h_i + 1) / N_HEADS)
            s = _dot_nt(q_ref[:, h_i * LANES:(h_i + 1) * LANES], kpair)
            s = jnp.where(valid, s - slope * dist, NEG)
            m = jnp.max(s, axis=-1, keepdims=True)
            p = jnp.exp(s - m)
            l = jnp.sum(p, axis=-1, keepdims=True)
            res.append((_dot(p.astype(BF16), vpair) / l, m + jnp.log(l)))
        o_ref[:, j * LANES:(j + 1) * LANES] = jnp.where(low, res[0][0], res[1][0]).astype(o_ref.dtype)
        lse_ref[:, j * LANES:(j + 1) * LANES] = jnp.where(low, res[0][1], res[1][1])


def _dilated(qd, kd, vd, dil, n1, s1, s2):
    t_all = qd.shape[0]
    tq = TQ_DIL
    rows = t_all // dil
    nt = rows // tq
    wq = N_HEADS * LANES
    wk = N_HEADS * HEAD_DIM
    qv = qd.reshape(rows, dil * wq)
    kv = kd.reshape(rows, dil * wk)
    vv = vd.reshape(rows, dil * wk)
    cur = lambda i, r: (i, r)
    prv = lambda i, r: (jnp.maximum(i - 1, 0), r)
    nxt = lambda i, r: (jnp.minimum(i + 1, nt - 1), r)
    kspecs = [pl.BlockSpec((tq, wk), f) for f in (prv, cur, nxt)]
    o, lse = pl.pallas_call(
        functools.partial(_dil_kernel, dil=dil, n1_rows=n1 // dil, len1=s1 // dil, len2=s2 // dil),
        out_shape=[jax.ShapeDtypeStruct((rows, dil * wk), BF16), jax.ShapeDtypeStruct((rows, dil * wk), F32)],
        grid=(nt, dil),
        in_specs=[pl.BlockSpec((tq, wq), cur)] + kspecs + kspecs,
        out_specs=[pl.BlockSpec((tq, wk), cur), pl.BlockSpec((tq, wk), cur)],
        compiler_params=_cparams(("parallel", "parallel")),
        name="dilated_attention_d%d" % dil,
    )(qv, kv, kv, kv, vv, vv, vv)
    return o.reshape(t_all, wk), lse.reshape(t_all, wk)


def _postmix_kernel(x_ref, oa_ref, o1_ref, o2_ref, o3_ref, l1_ref, l2_ref, l3_ref, ada_ref, goa_ref, gob_ref,
                    wo_ref, gpm_ref, gpf_ref, rwt_ref, rb_ref, wsg_ref, wsu_ref, wsd_ref,
                    x1_ref, h2_ref, sh_ref, idx_ref, gate_ref, rank_ref, cnt_ref, carry_ref):
    tm = x_ref.shape[0]

    @pl.when(pl.program_id(0) == 0)
    def _():
        carry_ref[...] = jnp.zeros_like(carry_ref)

    gt1, sh2, sc2 = ada_ref[0, 2:3, :], ada_ref[0, 3:4, :], ada_ref[0, 4:5, :]

    la, lb, lc = l1_ref[...], l2_ref[...], l3_ref[...]
    mx = jnp.maximum(jnp.maximum(la, lb), lc)
    ea, eb, ec = jnp.exp(la - mx), jnp.exp(lb - mx), jnp.exp(lc - mx)
    ob = (ea * o1_ref[...].astype(F32) + eb * o2_ref[...].astype(F32) + ec * o3_ref[...].astype(F32)) / (ea + eb + ec)

    na = _rms(oa_ref[...].astype(F32), goa_ref[...]).astype(BF16)
    nb = _rms(ob, gob_ref[...]).astype(BF16)
    half_w = N_HEADS * HEAD_DIM
    o = _dot(na, wo_ref[0:half_w, :]) + _dot(nb, wo_ref[half_w:2 * half_w, :])
    x1 = x_ref[...] + gt1 * _rms(o, gpm_ref[...])
    x1_ref[...] = x1
    h2 = _rms(x1, gpf_ref[...]) * (1.0 + sc2) + sh2
    h2b = h2.astype(BF16)
    h2_ref[...] = _pack_rows(h2)

    g = _dot(h2b, wsg_ref[...])
    u = _dot(h2b, wsu_ref[...])
    sh_ref[...] = _dot((g * _sigmoid(g) * u).astype(BF16), wsd_ref[...])

    scores = _sigmoid(_dot_nt(rwt_ref[...], h2, precision=lax.Precision.HIGHEST))
    biased = scores + rb_ref[...]
    ninf = -jnp.inf
    row = lax.broadcasted_iota(jnp.int32, (N_EXPERTS, tm), 0)
    rwg = lax.broadcasted_iota(jnp.int32, (GROUP_SIZE, tm), 0)
    gsc = []
    for gi in range(N_GROUPS):
        blk = biased[gi * GROUP_SIZE:(gi + 1) * GROUP_SIZE]
        m1 = jnp.max(blk, axis=0, keepdims=True)
        i1 = jnp.min(jnp.where(blk == m1, rwg, N_EXPERTS), axis=0, keepdims=True)
        m2 = jnp.max(jnp.where(rwg == i1, ninf, blk), axis=0, keepdims=True)
        gsc.append(m1 + m2)
    gsel = [jnp.zeros((1, tm), F32) for _ in range(N_GROUPS)]
    for _ in range(TOPK_GROUPS):
        m = functools.reduce(jnp.maximum, gsc)
        free = jnp.ones((1, tm), F32)
        for gi in range(N_GROUPS):
            hit = jnp.where(gsc[gi] == m, free, 0.0)
            free = free - hit
            gsel[gi] = gsel[gi] + hit
            gsc[gi] = jnp.where(hit > 0.0, ninf, gsc[gi])
    masked = jnp.concatenate(
        [jnp.where(gsel[gi] > 0.0, biased[gi * GROUP_SIZE:(gi + 1) * GROUP_SIZE], ninf)
         for gi in range(N_GROUPS)], axis=0)
    cur = masked
    idxs, gts = [], []
    for _ in range(TOP_K):
        m = jnp.max(cur, axis=0, keepdims=True)
        ik = jnp.min(jnp.where(cur == m, row, N_EXPERTS), axis=0, keepdims=True)
        hit = row == ik
        idxs.append(ik)
        gts.append(jnp.sum(jnp.where(hit, scores, 0.0), axis=0, keepdims=True))
        cur = jnp.where(hit, ninf, cur)
    idx = jnp.concatenate(idxs, axis=0)
    gates = jnp.concatenate(gts, axis=0)
    idx_ref[...] = idx
    gate_ref[...] = gates / jnp.sum(gates, axis=0, keepdims=True) * ROUTE_SCALE

    sel = jnp.where(cur != masked, 1.0, 0.0)
    tr = lax.broadcasted_iota(jnp.int32, (tm, tm), 0)
    tc = lax.broadcasted_iota(jnp.int32, (tm, tm), 1)
    before = jnp.where(tr < tc, 1.0, 0.0).astype(BF16)
    pos = _dot(sel.astype(BF16), before) + carry_ref[...]
    rank_ref[...] = jnp.concatenate(
        [jnp.sum(jnp.where(row == idxs[k], pos, 0.0), axis=0, keepdims=True) for k in range(TOP_K)],
        axis=0).astype(jnp.int32)
    carry = carry_ref[...] + jnp.sum(sel, axis=1, keepdims=True)
    carry_ref[...] = carry
    cnt_ref[...] = jnp.broadcast_to(carry, cnt_ref.shape)


def _postmix(x_all, oa, os_, ls_, ada8, chunk, g_oa, g_ob, wo, g_pm, g_pf, rwt, rb, wsg, wsu, wsd):
    t_all = x_all.shape[0]
    tm = TM_POST
    row = lambda i: (i, 0)
    col = lambda i: (0, i)
    const = lambda i: (0, 0)
    hw = N_HEADS * HEAD_DIM
    in_specs = ([pl.BlockSpec((tm, D_MODEL), row), pl.BlockSpec((tm, hw), row)]
                + [pl.BlockSpec((tm, hw), row)] * 6
                + [pl.BlockSpec((1, 6, D_MODEL), lambda i: (i * tm // chunk, 0, 0)),
                   pl.BlockSpec((1, hw), const), pl.BlockSpec((1, hw), const),
                   pl.BlockSpec(wo.shape, const),
                   pl.BlockSpec((1, D_MODEL), const), pl.BlockSpec((1, D_MODEL), const),
                   pl.BlockSpec(rwt.shape, const), pl.BlockSpec((N_EXPERTS, 1), const),
                   pl.BlockSpec(wsg.shape, const), pl.BlockSpec(wsu.shape, const), pl.BlockSpec(wsd.shape, const)])
    out_shape = [jax.ShapeDtypeStruct((t_all, D_MODEL), F32),
                 jax.ShapeDtypeStruct((t_all, D_MODEL // 2), jnp.int32),
                 jax.ShapeDtypeStruct((t_all, D_MODEL), F32),
                 jax.ShapeDtypeStruct((TOP_K, t_all), jnp.int32),
                 jax.ShapeDtypeStruct((TOP_K, t_all), F32),
                 jax.ShapeDtypeStruct((TOP_K, t_all), jnp.int32),
                 jax.ShapeDtypeStruct((N_EXPERTS, LANES), F32)]
    out_specs = [pl.BlockSpec((tm, D_MODEL), row), pl.BlockSpec((tm, D_MODEL // 2), row), pl.BlockSpec((tm, D_MODEL), row),
                 pl.BlockSpec((TOP_K, tm), col), pl.BlockSpec((TOP_K, tm), col), pl.BlockSpec((TOP_K, tm), col),
                 pl.BlockSpec((N_EXPERTS, LANES), const)]
    return pl.pallas_call(
        _postmix_kernel,
        out_shape=out_shape,
        grid=(t_all // tm,),
        in_specs=in_specs,
        out_specs=out_specs,
        scratch_shapes=[pltpu.VMEM((N_EXPERTS, 1), F32)],
        compiler_params=_cparams(("arbitrary",)),
        name="postmix_router",
    )(x_all, oa, *os_, *ls_, ada8, g_oa, g_ob, wo, g_pm, g_pf, rwt, rb, wsg, wsu, wsd)


def _dest_kernel(idx_ref, rank_ref, start_ref, dest_ref):
    tm = idx_ref.shape[1]
    row = lax.broadcasted_iota(jnp.int32, (N_EXPERTS, tm), 0)
    start = start_ref[...]
    base = [jnp.sum(jnp.where(row == idx_ref[k:k + 1, :], start, 0), axis=0, keepdims=True) for k in range(TOP_K)]
    dest_ref[...] = jnp.concatenate(base, axis=0) + rank_ref[...]


def _dest(idx_t, rank_t, pad_start):
    t_all = idx_t.shape[1]
    tm = TM_DEST
    col = lambda i: (0, i)
    return pl.pallas_call(
        _dest_kernel,
        out_shape=jax.ShapeDtypeStruct((TOP_K, t_all), jnp.int32),
        grid=(t_all // tm,),
        in_specs=[pl.BlockSpec((TOP_K, tm), col), pl.BlockSpec((TOP_K, tm), col),
                  pl.BlockSpec((N_EXPERTS, 1), lambda i: (0, 0))],
        out_specs=pl.BlockSpec((TOP_K, tm), col),
        compiler_params=_cparams(("parallel",)),
        name="slot_index",
    )(idx_t, rank_t, pad_start.reshape(N_EXPERTS, 1))


def _sc_mesh():
    return plsc.VectorSubcoreMesh(core_axis_name="core", subcore_axis_name="subcore")


def _dispatch_rows(rows, dest, n_slots):
    t_all, width = rows.shape
    win = SC_WINDOW
    info = plsc.get_sparse_core_info()
    n_workers = info.num_cores * info.num_subcores
    per_worker = t_all // n_workers
    assert per_worker % win == 0

    @functools.partial(
        pl.kernel, out_type=jax.ShapeDtypeStruct((n_slots, width), rows.dtype), mesh=_sc_mesh(),
        scratch_types=[pltpu.VMEM((TOP_K, win), jnp.int32), pltpu.VMEM((win, width), rows.dtype),
                       pltpu.SemaphoreType.DMA])
    def scatter_kernel(x_hbm, i_hbm, o_hbm, idx_v, rows_v, sem):
        base = (lax.axis_index("subcore") * info.num_cores + lax.axis_index("core")) * per_worker

        @pl.loop(0, per_worker // win)
        def _(w):
            off = base + w * win
            for k in range(TOP_K):
                pltpu.sync_copy(i_hbm.at[pl.ds(k * t_all + off, win)], idx_v.at[k])
            pltpu.sync_copy(x_hbm.at[pl.ds(off, win)], rows_v)
            copies = [pltpu.async_copy(rows_v, o_hbm.at[idx_v.at[k]], sem) for k in range(TOP_K)]
            for c in copies:
                c.wait()

    return scatter_kernel(rows, dest)


def _gather_rows(slots, index):
    n = index.shape[0]
    width = slots.shape[1]
    win = SC_WINDOW
    info = plsc.get_sparse_core_info()
    n_workers = info.num_cores * info.num_subcores
    per_worker = n // n_workers
    assert per_worker % win == 0

    @functools.partial(
        pl.kernel, out_type=jax.ShapeDtypeStruct((n, width), slots.dtype), mesh=_sc_mesh(),
        scratch_types=[pltpu.VMEM((win,), jnp.int32), pltpu.VMEM((win, width), slots.dtype),
                       pltpu.SemaphoreType.DMA])
    def gather_kernel(y_hbm, i_hbm, o_hbm, idx_v, rows_v, sem):
        base = (lax.axis_index("subcore") * info.num_cores + lax.axis_index("core")) * per_worker

        @pl.loop(0, per_worker // win)
        def _(w):
            off = base + w * win
            pltpu.sync_copy(i_hbm.at[pl.ds(off, win)], idx_v)
            pltpu.async_copy(y_hbm.at[idx_v], rows_v, sem).wait()
            pltpu.sync_copy(rows_v, o_hbm.at[pl.ds(off, win)])

    return gather_kernel(slots, index)


def _expert_kernel(be_ref, bv_ref, nu_ref, x_ref, wg_ref, wu_ref, wd_ref, y_ref, wgu_s, wd_s):
    i = pl.program_id(0)
    used = i < nu_ref[0]
    new_expert = (i == 0) | (be_ref[i] != be_ref[jnp.maximum(i - 1, 0)])

    @pl.when(used & new_expert)
    def _():
        wgu_s[:, 0:D_EXPERT] = wg_ref[0].astype(BF16)
        wgu_s[:, D_EXPERT:2 * D_EXPERT] = wu_ref[0].astype(BF16)
        wd_s[...] = wd_ref[0].astype(BF16)

    @pl.when(used)
    def _():
        live = lax.broadcasted_iota(jnp.int32, x_ref.shape, 0) < bv_ref[i]
        xa, xb = _unpack_rows(jnp.where(live, x_ref[...], 0))
        half = D_MODEL // 2
        gu = _dot(xa.astype(BF16), wgu_s[0:half, :]) + _dot(xb.astype(BF16), wgu_s[half:D_MODEL, :])
        g, u = gu[:, 0:D_EXPERT], gu[:, D_EXPERT:2 * D_EXPERT]
        y_ref[...] = _pack_rows(_dot((g * _sigmoid(g) * u).astype(BF16), wd_s[...]))

    @pl.when(jnp.logical_not(used))
    def _():
        y_ref[...] = jnp.zeros_like(y_ref)


def _experts(block_expert, block_valid, n_used, x_slots, wg, wu, wd):
    p_rows = x_slots.shape[0]
    blk = EXPERT_BLOCK
    wmap = lambda i, be, bv, nu: (be[i], 0, 0)
    grid_spec = pltpu.PrefetchScalarGridSpec(
        num_scalar_prefetch=3,
        grid=(p_rows // blk,),
        in_specs=[pl.BlockSpec((blk, D_MODEL // 2),
                               lambda i, be, bv, nu: (jnp.maximum(jnp.minimum(i, nu[0] - 1), 0), 0)),
                  pl.BlockSpec((1, D_MODEL, D_EXPERT), wmap),
                  pl.BlockSpec((1, D_MODEL, D_EXPERT), wmap),
                  pl.BlockSpec((1, D_EXPERT, D_MODEL), wmap)],
        out_specs=pl.BlockSpec((blk, D_MODEL // 2), lambda i, be, bv, nu: (i, 0)),
        scratch_shapes=[pltpu.VMEM((D_MODEL, 2 * D_EXPERT), BF16), pltpu.VMEM((D_EXPERT, D_MODEL), BF16)])
    return pl.pallas_call(
        _expert_kernel,
        out_shape=jax.ShapeDtypeStruct((p_rows, D_MODEL // 2), jnp.int32),
        grid_spec=grid_spec,
        compiler_params=_cparams(("arbitrary",)),
        name="expert_ffn",
    )(block_expert, block_valid, n_used, x_slots, wg, wu, wd)


def _combine_kernel(yg_ref, gate_ref, sh_ref, x1_ref, ada_ref, g_ref, y_ref):
    gates = gate_ref[...]
    half = D_MODEL // 2
    acc_a = sh_ref[:, 0:half]
    acc_b = sh_ref[:, half:D_MODEL]
    for k in range(TOP_K):
        ya, yb = _unpack_rows(yg_ref[k])
        acc_a = acc_a + gates[:, k:k + 1] * ya
        acc_b = acc_b + gates[:, k:k + 1] * yb
    gt2 = ada_ref[0, 5:6, :]
    y_ref[...] = x1_ref[...] + gt2 * _rms(jnp.concatenate([acc_a, acc_b], axis=1), g_ref[...])


def _combine(yg, gates_t, sh, x1, ada8, chunk, g_post):
    t_all = x1.shape[0]
    tm = TM_COMB
    row = lambda i: (i, 0)
    return pl.pallas_call(
        _combine_kernel,
        out_shape=jax.ShapeDtypeStruct((t_all, D_MODEL), F32),
        grid=(t_all // tm,),
        in_specs=[pl.BlockSpec((TOP_K, tm, D_MODEL // 2), lambda i: (0, i, 0)),
                  pl.BlockSpec((tm, TOP_K), row),
                  pl.BlockSpec((tm, D_MODEL), row),
                  pl.BlockSpec((tm, D_MODEL), row),
                  pl.BlockSpec((1, 6, D_MODEL), lambda i: (i * tm // chunk, 0, 0)),
                  pl.BlockSpec((1, D_MODEL), lambda i: (0, 0))],
        out_specs=pl.BlockSpec((tm, D_MODEL), row),
        compiler_params=_cparams(("parallel",)),
        name="moe_combine",
    )(yg, gates_t, sh, x1, ada8, g_post)


def _rope_partner(w):
    half = QK_ROPE // 2
    return jnp.concatenate([-w[..., half:], w[..., :half]], axis=-1)


def _prep_weights(w_in, w_uq, w_ukv):
    d = w_in.shape[0]
    zeros = lambda r, c: jnp.zeros((r, c), F32)
    kr = w_in[:, 384:416]
    r_main = jnp.concatenate([zeros(d, QK_NOPE), kr, zeros(d, HEAD_PAD - QK_NOPE - QK_ROPE)], axis=1)
    r_part = jnp.concatenate([zeros(d, QK_NOPE), _rope_partner(kr), zeros(d, HEAD_PAD - QK_NOPE - QK_ROPE)], axis=1)
    win2 = jnp.concatenate([w_in[:, :384], r_main, r_part, w_in[:, 416:]], axis=1).astype(BF16)

    wq = w_uq.reshape(Q_LORA, N_HEADS, QK_NOPE + QK_ROPE)
    zq = jnp.zeros((Q_LORA, N_HEADS, HEAD_PAD - QK_NOPE - QK_ROPE), F32)
    q_main = jnp.concatenate([wq, zq], axis=-1).reshape(Q_LORA, N_HEADS * HEAD_PAD)
    q_part = jnp.concatenate([jnp.zeros((Q_LORA, N_HEADS, QK_NOPE), F32), _rope_partner(wq[..., QK_NOPE:]), zq],
                             axis=-1).reshape(Q_LORA, N_HEADS * HEAD_PAD)
    wuq2 = jnp.concatenate([q_main, q_part], axis=1).astype(BF16)

    wkv = w_ukv.reshape(KV_LORA, N_HEADS, QK_NOPE + HEAD_DIM)
    k_pad = jnp.concatenate([wkv[..., :QK_NOPE], jnp.zeros((KV_LORA, N_HEADS, HEAD_PAD - QK_NOPE), F32)],
                            axis=-1).reshape(KV_LORA, N_HEADS * HEAD_PAD)
    v_cols = wkv[..., QK_NOPE:].reshape(KV_LORA, N_HEADS * HEAD_DIM)
    wukv2 = jnp.concatenate([k_pad, v_cols], axis=1).astype(BF16)
    return win2, wuq2, wukv2


def _rope_table(s_max):
    half = QK_ROPE // 2
    inv = ROPE_BASE ** (-jnp.arange(half, dtype=F32) / half)
    ang = jnp.arange(s_max, dtype=F32)[:, None] * inv[None, :]
    cos = jnp.concatenate([jnp.cos(ang), jnp.cos(ang)], axis=1)
    sin = jnp.concatenate([jnp.sin(ang), jnp.sin(ang)], axis=1)
    scale = (QK_NOPE + QK_ROPE) ** -0.5 * math.log2(math.e)
    pad = jnp.zeros((s_max, HEAD_PAD - QK_NOPE - QK_ROPE), F32)
    zn = jnp.zeros((s_max, QK_NOPE), F32)
    cosq = jnp.concatenate([jnp.full((s_max, QK_NOPE), scale, F32), cos * scale, pad], axis=1)
    sinq = jnp.concatenate([zn, sin * scale, pad], axis=1)
    cosk = jnp.concatenate([zn, cos, pad], axis=1)
    sink = jnp.concatenate([zn, sin, pad], axis=1)
    return jnp.concatenate([cosq, sinq, cosk, sink], axis=1)


def _layer(x_prompt, x_sample, c_prompt, c_sample, w_ada, b_ada, g_pre_mix, w_in, g_qa, w_uq, g_kva, w_ukv,
           g_out_a, g_out_b, w_o, g_post_mix, g_pre_ffn, router_w, router_bias, w_exp_gate, w_exp_up, w_exp_down,
           w_sh_gate, w_sh_up, w_sh_down, g_post_ffn):
    b1, s1, d = x_prompt.shape
    b2, s2, _ = x_sample.shape
    n1, n2 = b1 * s1, b2 * s2
    t_all = n1 + n2
    chunk = math.gcd(s1, s2)
    assert n1 % s2 == 0 and chunk % max(TM_PRE, TM_POST, TM_COMB) == 0

    x_all = jnp.concatenate([x_prompt.reshape(n1, d), x_sample.reshape(n2, d)], axis=0)
    c_all = jnp.concatenate([c_prompt, c_sample, jnp.zeros((8 - b1 - b2, d), F32)], axis=0)
    ada = _ada(c_all, w_ada, b_ada)
    chunk_batch = jnp.concatenate([jnp.repeat(jnp.arange(b1), s1 // chunk),
                                   b1 + jnp.repeat(jnp.arange(b2), s2 // chunk)])
    ada8 = ada.reshape(8, 6, d)[chunk_batch]

    win2, wuq2, wukv2 = _prep_weights(w_in, w_uq, w_ukv)
    tab = _rope_table(max(s1, s2))
    r2 = lambda g: g.reshape(1, -1)
    qa, ka, va, qd, kd, vd = _premix(x_all, ada8, chunk, r2(g_pre_mix), win2, r2(g_qa), wuq2, r2(g_kva), wukv2, tab,
                                     n1, s1, s2)

    oa = jnp.concatenate([_mla(qa, ka, va, 0, b1, s1), _mla(qa, ka, va, n1, b2, s2)], axis=0)
    os_, ls_ = [], []
    for _, dil in DIL_PATTERNS:
        o, lse = _dilated(qd, kd, vd, dil, n1, s1, s2)
        os_.append(o)
        ls_.append(lse)

    x1, h2p, sh, idx_t, gate_t, rank_t, cnt = _postmix(
        x_all, oa, os_, ls_, ada8, chunk, r2(g_out_a), r2(g_out_b), w_o.astype(BF16), r2(g_post_mix), r2(g_pre_ffn),
        router_w.T, router_bias.reshape(N_EXPERTS, 1), w_sh_gate.astype(BF16), w_sh_up.astype(BF16),
        w_sh_down.astype(BF16))

    blk = EXPERT_BLOCK
    n_assign = t_all * TOP_K
    n_blocks = -(-(n_assign + N_EXPERTS * (blk - 1)) // blk)
    counts = cnt[:, 0].astype(jnp.int32)
    padded = (counts + blk - 1) // blk * blk
    pad_end = jnp.cumsum(padded)
    pad_start = pad_end - padded
    dest = _dest(idx_t, rank_t, pad_start)
    block_first = jnp.arange(n_blocks, dtype=jnp.int32) * blk
    block_expert = jnp.minimum(jnp.sum((pad_end[None, :] <= block_first[:, None]).astype(jnp.int32), axis=1),
                               N_EXPERTS - 1)
    n_used = (pad_end[-1] // blk).astype(jnp.int32).reshape(1)
    block_valid = jnp.clip(counts[block_expert] - (block_first - pad_start[block_expert]), 0, blk)
    dest_flat = dest.reshape(n_assign)
    x_slots = _dispatch_rows(h2p, dest_flat, n_blocks * blk)
    y_slots = _experts(block_expert, block_valid, n_used, x_slots, w_exp_gate, w_exp_up, w_exp_down)
    yg = _gather_rows(y_slots, dest_flat).reshape(TOP_K, t_all, d // 2)

    y = _combine(yg, gate_t.T, sh, x1, ada8, chunk, r2(g_post_ffn))
    return y[:n1].reshape(b1, s1, d), y[n1:].reshape(b2, s2, d)


def kernel(x_prompt, x_sample, c_prompt, c_sample, w_ada, b_ada, g_pre_mix, w_in, g_qa, w_uq, g_kva, w_ukv, g_out_a, g_out_b, w_o, g_post_mix, g_pre_ffn, router_w, router_bias, w_exp_gate, w_exp_up, w_exp_down, w_sh_gate, w_sh_up, w_sh_down, g_post_ffn):
    layer = [p[0] for p in (w_ada, b_ada, g_pre_mix, w_in, g_qa, w_uq, g_kva, w_ukv, g_out_a, g_out_b, w_o,
                            g_post_mix, g_pre_ffn, router_w, router_bias, w_exp_gate, w_exp_up, w_exp_down,
                            w_sh_gate, w_sh_up, w_sh_down, g_post_ffn)]
    return _layer(x_prompt, x_sample, c_prompt, c_sample, *layer)
```

```python
import functools
import math

import jax
import jax.numpy as jnp
from jax import lax
from jax.experimental import pallas as pl
from jax.experimental.pallas import tpu as pltpu
from jax.experimental.pallas import tpu_sc as plsc

F32 = jnp.float32
BF16 = jnp.bfloat16

D_MODEL = 1024
HEAD_DIM = 64
N_HEADS = 8
Q_LORA = 256
KV_LORA = 128
QK_NOPE = 64
QK_ROPE = 32
ROPE_BASE = 10000.0
DIL_PATTERNS = ((128, 1), (512, 4), (2048, 16))
DILATIONS = tuple(d for _, d in DIL_PATTERNS)
DIL_HALF = 64
assert all(w // (2 * d) == DIL_HALF for w, d in DIL_PATTERNS) and DILATIONS[0] == 1
N_EXPERTS = 256
TOP_K = 8
N_GROUPS = 8
GROUP_SIZE = N_EXPERTS // N_GROUPS
TOPK_GROUPS = 4
D_EXPERT = 256
ROUTE_SCALE = 2.5
EPS = 1e-6
NEG = -1e30

LANES = 128
HEAD_PAD = 128

TM_PRE = 512
TM_POST = 512
TM_COMB = 512
TQ_MLA = 256
RB_MLA = 64
TM_DEST = 2048
SC_WINDOW = 128
TQ_DIL = 128
EXPERT_BLOCK = 512
VMEM_LIMIT = 56 * 1024 * 1024


def _cparams(sem):
    return pltpu.CompilerParams(dimension_semantics=sem, vmem_limit_bytes=VMEM_LIMIT)


def _rms(x, g):
    return x * lax.rsqrt(jnp.mean(x * x, axis=-1, keepdims=True) + EPS) * g


def _sigmoid(x):
    return 1.0 / (1.0 + jnp.exp(-x))


def _dot(a, b):
    return jnp.dot(a, b, preferred_element_type=F32)


def _pack_rows(x):
    n = x.shape[1] // 2
    bits = lax.bitcast_convert_type(x.astype(BF16).astype(F32), jnp.int32)
    return bits[:, :n] | lax.shift_right_logical(bits[:, n:], 16)


def _unpack_rows(u):
    hi = lax.bitcast_convert_type(u & jnp.int32(-65536), F32)
    lo = lax.bitcast_convert_type(lax.shift_left(u, 16), F32)
    return hi, lo


def _dot_nt(a, b, precision=None):
    return lax.dot_general(a, b, (((1,), (1,)), ((), ())), preferred_element_type=F32, precision=precision)


def _ada_kernel(c_ref, w_ref, b_ref, o_ref):
    c = c_ref[...]
    s = c * _sigmoid(c)
    o_ref[...] = jnp.dot(s, w_ref[...], preferred_element_type=F32, precision=lax.Precision.HIGHEST) + b_ref[...]


def _ada(c_all, w_ada, b_ada):
    nb, d = c_all.shape
    n_out = w_ada.shape[1]
    tn = 1024
    return pl.pallas_call(
        _ada_kernel,
        out_shape=jax.ShapeDtypeStruct((nb, n_out), F32),
        grid=(n_out // tn,),
        in_specs=[pl.BlockSpec((nb, d), lambda j: (0, 0)),
                  pl.BlockSpec((d, tn), lambda j: (0, j)),
                  pl.BlockSpec((1, tn), lambda j: (0, j))],
        out_specs=pl.BlockSpec((nb, tn), lambda j: (0, j)),
        compiler_params=_cparams(("arbitrary",)),
        name="ada",
    )(c_all, w_ada, b_ada.reshape(1, n_out))


def _premix_kernel(x_ref, ada_ref, g_ref, win_ref, gqa_ref, wuq_ref, gkva_ref, wukv_ref, tab_ref,
                   qa_ref, ka_ref, va_ref, *rest):
    dil_refs, zs_ref = rest[:-1], rest[-1]
    x = x_ref[...]
    sh1 = ada_ref[0, 0:1, :]
    sc1 = ada_ref[0, 1:2, :]
    h = _rms(x, g_ref[...]) * (1.0 + sc1) + sh1
    z = _dot(h.astype(BF16), win_ref[...])
    tab = tab_ref[...]
    cosq, sinq = tab[:, 0:128], tab[:, 128:256]
    cosk, sink = tab[:, 256:384], tab[:, 384:512]
    nh = N_HEADS * HEAD_PAD

    cq = _rms(z[:, 0:Q_LORA], gqa_ref[...]).astype(BF16)
    qq = _dot(cq, wuq_ref[...])
    for h_i in range(N_HEADS):
        lo = h_i * HEAD_PAD
        qa_ref[:, lo:lo + HEAD_PAD] = (qq[:, lo:lo + HEAD_PAD] * cosq
                                       + qq[:, nh + lo:nh + lo + HEAD_PAD] * sinq).astype(BF16)

    ckv = _rms(z[:, 256:384], gkva_ref[...]).astype(BF16)
    kk = _dot(ckv, wukv_ref[...])
    rr = z[:, 384:512] * cosk + z[:, 512:640] * sink
    for h_i in range(N_HEADS):
        lo = h_i * HEAD_PAD
        ka_ref[:, lo:lo + HEAD_PAD] = (kk[:, lo:lo + HEAD_PAD] + rr).astype(BF16)
    va_ref[...] = kk[:, nh:nh + 512].astype(BF16)

    tm = x.shape[0]
    wd = N_HEADS * HEAD_DIM
    n_slab = 3 * wd // LANES
    for c in range(n_slab):
        scale = HEAD_DIM ** -0.5 if c < wd // LANES else 1.0
        zs_ref[c] = z[:, 640 + c * LANES:640 + (c + 1) * LANES] * scale
    for dil, refs in zip(DILATIONS, (dil_refs[0:3], dil_refs[3:6], dil_refs[6:9])):
        n = tm // dil
        for r in range(dil):
            for c in range(n_slab):
                rows = zs_ref[c] if dil == 1 else zs_ref.at[c][pl.ds(r, n, stride=dil), :]
                col = r * wd + (c % (wd // LANES)) * LANES
                refs[c // (wd // LANES)][:, col:col + LANES] = rows.astype(BF16)


def _premix(x_all, ada8, chunk, g_pre, win2, g_qa, wuq2, g_kva, wukv2, tab, n1, s1, s2):
    t_all = x_all.shape[0]
    tm = TM_PRE
    nt1 = n1 // tm
    tb1, tb2 = s1 // tm, s2 // tm

    def tab_map(i):
        return (jnp.where(i < nt1, i % tb1, (i - nt1) % tb2), 0)

    row = lambda i: (i, 0)
    const = lambda i: (0, 0)
    wd = N_HEADS * HEAD_DIM
    outs = [jax.ShapeDtypeStruct((t_all, w), BF16) for w in (1024, 1024, 512)]
    out_specs = [pl.BlockSpec((tm, w), row) for w in (1024, 1024, 512)]
    for dil in DILATIONS:
        outs += [jax.ShapeDtypeStruct((t_all // dil, dil * wd), BF16)] * 3
        out_specs += [pl.BlockSpec((tm // dil, dil * wd), row)] * 3
    return pl.pallas_call(
        _premix_kernel,
        out_shape=outs,
        grid=(t_all // tm,),
        in_specs=[pl.BlockSpec((tm, D_MODEL), row),
                  pl.BlockSpec((1, 6, D_MODEL), lambda i: (i * tm // chunk, 0, 0)),
                  pl.BlockSpec((1, D_MODEL), const),
                  pl.BlockSpec(win2.shape, const),
                  pl.BlockSpec((1, Q_LORA), const),
                  pl.BlockSpec(wuq2.shape, const),
                  pl.BlockSpec((1, KV_LORA), const),
                  pl.BlockSpec(wukv2.shape, const),
                  pl.BlockSpec((tm, 512), tab_map)],
        out_specs=out_specs,
        scratch_shapes=[pltpu.VMEM((3 * wd // LANES, tm, LANES), F32)],
        compiler_params=_cparams(("parallel",)),
        name="premix",
    )(x_all, ada8, g_pre, win2, g_qa, wuq2, g_kva, wukv2, tab)


def _mla_kernel(q_ref, k_ref, v_ref, o_ref, s_ref, p_ref, m_ref, l_ref, *, rb):
    tq = q_ref.shape[0]
    lane = lax.broadcasted_iota(jnp.int32, (tq, LANES), 1)

    def scores(hh):
        s = _dot_nt(q_ref[:, hh * HEAD_PAD:(hh + 1) * HEAD_PAD], k_ref[:, hh * HEAD_PAD:(hh + 1) * HEAD_PAD])
        s_ref[hh] = s
        m_ref[hh] = jnp.max(s, axis=-1, keepdims=True)

    def numerators(hh):
        for r0 in range(0, tq, rb):
            p = jnp.exp2(s_ref[hh, r0:r0 + rb, :] - m_ref[hh, r0:r0 + rb, :])
            l_ref[hh, r0:r0 + rb, :] = jnp.sum(p, axis=-1, keepdims=True)
            p_ref[hh, r0:r0 + rb, :] = p.astype(BF16)

    def values(hh):
        return _dot(p_ref[hh], v_ref[...]) / l_ref[hh]

    scores(0)
    scores(1)
    numerators(0)
    o0 = values(0)
    numerators(1)
    o1 = values(1)
    o_ref[...] = jnp.where(lane < HEAD_DIM, o0, o1).astype(o_ref.dtype)


def _mla(qa, ka, va, row_off, batch, seq):
    tq = TQ_MLA
    nq = seq // tq
    qoff = row_off // tq
    soff = row_off // seq
    return pl.pallas_call(
        functools.partial(_mla_kernel, rb=RB_MLA),
        out_shape=jax.ShapeDtypeStruct((batch * seq, N_HEADS * HEAD_DIM), BF16),
        grid=(batch, N_HEADS // 2, nq),
        in_specs=[pl.BlockSpec((tq, 2 * HEAD_PAD), lambda b, j, qi: (qoff + b * nq + qi, j)),
                  pl.BlockSpec((seq, 2 * HEAD_PAD), lambda b, j, qi: (soff + b, j)),
                  pl.BlockSpec((seq, LANES), lambda b, j, qi: (soff + b, j))],
        out_specs=pl.BlockSpec((tq, LANES), lambda b, j, qi: (b * nq + qi, j)),
        scratch_shapes=[pltpu.VMEM((2, tq, seq), F32), pltpu.VMEM((2, tq, seq), BF16),
                        pltpu.VMEM((2, tq, 1), F32), pltpu.VMEM((2, tq, 1), F32)],
        compiler_params=_cparams(("parallel", "parallel", "parallel")),
        name="mla_attention",
    )(qa, ka, va)


def _dil_kernel(q_ref, kp_ref, kc_ref, kn_ref, vp_ref, vc_ref, vn_ref, o_ref, lse_ref, *, dil, n1_rows, len1, len2):
    tq = q_ref.shape[0]
    half = DIL_HALF
    w = tq + 2 * half
    row0 = pl.program_id(0) * tq
    in1 = row0 < n1_rows
    seq_len = jnp.where(in1, len1, len2)
    q0 = jnp.where(in1, row0 % len1, (row0 - n1_rows) % len2)

    kw = jnp.concatenate([kp_ref[tq - half:tq, :], kc_ref[...], kn_ref[0:half, :]], axis=0)
    vw = jnp.concatenate([vp_ref[tq - half:tq, :], vc_ref[...], vn_ref[0:half, :]], axis=0)
    col = lax.broadcasted_iota(jnp.int32, (tq, w), 1)
    rw = lax.broadcasted_iota(jnp.int32, (tq, w), 0)
    absd = jnp.abs(col - half - rw)
    kpos = q0 + col - half
    valid = (absd <= half) & (kpos >= 0) & (kpos < seq_len)
    dist = (absd * dil).astype(F32)
    lane = lax.broadcasted_iota(jnp.int32, (tq, LANES), 1)
    low = lane < HEAD_DIM
    for j in range(N_HEADS // 2):
        kpair = kw[:, j * LANES:(j + 1) * LANES]
        vpair = vw[:, j * LANES:(j + 1) * LANES]
        qpair = q_ref[:, j * LANES:(j + 1) * LANES]
        zero = jnp.zeros_like(qpair)
        res = []
        for hh in range(2):
            h_i = 2 * j + hh
            slope = 2.0 ** (-8.0 * (h_i + 1) / N_HEADS)
            q = jnp.where(low, qpair, zero) if hh == 0 else jnp.where(low, zero, qpair)
            s = _dot_nt(q, kpair)
            s = jnp.where(valid, s - slope * dist, NEG)
            m = jnp.max(s, axis=-1, keepdims=True)
            p = jnp.exp(s - m)
            l = jnp.sum(p, axis=-1, keepdims=True)
            res.append((_dot(p.astype(BF16), vpair) / l, m + jnp.log(l)))
        o_ref[:, j * LANES:(j + 1) * LANES] = jnp.where(low, res[0][0], res[1][0]).astype(o_ref.dtype)
        lse_ref[:, j * LANES:(j + 1) * LANES] = jnp.where(low, res[0][1], res[1][1])


def _dilated(qv, kv, vv, dil, n1, s1, s2):
    rows = qv.shape[0]
    tq = TQ_DIL
    nt = rows // tq
    wk = N_HEADS * HEAD_DIM
    cur = lambda i, r: (i, r)
    prv = lambda i, r: (jnp.maximum(i - 1, 0), r)
    nxt = lambda i, r: (jnp.minimum(i + 1, nt - 1), r)
    kspecs = [pl.BlockSpec((tq, wk), f) for f in (prv, cur, nxt)]
    return pl.pallas_call(
        functools.partial(_dil_kernel, dil=dil, n1_rows=n1 // dil, len1=s1 // dil, len2=s2 // dil),
        out_shape=[jax.ShapeDtypeStruct((rows, dil * wk), BF16), jax.ShapeDtypeStruct((rows, dil * wk), F32)],
        grid=(nt, dil),
        in_specs=[pl.BlockSpec((tq, wk), cur)] + kspecs + kspecs,
        out_specs=[pl.BlockSpec((tq, wk), cur), pl.BlockSpec((tq, wk), cur)],
        compiler_params=_cparams(("parallel", "parallel")),
        name="dilated_attention_d%d" % dil,
    )(qv, kv, kv, kv, vv, vv, vv)


def _postmix_kernel(x_ref, oa_ref, o1_ref, o2_ref, o3_ref, l1_ref, l2_ref, l3_ref, ada_ref, goa_ref, gob_ref,
                    wo_ref, gpm_ref, gpf_ref, rwt_ref, rb_ref, wsg_ref, wsu_ref, wsd_ref,
                    x1_ref, h2_ref, sh_ref, idx_ref, gate_ref, rank_ref, cnt_ref, carry_ref, nat_ref):
    tm = x_ref.shape[0]

    @pl.when(pl.program_id(0) == 0)
    def _():
        carry_ref[...] = jnp.zeros_like(carry_ref)

    gt1, sh2, sc2 = ada_ref[0, 2:3, :], ada_ref[0, 3:4, :], ada_ref[0, 4:5, :]

    wd = N_HEADS * HEAD_DIM
    n_slab = wd // LANES

    def row_order(src_ref, dil, base):
        if dil == 1:
            return src_ref[...].astype(F32)
        n = tm // dil
        for r in range(dil):
            for c in range(n_slab):
                col = r * wd + c * LANES
                nat_ref.at[base + c][pl.ds(r, n, stride=dil), :] = src_ref[:, col:col + LANES].astype(F32)
        return jnp.concatenate([nat_ref[base + c] for c in range(n_slab)], axis=1)

    o_pat = [row_order(ref, dil, (2 * p) * n_slab) for p, (ref, dil) in enumerate(zip((o1_ref, o2_ref, o3_ref), DILATIONS))]
    la, lb, lc = [row_order(ref, dil, (2 * p + 1) * n_slab)
                  for p, (ref, dil) in enumerate(zip((l1_ref, l2_ref, l3_ref), DILATIONS))]

    mx = jnp.maximum(jnp.maximum(la, lb), lc)
    ea, eb, ec = jnp.exp(la - mx), jnp.exp(lb - mx), jnp.exp(lc - mx)
    ob = (ea * o_pat[0] + eb * o_pat[1] + ec * o_pat[2]) / (ea + eb + ec)

    na = _rms(oa_ref[...].astype(F32), goa_ref[...]).astype(BF16)
    nb = _rms(ob, gob_ref[...]).astype(BF16)
    half_w = N_HEADS * HEAD_DIM
    o = _dot(na, wo_ref[0:half_w, :]) + _dot(nb, wo_ref[half_w:2 * half_w, :])
    x1 = x_ref[...] + gt1 * _rms(o, gpm_ref[...])
    x1_ref[...] = x1
    h2 = _rms(x1, gpf_ref[...]) * (1.0 + sc2) + sh2
    h2b = h2.astype(BF16)
    h2_ref[...] = _pack_rows(h2)

    g = _dot(h2b, wsg_ref[...])
    u = _dot(h2b, wsu_ref[...])
    sh_ref[...] = _dot((g * _sigmoid(g) * u).astype(BF16), wsd_ref[...])

    scores = _sigmoid(_dot_nt(rwt_ref[...], h2, precision=lax.Precision.HIGHEST))
    biased = scores + rb_ref[...]
    ninf = -jnp.inf
    row = lax.broadcasted_iota(jnp.int32, (N_EXPERTS, tm), 0)
    rwg = lax.broadcasted_iota(jnp.int32, (GROUP_SIZE, tm), 0)
    gsc = []
    for gi in range(N_GROUPS):
        blk = biased[gi * GROUP_SIZE:(gi + 1) * GROUP_SIZE]
        m1 = jnp.max(blk, axis=0, keepdims=True)
        i1 = jnp.min(jnp.where(blk == m1, rwg, N_EXPERTS), axis=0, keepdims=True)
        m2 = jnp.max(jnp.where(rwg == i1, ninf, blk), axis=0, keepdims=True)
        gsc.append(m1 + m2)
    gsel = [jnp.zeros((1, tm), F32) for _ in range(N_GROUPS)]
    for _ in range(TOPK_GROUPS):
        m = functools.reduce(jnp.maximum, gsc)
        free = jnp.ones((1, tm), F32)
        for gi in range(N_GROUPS):
            hit = jnp.where(gsc[gi] == m, free, 0.0)
            free = free - hit
            gsel[gi] = gsel[gi] + hit
            gsc[gi] = jnp.where(hit > 0.0, ninf, gsc[gi])
    masked = jnp.concatenate(
        [jnp.where(gsel[gi] > 0.0, biased[gi * GROUP_SIZE:(gi + 1) * GROUP_SIZE], ninf)
         for gi in range(N_GROUPS)], axis=0)
    cur = masked
    idxs, gts = [], []
    for _ in range(TOP_K):
        m = jnp.max(cur, axis=0, keepdims=True)
        ik = jnp.min(jnp.where(cur == m, row, N_EXPERTS), axis=0, keepdims=True)
        hit = row == ik
        idxs.append(ik)
        gts.append(jnp.sum(jnp.where(hit, scores, 0.0), axis=0, keepdims=True))
        cur = jnp.where(hit, ninf, cur)
    idx = jnp.concatenate(idxs, axis=0)
    gates = jnp.concatenate(gts, axis=0)
    idx_ref[...] = idx
    gate_ref[...] = gates / jnp.sum(gates, axis=0, keepdims=True) * ROUTE_SCALE

    sel = jnp.where(cur != masked, 1.0, 0.0)
    tr = lax.broadcasted_iota(jnp.int32, (tm, tm), 0)
    tc = lax.broadcasted_iota(jnp.int32, (tm, tm), 1)
    before = jnp.where(tr < tc, 1.0, 0.0).astype(BF16)
    pos = _dot(sel.astype(BF16), before) + carry_ref[...]
    rank_ref[...] = jnp.concatenate(
        [jnp.sum(jnp.where(row == idxs[k], pos, 0.0), axis=0, keepdims=True) for k in range(TOP_K)],
        axis=0).astype(jnp.int32)
    carry = carry_ref[...] + jnp.sum(sel, axis=1, keepdims=True)
    carry_ref[...] = carry
    cnt_ref[...] = jnp.broadcast_to(carry, cnt_ref.shape)


def _postmix(x_all, oa, os_, ls_, ada8, chunk, g_oa, g_ob, wo, g_pm, g_pf, rwt, rb, wsg, wsu, wsd):
    t_all = x_all.shape[0]
    tm = TM_POST
    row = lambda i: (i, 0)
    col = lambda i: (0, i)
    const = lambda i: (0, 0)
    hw = N_HEADS * HEAD_DIM
    pat_specs = [pl.BlockSpec((tm // dil, dil * hw), row) for dil in DILATIONS]
    in_specs = ([pl.BlockSpec((tm, D_MODEL), row), pl.BlockSpec((tm, hw), row)]
                + pat_specs + pat_specs
                + [pl.BlockSpec((1, 6, D_MODEL), lambda i: (i * tm // chunk, 0, 0)),
                   pl.BlockSpec((1, hw), const), pl.BlockSpec((1, hw), const),
                   pl.BlockSpec(wo.shape, const),
                   pl.BlockSpec((1, D_MODEL), const), pl.BlockSpec((1, D_MODEL), const),
                   pl.BlockSpec(rwt.shape, const), pl.BlockSpec((N_EXPERTS, 1), const),
                   pl.BlockSpec(wsg.shape, const), pl.BlockSpec(wsu.shape, const), pl.BlockSpec(wsd.shape, const)])
    out_shape = [jax.ShapeDtypeStruct((t_all, D_MODEL), F32),
                 jax.ShapeDtypeStruct((t_all, D_MODEL // 2), jnp.int32),
                 jax.ShapeDtypeStruct((t_all, D_MODEL), F32),
                 jax.ShapeDtypeStruct((TOP_K, t_all), jnp.int32),
                 jax.ShapeDtypeStruct((TOP_K, t_all), F32),
                 jax.ShapeDtypeStruct((TOP_K, t_all), jnp.int32),
                 jax.ShapeDtypeStruct((N_EXPERTS, LANES), F32)]
    out_specs = [pl.BlockSpec((tm, D_MODEL), row), pl.BlockSpec((tm, D_MODEL // 2), row), pl.BlockSpec((tm, D_MODEL), row),
                 pl.BlockSpec((TOP_K, tm), col), pl.BlockSpec((TOP_K, tm), col), pl.BlockSpec((TOP_K, tm), col),
                 pl.BlockSpec((N_EXPERTS, LANES), const)]
    return pl.pallas_call(
        _postmix_kernel,
        out_shape=out_shape,
        grid=(t_all // tm,),
        in_specs=in_specs,
        out_specs=out_specs,
        scratch_shapes=[pltpu.VMEM((N_EXPERTS, 1), F32),
                        pltpu.VMEM((2 * len(DILATIONS) * hw // LANES, tm, LANES), F32)],
        compiler_params=_cparams(("arbitrary",)),
        name="postmix_router",
    )(x_all, oa, *os_, *ls_, ada8, g_oa, g_ob, wo, g_pm, g_pf, rwt, rb, wsg, wsu, wsd)


def _dest_kernel(idx_ref, rank_ref, start_ref, dest_ref):
    tm = idx_ref.shape[1]
    row = lax.broadcasted_iota(jnp.int32, (N_EXPERTS, tm), 0)
    start = start_ref[...]
    base = [jnp.sum(jnp.where(row == idx_ref[k:k + 1, :], start, 0), axis=0, keepdims=True) for k in range(TOP_K)]
    dest_ref[...] = jnp.concatenate(base, axis=0) + rank_ref[...]


def _dest(idx_t, rank_t, pad_start):
    t_all = idx_t.shape[1]
    tm = TM_DEST
    col = lambda i: (0, i)
    return pl.pallas_call(
        _dest_kernel,
        out_shape=jax.ShapeDtypeStruct((TOP_K, t_all), jnp.int32),
        grid=(t_all // tm,),
        in_specs=[pl.BlockSpec((TOP_K, tm), col), pl.BlockSpec((TOP_K, tm), col),
                  pl.BlockSpec((N_EXPERTS, 1), lambda i: (0, 0))],
        out_specs=pl.BlockSpec((TOP_K, tm), col),
        compiler_params=_cparams(("parallel",)),
        name="slot_index",
    )(idx_t, rank_t, pad_start.reshape(N_EXPERTS, 1))


def _sc_mesh():
    return plsc.VectorSubcoreMesh(core_axis_name="core", subcore_axis_name="subcore")


def _dispatch_rows(rows, dest, n_slots):
    t_all, width = rows.shape
    win = SC_WINDOW
    info = plsc.get_sparse_core_info()
    n_workers = info.num_cores * info.num_subcores
    per_worker = t_all // n_workers
    assert per_worker % win == 0

    @functools.partial(
        pl.kernel, out_type=jax.ShapeDtypeStruct((n_slots, width), rows.dtype), mesh=_sc_mesh(),
        scratch_types=[pltpu.VMEM((TOP_K, win), jnp.int32), pltpu.VMEM((win, width), rows.dtype),
                       pltpu.SemaphoreType.DMA])
    def scatter_kernel(x_hbm, i_hbm, o_hbm, idx_v, rows_v, sem):
        base = (lax.axis_index("subcore") * info.num_cores + lax.axis_index("core")) * per_worker

        @pl.loop(0, per_worker // win)
        def _(w):
            off = base + w * win
            for k in range(TOP_K):
                pltpu.sync_copy(i_hbm.at[pl.ds(k * t_all + off, win)], idx_v.at[k])
            pltpu.sync_copy(x_hbm.at[pl.ds(off, win)], rows_v)
            copies = [pltpu.async_copy(rows_v, o_hbm.at[idx_v.at[k]], sem) for k in range(TOP_K)]
            for c in copies:
                c.wait()

    return scatter_kernel(rows, dest)


def _gather_rows(slots, index):
    n = index.shape[0]
    width = slots.shape[1]
    win = SC_WINDOW
    info = plsc.get_sparse_core_info()
    n_workers = info.num_cores * info.num_subcores
    per_worker = n // n_workers
    assert per_worker % win == 0

    @functools.partial(
        pl.kernel, out_type=jax.ShapeDtypeStruct((n, width), slots.dtype), mesh=_sc_mesh(),
        scratch_types=[pltpu.VMEM((win,), jnp.int32), pltpu.VMEM((win, width), slots.dtype),
                       pltpu.SemaphoreType.DMA])
    def gather_kernel(y_hbm, i_hbm, o_hbm, idx_v, rows_v, sem):
        base = (lax.axis_index("subcore") * info.num_cores + lax.axis_index("core")) * per_worker

        @pl.loop(0, per_worker // win)
        def _(w):
            off = base + w * win
            pltpu.sync_copy(i_hbm.at[pl.ds(off, win)], idx_v)
            pltpu.async_copy(y_hbm.at[idx_v], rows_v, sem).wait()
            pltpu.sync_copy(rows_v, o_hbm.at[pl.ds(off, win)])

    return gather_kernel(slots, index)


def _expert_kernel(be_ref, bv_ref, nu_ref, x_ref, wg_ref, wu_ref, wd_ref, y_ref, wgu_s, wd_s):
    i = pl.program_id(0)
    used = i < nu_ref[0]
    new_expert = (i == 0) | (be_ref[i] != be_ref[jnp.maximum(i - 1, 0)])

    @pl.when(used & new_expert)
    def _():
        wgu_s[:, 0:D_EXPERT] = wg_ref[0].astype(BF16)
        wgu_s[:, D_EXPERT:2 * D_EXPERT] = wu_ref[0].astype(BF16)
        wd_s[...] = wd_ref[0].astype(BF16)

    @pl.when(used)
    def _():
        live = lax.broadcasted_iota(jnp.int32, x_ref.shape, 0) < bv_ref[i]
        xa, xb = _unpack_rows(jnp.where(live, x_ref[...], 0))
        half = D_MODEL // 2
        gu = _dot(xa.astype(BF16), wgu_s[0:half, :]) + _dot(xb.astype(BF16), wgu_s[half:D_MODEL, :])
        g, u = gu[:, 0:D_EXPERT], gu[:, D_EXPERT:2 * D_EXPERT]
        y_ref[...] = _pack_rows(_dot((g * _sigmoid(g) * u).astype(BF16), wd_s[...]))

    @pl.when(jnp.logical_not(used))
    def _():
        y_ref[...] = jnp.zeros_like(y_ref)


def _experts(block_expert, block_valid, n_used, x_slots, wg, wu, wd):
    p_rows = x_slots.shape[0]
    blk = EXPERT_BLOCK
    wmap = lambda i, be, bv, nu: (be[i], 0, 0)
    grid_spec = pltpu.PrefetchScalarGridSpec(
        num_scalar_prefetch=3,
        grid=(p_rows // blk,),
        in_specs=[pl.BlockSpec((blk, D_MODEL // 2),
                               lambda i, be, bv, nu: (jnp.maximum(jnp.minimum(i, nu[0] - 1), 0), 0)),
                  pl.BlockSpec((1, D_MODEL, D_EXPERT), wmap),
                  pl.BlockSpec((1, D_MODEL, D_EXPERT), wmap),
                  pl.BlockSpec((1, D_EXPERT, D_MODEL), wmap)],
        out_specs=pl.BlockSpec((blk, D_MODEL // 2), lambda i, be, bv, nu: (i, 0)),
        scratch_shapes=[pltpu.VMEM((D_MODEL, 2 * D_EXPERT), BF16), pltpu.VMEM((D_EXPERT, D_MODEL), BF16)])
    return pl.pallas_call(
        _expert_kernel,
        out_shape=jax.ShapeDtypeStruct((p_rows, D_MODEL // 2), jnp.int32),
        grid_spec=grid_spec,
        compiler_params=_cparams(("arbitrary",)),
        name="expert_ffn",
    )(block_expert, block_valid, n_used, x_slots, wg, wu, wd)


def _combine_kernel(yg_ref, gate_ref, sh_ref, x1_ref, ada_ref, g_ref, y_ref):
    gates = gate_ref[...]
    half = D_MODEL // 2
    acc_a = sh_ref[:, 0:half]
    acc_b = sh_ref[:, half:D_MODEL]
    for k in range(TOP_K):
        ya, yb = _unpack_rows(yg_ref[k])
        acc_a = acc_a + gates[:, k:k + 1] * ya
        acc_b = acc_b + gates[:, k:k + 1] * yb
    gt2 = ada_ref[0, 5:6, :]
    y_ref[...] = x1_ref[...] + gt2 * _rms(jnp.concatenate([acc_a, acc_b], axis=1), g_ref[...])


def _combine(yg, gates_t, sh, x1, ada8, chunk, g_post):
    t_all = x1.shape[0]
    tm = TM_COMB
    row = lambda i: (i, 0)
    return pl.pallas_call(
        _combine_kernel,
        out_shape=jax.ShapeDtypeStruct((t_all, D_MODEL), F32),
        grid=(t_all // tm,),
        in_specs=[pl.BlockSpec((TOP_K, tm, D_MODEL // 2), lambda i: (0, i, 0)),
                  pl.BlockSpec((tm, TOP_K), row),
                  pl.BlockSpec((tm, D_MODEL), row),
                  pl.BlockSpec((tm, D_MODEL), row),
                  pl.BlockSpec((1, 6, D_MODEL), lambda i: (i * tm // chunk, 0, 0)),
                  pl.BlockSpec((1, D_MODEL), lambda i: (0, 0))],
        out_specs=pl.BlockSpec((tm, D_MODEL), row),
        compiler_params=_cparams(("parallel",)),
        name="moe_combine",
    )(yg, gates_t, sh, x1, ada8, g_post)


def _rope_partner(w):
    half = QK_ROPE // 2
    return jnp.concatenate([-w[..., half:], w[..., :half]], axis=-1)


def _prep_weights(w_in, w_uq, w_ukv):
    d = w_in.shape[0]
    zeros = lambda r, c: jnp.zeros((r, c), F32)
    kr = w_in[:, 384:416]
    r_main = jnp.concatenate([zeros(d, QK_NOPE), kr, zeros(d, HEAD_PAD - QK_NOPE - QK_ROPE)], axis=1)
    r_part = jnp.concatenate([zeros(d, QK_NOPE), _rope_partner(kr), zeros(d, HEAD_PAD - QK_NOPE - QK_ROPE)], axis=1)
    win2 = jnp.concatenate([w_in[:, :384], r_main, r_part, w_in[:, 416:]], axis=1).astype(BF16)

    wq = w_uq.reshape(Q_LORA, N_HEADS, QK_NOPE + QK_ROPE)
    zq = jnp.zeros((Q_LORA, N_HEADS, HEAD_PAD - QK_NOPE - QK_ROPE), F32)
    q_main = jnp.concatenate([wq, zq], axis=-1).reshape(Q_LORA, N_HEADS * HEAD_PAD)
    q_part = jnp.concatenate([jnp.zeros((Q_LORA, N_HEADS, QK_NOPE), F32), _rope_partner(wq[..., QK_NOPE:]), zq],
                             axis=-1).reshape(Q_LORA, N_HEADS * HEAD_PAD)
    wuq2 = jnp.concatenate([q_main, q_part], axis=1).astype(BF16)

    wkv = w_ukv.reshape(KV_LORA, N_HEADS, QK_NOPE + HEAD_DIM)
    k_pad = jnp.concatenate([wkv[..., :QK_NOPE], jnp.zeros((KV_LORA, N_HEADS, HEAD_PAD - QK_NOPE), F32)],
                            axis=-1).reshape(KV_LORA, N_HEADS * HEAD_PAD)
    v_cols = wkv[..., QK_NOPE:].reshape(KV_LORA, N_HEADS * HEAD_DIM)
    wukv2 = jnp.concatenate([k_pad, v_cols], axis=1).astype(BF16)
    return win2, wuq2, wukv2


def _rope_table(s_max):
    half = QK_ROPE // 2
    inv = ROPE_BASE ** (-jnp.arange(half, dtype=F32) / half)
    ang = jnp.arange(s_max, dtype=F32)[:, None] * inv[None, :]
    cos = jnp.concatenate([jnp.cos(ang), jnp.cos(ang)], axis=1)
    sin = jnp.concatenate([jnp.sin(ang), jnp.sin(ang)], axis=1)
    scale = (QK_NOPE + QK_ROPE) ** -0.5 * math.log2(math.e)
    pad = jnp.zeros((s_max, HEAD_PAD - QK_NOPE - QK_ROPE), F32)
    zn = jnp.zeros((s_max, QK_NOPE), F32)
    cosq = jnp.concatenate([jnp.full((s_max, QK_NOPE), scale, F32), cos * scale, pad], axis=1)
    sinq = jnp.concatenate([zn, sin * scale, pad], axis=1)
    cosk = jnp.concatenate([zn, cos, pad], axis=1)
    sink = jnp.concatenate([zn, sin, pad], axis=1)
    return jnp.concatenate([cosq, sinq, cosk, sink], axis=1)


def _layer(x_prompt, x_sample, c_prompt, c_sample, w_ada, b_ada, g_pre_mix, w_in, g_qa, w_uq, g_kva, w_ukv,
           g_out_a, g_out_b, w_o, g_post_mix, g_pre_ffn, router_w, router_bias, w_exp_gate, w_exp_up, w_exp_down,
           w_sh_gate, w_sh_up, w_sh_down, g_post_ffn):
    b1, s1, d = x_prompt.shape
    b2, s2, _ = x_sample.shape
    n1, n2 = b1 * s1, b2 * s2
    t_all = n1 + n2
    chunk = math.gcd(s1, s2)
    assert n1 % s2 == 0 and chunk % max(TM_PRE, TM_POST, TM_COMB) == 0

    x_all = jnp.concatenate([x_prompt.reshape(n1, d), x_sample.reshape(n2, d)], axis=0)
    c_all = jnp.concatenate([c_prompt, c_sample, jnp.zeros((8 - b1 - b2, d), F32)], axis=0)
    ada = _ada(c_all, w_ada, b_ada)
    chunk_batch = jnp.concatenate([jnp.repeat(jnp.arange(b1), s1 // chunk),
                                   b1 + jnp.repeat(jnp.arange(b2), s2 // chunk)])
    ada8 = ada.reshape(8, 6, d)[chunk_batch]

    win2, wuq2, wukv2 = _prep_weights(w_in, w_uq, w_ukv)
    tab = _rope_table(max(s1, s2))
    r2 = lambda g: g.reshape(1, -1)
    qa, ka, va, *dil_qkv = _premix(x_all, ada8, chunk, r2(g_pre_mix), win2, r2(g_qa), wuq2, r2(g_kva), wukv2, tab,
                                   n1, s1, s2)

    oa = jnp.concatenate([_mla(qa, ka, va, 0, b1, s1), _mla(qa, ka, va, n1, b2, s2)], axis=0)
    os_, ls_ = [], []
    for p, dil in enumerate(DILATIONS):
        o, lse = _dilated(*dil_qkv[3 * p:3 * p + 3], dil, n1, s1, s2)
        os_.append(o)
        ls_.append(lse)

    x1, h2p, sh, idx_t, gate_t, rank_t, cnt = _postmix(
        x_all, oa, os_, ls_, ada8, chunk, r2(g_out_a), r2(g_out_b), w_o.astype(BF16), r2(g_post_mix), r2(g_pre_ffn),
        router_w.T, router_bias.reshape(N_EXPERTS, 1), w_sh_gate.astype(BF16), w_sh_up.astype(BF16),
        w_sh_down.astype(BF16))

    blk = EXPERT_BLOCK
    n_assign = t_all * TOP_K
    n_blocks = -(-(n_assign + N_EXPERTS * (blk - 1)) // blk)
    counts = cnt[:, 0].astype(jnp.int32)
    padded = (counts + blk - 1) // blk * blk
    pad_end = jnp.cumsum(padded)
    pad_start = pad_end - padded
    dest = _dest(idx_t, rank_t, pad_start)
    block_first = jnp.arange(n_blocks, dtype=jnp.int32) * blk
    block_expert = jnp.minimum(jnp.sum((pad_end[None, :] <= block_first[:, None]).astype(jnp.int32), axis=1),
                               N_EXPERTS - 1)
    n_used = (pad_end[-1] // blk).astype(jnp.int32).reshape(1)
    block_valid = jnp.clip(counts[block_expert] - (block_first - pad_start[block_expert]), 0, blk)
    dest_flat = dest.reshape(n_assign)
    x_slots = _dispatch_rows(h2p, dest_flat, n_blocks * blk)
    y_slots = _experts(block_expert, block_valid, n_used, x_slots, w_exp_gate, w_exp_up, w_exp_down)
    yg = _gather_rows(y_slots, dest_flat).reshape(TOP_K, t_all, d // 2)

    y = _combine(yg, gate_t.T, sh, x1, ada8, chunk, r2(g_post_ffn))
    return y[:n1].reshape(b1, s1, d), y[n1:].reshape(b2, s2, d)


def kernel(x_prompt, x_sample, c_prompt, c_sample, w_ada, b_ada, g_pre_mix, w_in, g_qa, w_uq, g_kva, w_ukv, g_out_a, g_out_b, w_o, g_post_mix, g_pre_ffn, router_w, router_bias, w_exp_gate, w_exp_up, w_exp_down, w_sh_gate, w_sh_up, w_sh_down, g_post_ffn):
    layer = [p[0] for p in (w_ada, b_ada, g_pre_mix, w_in, g_qa, w_uq, g_kva, w_ukv, g_out_a, g_out_b, w_o,
                            g_post_mix, g_pre_ffn, router_w, router_bias, w_exp_gate, w_exp_up, w_exp_down,
                            w_sh_gate, w_sh_up, w_sh_down, g_post_ffn)]
    return _layer(x_prompt, x_sample, c_prompt, c_sample, *layer)
```

```python
import functools
import math

import jax
import jax.numpy as jnp
from jax import lax
from jax.experimental import pallas as pl
from jax.experimental.pallas import tpu as pltpu
from jax.experimental.pallas import tpu_sc as plsc

F32 = jnp.float32
BF16 = jnp.bfloat16

D_MODEL = 1024
HEAD_DIM = 64
N_HEADS = 8
Q_LORA = 256
KV_LORA = 128
QK_NOPE = 64
QK_ROPE = 32
ROPE_BASE = 10000.0
DIL_PATTERNS = ((128, 1), (512, 4), (2048, 16))
DILATIONS = tuple(d for _, d in DIL_PATTERNS)
DIL_HALF = 64
assert all(w // (2 * d) == DIL_HALF for w, d in DIL_PATTERNS) and DILATIONS[0] == 1
N_EXPERTS = 256
TOP_K = 8
N_GROUPS = 8
GROUP_SIZE = N_EXPERTS // N_GROUPS
TOPK_GROUPS = 4
D_EXPERT = 256
ROUTE_SCALE = 2.5
EPS = 1e-6
NEG = -1e30

LANES = 128
HEAD_PAD = 128

TM_PRE = 512
TM_POST = 512
TM_COMB = 512
TQ_MLA = 256
RB_MLA = 64
TM_DEST = 2048
SC_WINDOW = 128
TQ_DIL = 128
EXPERT_BLOCK = 512
VMEM_LIMIT = 56 * 1024 * 1024


def _cparams(sem):
    return pltpu.CompilerParams(dimension_semantics=sem, vmem_limit_bytes=VMEM_LIMIT)


def _rms(x, g):
    return x * lax.rsqrt(jnp.mean(x * x, axis=-1, keepdims=True) + EPS) * g


def _sigmoid(x):
    return 1.0 / (1.0 + jnp.exp(-x))


def _dot(a, b):
    return jnp.dot(a, b, preferred_element_type=F32)


def _group_specs(tm, width, nt1):
    return [pl.BlockSpec((tm, width), lambda i: (jnp.minimum(i, nt1 - 1), 0)),
            pl.BlockSpec((tm, width), lambda i: (jnp.maximum(i - nt1, 0), 0))]


def _group_tile(first_ref, second_ref, nt1):
    return jnp.where(pl.program_id(0) < nt1, first_ref[...], second_ref[...])


def _pack_rows(x):
    n = x.shape[1] // 2
    bits = lax.bitcast_convert_type(x.astype(BF16).astype(F32), jnp.int32)
    return bits[:, :n] | lax.shift_right_logical(bits[:, n:], 16)


def _unpack_rows(u):
    hi = lax.bitcast_convert_type(u & jnp.int32(-65536), F32)
    lo = lax.bitcast_convert_type(lax.shift_left(u, 16), F32)
    return hi, lo


def _dot_nt(a, b, precision=None):
    return lax.dot_general(a, b, (((1,), (1,)), ((), ())), preferred_element_type=F32, precision=precision)


def _ada_kernel(c_ref, w_ref, b_ref, o_ref):
    c = c_ref[...]
    s = c * _sigmoid(c)
    o_ref[...] = jnp.dot(s, w_ref[...], preferred_element_type=F32, precision=lax.Precision.HIGHEST) + b_ref[...]


def _ada(c_all, w_ada, b_ada):
    nb, d = c_all.shape
    n_out = w_ada.shape[1]
    tn = 1024
    return pl.pallas_call(
        _ada_kernel,
        out_shape=jax.ShapeDtypeStruct((nb, n_out), F32),
        grid=(n_out // tn,),
        in_specs=[pl.BlockSpec((nb, d), lambda j: (0, 0)),
                  pl.BlockSpec((d, tn), lambda j: (0, j)),
                  pl.BlockSpec((1, tn), lambda j: (0, j))],
        out_specs=pl.BlockSpec((nb, tn), lambda j: (0, j)),
        compiler_params=_cparams(("arbitrary",)),
        name="ada",
    )(c_all, w_ada, b_ada.reshape(1, n_out))


def _premix_kernel(xp_ref, xs_ref, ada_ref, g_ref, win_ref, gqa_ref, wuq_ref, gkva_ref, wukv_ref, tab_ref,
                   qa_ref, ka_ref, va_ref, *rest, nt1):
    dil_refs, zs_ref = rest[:-1], rest[-1]
    x = _group_tile(xp_ref, xs_ref, nt1)
    sh1 = ada_ref[0, 0:1, :]
    sc1 = ada_ref[0, 1:2, :]
    h = _rms(x, g_ref[...]) * (1.0 + sc1) + sh1
    z = _dot(h.astype(BF16), win_ref[...])
    tab = tab_ref[...]
    cosq, sinq = tab[:, 0:128], tab[:, 128:256]
    cosk, sink = tab[:, 256:384], tab[:, 384:512]
    nh = N_HEADS * HEAD_PAD

    cq = _rms(z[:, 0:Q_LORA], gqa_ref[...]).astype(BF16)
    qq = _dot(cq, wuq_ref[...])
    for h_i in range(N_HEADS):
        lo = h_i * HEAD_PAD
        qa_ref[:, lo:lo + HEAD_PAD] = (qq[:, lo:lo + HEAD_PAD] * cosq
                                       + qq[:, nh + lo:nh + lo + HEAD_PAD] * sinq).astype(BF16)

    ckv = _rms(z[:, 256:384], gkva_ref[...]).astype(BF16)
    kk = _dot(ckv, wukv_ref[...])
    rr = z[:, 384:512] * cosk + z[:, 512:640] * sink
    for h_i in range(N_HEADS):
        lo = h_i * HEAD_PAD
        ka_ref[:, lo:lo + HEAD_PAD] = (kk[:, lo:lo + HEAD_PAD] + rr).astype(BF16)
    va_ref[...] = kk[:, nh:nh + 512].astype(BF16)

    tm = x.shape[0]
    wd = N_HEADS * HEAD_DIM
    n_slab = 3 * wd // LANES
    for c in range(n_slab):
        scale = HEAD_DIM ** -0.5 if c < wd // LANES else 1.0
        zs_ref[c] = z[:, 640 + c * LANES:640 + (c + 1) * LANES] * scale
    for dil, refs in zip(DILATIONS, (dil_refs[0:3], dil_refs[3:6], dil_refs[6:9])):
        n = tm // dil
        for r in range(dil):
            for c in range(n_slab):
                rows = zs_ref[c] if dil == 1 else zs_ref.at[c][pl.ds(r, n, stride=dil), :]
                col = r * wd + (c % (wd // LANES)) * LANES
                refs[c // (wd // LANES)][:, col:col + LANES] = rows.astype(BF16)


def _premix(x_p, x_s, ada8, chunk, g_pre, win2, g_qa, wuq2, g_kva, wukv2, tab, n1, s1, s2):
    t_all = x_p.shape[0] + x_s.shape[0]
    tm = TM_PRE
    nt1 = n1 // tm
    tb1, tb2 = s1 // tm, s2 // tm

    def tab_map(i):
        return (jnp.where(i < nt1, i % tb1, (i - nt1) % tb2), 0)

    row = lambda i: (i, 0)
    const = lambda i: (0, 0)
    wd = N_HEADS * HEAD_DIM
    outs = [jax.ShapeDtypeStruct((t_all, w), BF16) for w in (1024, 1024, 512)]
    out_specs = [pl.BlockSpec((tm, w), row) for w in (1024, 1024, 512)]
    for dil in DILATIONS:
        outs += [jax.ShapeDtypeStruct((t_all // dil, dil * wd), BF16)] * 3
        out_specs += [pl.BlockSpec((tm // dil, dil * wd), row)] * 3
    return pl.pallas_call(
        functools.partial(_premix_kernel, nt1=nt1),
        out_shape=outs,
        grid=(t_all // tm,),
        in_specs=_group_specs(tm, D_MODEL, nt1) + [
                  pl.BlockSpec((1, 6, D_MODEL), lambda i: (i * tm // chunk, 0, 0)),
                  pl.BlockSpec((1, D_MODEL), const),
                  pl.BlockSpec(win2.shape, const),
                  pl.BlockSpec((1, Q_LORA), const),
                  pl.BlockSpec(wuq2.shape, const),
                  pl.BlockSpec((1, KV_LORA), const),
                  pl.BlockSpec(wukv2.shape, const),
                  pl.BlockSpec((tm, 512), tab_map)],
        out_specs=out_specs,
        scratch_shapes=[pltpu.VMEM((3 * wd // LANES, tm, LANES), F32)],
        compiler_params=_cparams(("parallel",)),
        name="premix",
    )(x_p, x_s, ada8, g_pre, win2, g_qa, wuq2, g_kva, wukv2, tab)


def _mla_kernel(q_ref, k_ref, v_ref, o_ref, s_ref, p_ref, m_ref, l_ref, *, rb):
    tq = q_ref.shape[0]
    lane = lax.broadcasted_iota(jnp.int32, (tq, LANES), 1)

    def scores(hh):
        s = _dot_nt(q_ref[:, hh * HEAD_PAD:(hh + 1) * HEAD_PAD], k_ref[:, hh * HEAD_PAD:(hh + 1) * HEAD_PAD])
        s_ref[hh] = s
        m_ref[hh] = jnp.max(s, axis=-1, keepdims=True)

    def numerators(hh):
        for r0 in range(0, tq, rb):
            p = jnp.exp2(s_ref[hh, r0:r0 + rb, :] - m_ref[hh, r0:r0 + rb, :])
            l_ref[hh, r0:r0 + rb, :] = jnp.sum(p, axis=-1, keepdims=True)
            p_ref[hh, r0:r0 + rb, :] = p.astype(BF16)

    def values(hh):
        return _dot(p_ref[hh], v_ref[...]) / l_ref[hh]

    scores(0)
    scores(1)
    numerators(0)
    o0 = values(0)
    numerators(1)
    o1 = values(1)
    o_ref[...] = jnp.where(lane < HEAD_DIM, o0, o1).astype(o_ref.dtype)


def _mla(qa, ka, va, row_off, batch, seq):
    tq = TQ_MLA
    nq = seq // tq
    qoff = row_off // tq
    soff = row_off // seq
    return pl.pallas_call(
        functools.partial(_mla_kernel, rb=RB_MLA),
        out_shape=jax.ShapeDtypeStruct((batch * seq, N_HEADS * HEAD_DIM), BF16),
        grid=(batch, N_HEADS // 2, nq),
        in_specs=[pl.BlockSpec((tq, 2 * HEAD_PAD), lambda b, j, qi: (qoff + b * nq + qi, j)),
                  pl.BlockSpec((seq, 2 * HEAD_PAD), lambda b, j, qi: (soff + b, j)),
                  pl.BlockSpec((seq, LANES), lambda b, j, qi: (soff + b, j))],
        out_specs=pl.BlockSpec((tq, LANES), lambda b, j, qi: (b * nq + qi, j)),
        scratch_shapes=[pltpu.VMEM((2, tq, seq), F32), pltpu.VMEM((2, tq, seq), BF16),
                        pltpu.VMEM((2, tq, 1), F32), pltpu.VMEM((2, tq, 1), F32)],
        compiler_params=_cparams(("parallel", "parallel", "parallel")),
        name="mla_attention",
    )(qa, ka, va)


def _dil_kernel(q_ref, kp_ref, kc_ref, kn_ref, vp_ref, vc_ref, vn_ref, o_ref, lse_ref, *, dil, n1_rows, len1, len2):
    tq = q_ref.shape[0]
    half = DIL_HALF
    w = tq + 2 * half
    row0 = pl.program_id(0) * tq
    in1 = row0 < n1_rows
    seq_len = jnp.where(in1, len1, len2)
    q0 = jnp.where(in1, row0 % len1, (row0 - n1_rows) % len2)

    kw = jnp.concatenate([kp_ref[tq - half:tq, :], kc_ref[...], kn_ref[0:half, :]], axis=0)
    vw = jnp.concatenate([vp_ref[tq - half:tq, :], vc_ref[...], vn_ref[0:half, :]], axis=0)
    col = lax.broadcasted_iota(jnp.int32, (tq, w), 1)
    rw = lax.broadcasted_iota(jnp.int32, (tq, w), 0)
    absd = jnp.abs(col - half - rw)
    kpos = q0 + col - half
    valid = (absd <= half) & (kpos >= 0) & (kpos < seq_len)
    dist = (absd * dil).astype(F32)
    lane = lax.broadcasted_iota(jnp.int32, (tq, LANES), 1)
    low = lane < HEAD_DIM
    for j in range(N_HEADS // 2):
        kpair = kw[:, j * LANES:(j + 1) * LANES]
        vpair = vw[:, j * LANES:(j + 1) * LANES]
        qpair = q_ref[:, j * LANES:(j + 1) * LANES]
        zero = jnp.zeros_like(qpair)
        res = []
        for hh in range(2):
            h_i = 2 * j + hh
            slope = 2.0 ** (-8.0 * (h_i + 1) / N_HEADS)
            q = jnp.where(low, qpair, zero) if hh == 0 else jnp.where(low, zero, qpair)
            s = _dot_nt(q, kpair)
            s = jnp.where(valid, s - slope * dist, NEG)
            m = jnp.max(s, axis=-1, keepdims=True)
            p = jnp.exp(s - m)
            l = jnp.sum(p, axis=-1, keepdims=True)
            res.append((_dot(p.astype(BF16), vpair) / l, m + jnp.log(l)))
        o_ref[:, j * LANES:(j + 1) * LANES] = jnp.where(low, res[0][0], res[1][0]).astype(o_ref.dtype)
        lse_ref[:, j * LANES:(j + 1) * LANES] = jnp.where(low, res[0][1], res[1][1])


def _dilated(qv, kv, vv, dil, n1, s1, s2):
    rows = qv.shape[0]
    tq = TQ_DIL
    nt = rows // tq
    wk = N_HEADS * HEAD_DIM
    cur = lambda i, r: (i, r)
    prv = lambda i, r: (jnp.maximum(i - 1, 0), r)
    nxt = lambda i, r: (jnp.minimum(i + 1, nt - 1), r)
    kspecs = [pl.BlockSpec((tq, wk), f) for f in (prv, cur, nxt)]
    return pl.pallas_call(
        functools.partial(_dil_kernel, dil=dil, n1_rows=n1 // dil, len1=s1 // dil, len2=s2 // dil),
        out_shape=[jax.ShapeDtypeStruct((rows, dil * wk), BF16), jax.ShapeDtypeStruct((rows, dil * wk), F32)],
        grid=(nt, dil),
        in_specs=[pl.BlockSpec((tq, wk), cur)] + kspecs + kspecs,
        out_specs=[pl.BlockSpec((tq, wk), cur), pl.BlockSpec((tq, wk), cur)],
        compiler_params=_cparams(("parallel", "parallel")),
        name="dilated_attention_d%d" % dil,
    )(qv, kv, kv, kv, vv, vv, vv)


def _postmix_kernel(xp_ref, xs_ref, oap_ref, oas_ref, o1_ref, o2_ref, o3_ref, l1_ref, l2_ref, l3_ref, ada_ref, goa_ref, gob_ref,
                    wo_ref, gpm_ref, gpf_ref, rwt_ref, rb_ref, wsg_ref, wsu_ref, wsd_ref,
                    x1_ref, h2_ref, sh_ref, idx_ref, gate_ref, rank_ref, cnt_ref, carry_ref, nat_ref, *, nt1):
    tm = xp_ref.shape[0]

    @pl.when(pl.program_id(0) == 0)
    def _():
        carry_ref[...] = jnp.zeros_like(carry_ref)

    gt1, sh2, sc2 = ada_ref[0, 2:3, :], ada_ref[0, 3:4, :], ada_ref[0, 4:5, :]

    wd = N_HEADS * HEAD_DIM
    n_slab = wd // LANES

    def row_order(src_ref, dil, base):
        if dil == 1:
            return src_ref[...].astype(F32)
        n = tm // dil
        for r in range(dil):
            for c in range(n_slab):
                col = r * wd + c * LANES
                nat_ref.at[base + c][pl.ds(r, n, stride=dil), :] = src_ref[:, col:col + LANES].astype(F32)
        return jnp.concatenate([nat_ref[base + c] for c in range(n_slab)], axis=1)

    o_pat = [row_order(ref, dil, (2 * p) * n_slab) for p, (ref, dil) in enumerate(zip((o1_ref, o2_ref, o3_ref), DILATIONS))]
    la, lb, lc = [row_order(ref, dil, (2 * p + 1) * n_slab)
                  for p, (ref, dil) in enumerate(zip((l1_ref, l2_ref, l3_ref), DILATIONS))]

    mx = jnp.maximum(jnp.maximum(la, lb), lc)
    ea, eb, ec = jnp.exp(la - mx), jnp.exp(lb - mx), jnp.exp(lc - mx)
    ob = (ea * o_pat[0] + eb * o_pat[1] + ec * o_pat[2]) / (ea + eb + ec)

    na = _rms(_group_tile(oap_ref, oas_ref, nt1).astype(F32), goa_ref[...]).astype(BF16)
    nb = _rms(ob, gob_ref[...]).astype(BF16)
    half_w = N_HEADS * HEAD_DIM
    o = _dot(na, wo_ref[0:half_w, :]) + _dot(nb, wo_ref[half_w:2 * half_w, :])
    x1 = _group_tile(xp_ref, xs_ref, nt1) + gt1 * _rms(o, gpm_ref[...])
    x1_ref[...] = x1
    h2 = _rms(x1, gpf_ref[...]) * (1.0 + sc2) + sh2
    h2b = h2.astype(BF16)
    h2_ref[...] = _pack_rows(h2)

    g = _dot(h2b, wsg_ref[...])
    u = _dot(h2b, wsu_ref[...])
    sh_ref[...] = _dot((g * _sigmoid(g) * u).astype(BF16), wsd_ref[...])

    scores = _sigmoid(_dot_nt(rwt_ref[...], h2, precision=lax.Precision.HIGHEST))
    biased = scores + rb_ref[...]
    ninf = -jnp.inf
    row = lax.broadcasted_iota(jnp.int32, (N_EXPERTS, tm), 0)
    rwg = lax.broadcasted_iota(jnp.int32, (GROUP_SIZE, tm), 0)
    gsc = []
    for gi in range(N_GROUPS):
        blk = biased[gi * GROUP_SIZE:(gi + 1) * GROUP_SIZE]
        m1 = jnp.max(blk, axis=0, keepdims=True)
        i1 = jnp.min(jnp.where(blk == m1, rwg, N_EXPERTS), axis=0, keepdims=True)
        m2 = jnp.max(jnp.where(rwg == i1, ninf, blk), axis=0, keepdims=True)
        gsc.append(m1 + m2)
    gsel = [jnp.zeros((1, tm), F32) for _ in range(N_GROUPS)]
    for _ in range(TOPK_GROUPS):
        m = functools.reduce(jnp.maximum, gsc)
        free = jnp.ones((1, tm), F32)
        for gi in range(N_GROUPS):
            hit = jnp.where(gsc[gi] == m, free, 0.0)
            free = free - hit
            gsel[gi] = gsel[gi] + hit
            gsc[gi] = jnp.where(hit > 0.0, ninf, gsc[gi])
    masked = jnp.concatenate(
        [jnp.where(gsel[gi] > 0.0, biased[gi * GROUP_SIZE:(gi + 1) * GROUP_SIZE], ninf)
         for gi in range(N_GROUPS)], axis=0)
    cur = masked
    idxs, gts = [], []
    for _ in range(TOP_K):
        m = jnp.max(cur, axis=0, keepdims=True)
        ik = jnp.min(jnp.where(cur == m, row, N_EXPERTS), axis=0, keepdims=True)
        hit = row == ik
        idxs.append(ik)
        gts.append(jnp.sum(jnp.where(hit, scores, 0.0), axis=0, keepdims=True))
        cur = jnp.where(hit, ninf, cur)
    idx = jnp.concatenate(idxs, axis=0)
    gates = jnp.concatenate(gts, axis=0)
    idx_ref[...] = idx
    gate_ref[...] = gates / jnp.sum(gates, axis=0, keepdims=True) * ROUTE_SCALE

    sel = jnp.where(cur != masked, 1.0, 0.0)
    tr = lax.broadcasted_iota(jnp.int32, (tm, tm), 0)
    tc = lax.broadcasted_iota(jnp.int32, (tm, tm), 1)
    before = jnp.where(tr < tc, 1.0, 0.0).astype(BF16)
    pos = _dot(sel.astype(BF16), before) + carry_ref[...]
    rank_ref[...] = jnp.concatenate(
        [jnp.sum(jnp.where(row == idxs[k], pos, 0.0), axis=0, keepdims=True) for k in range(TOP_K)],
        axis=0).astype(jnp.int32)
    carry = carry_ref[...] + jnp.sum(sel, axis=1, keepdims=True)
    carry_ref[...] = carry
    cnt_ref[...] = jnp.broadcast_to(carry, cnt_ref.shape)


def _postmix(x_p, x_s, oa_p, oa_s, os_, ls_, ada8, chunk, g_oa, g_ob, wo, g_pm, g_pf, rwt, rb, wsg, wsu, wsd):
    t_all = x_p.shape[0] + x_s.shape[0]
    tm = TM_POST
    nt1 = x_p.shape[0] // tm
    row = lambda i: (i, 0)
    col = lambda i: (0, i)
    const = lambda i: (0, 0)
    hw = N_HEADS * HEAD_DIM
    pat_specs = [pl.BlockSpec((tm // dil, dil * hw), row) for dil in DILATIONS]
    in_specs = (_group_specs(tm, D_MODEL, nt1) + _group_specs(tm, hw, nt1)
                + pat_specs + pat_specs
                + [pl.BlockSpec((1, 6, D_MODEL), lambda i: (i * tm // chunk, 0, 0)),
                   pl.BlockSpec((1, hw), const), pl.BlockSpec((1, hw), const),
                   pl.BlockSpec(wo.shape, const),
                   pl.BlockSpec((1, D_MODEL), const), pl.BlockSpec((1, D_MODEL), const),
                   pl.BlockSpec(rwt.shape, const), pl.BlockSpec((N_EXPERTS, 1), const),
                   pl.BlockSpec(wsg.shape, const), pl.BlockSpec(wsu.shape, const), pl.BlockSpec(wsd.shape, const)])
    out_shape = [jax.ShapeDtypeStruct((t_all, D_MODEL), F32),
                 jax.ShapeDtypeStruct((t_all, D_MODEL // 2), jnp.int32),
                 jax.ShapeDtypeStruct((t_all, D_MODEL), F32),
                 jax.ShapeDtypeStruct((TOP_K, t_all), jnp.int32),
                 jax.ShapeDtypeStruct((TOP_K, t_all), F32),
                 jax.ShapeDtypeStruct((TOP_K, t_all), jnp.int32),
                 jax.ShapeDtypeStruct((N_EXPERTS, LANES), F32)]
    out_specs = [pl.BlockSpec((tm, D_MODEL), row), pl.BlockSpec((tm, D_MODEL // 2), row), pl.BlockSpec((tm, D_MODEL), row),
                 pl.BlockSpec((TOP_K, tm), col), pl.BlockSpec((TOP_K, tm), col), pl.BlockSpec((TOP_K, tm), col),
                 pl.BlockSpec((N_EXPERTS, LANES), const)]
    return pl.pallas_call(
        functools.partial(_postmix_kernel, nt1=nt1),
        out_shape=out_shape,
        grid=(t_all // tm,),
        in_specs=in_specs,
        out_specs=out_specs,
        scratch_shapes=[pltpu.VMEM((N_EXPERTS, 1), F32),
                        pltpu.VMEM((2 * len(DILATIONS) * hw // LANES, tm, LANES), F32)],
        compiler_params=_cparams(("arbitrary",)),
        name="postmix_router",
    )(x_p, x_s, oa_p, oa_s, *os_, *ls_, ada8, g_oa, g_ob, wo, g_pm, g_pf, rwt, rb, wsg, wsu, wsd)


def _dest_kernel(idx_ref, rank_ref, start_ref, dest_ref):
    tm = idx_ref.shape[1]
    row = lax.broadcasted_iota(jnp.int32, (N_EXPERTS, tm), 0)
    start = start_ref[...]
    base = [jnp.sum(jnp.where(row == idx_ref[k:k + 1, :], start, 0), axis=0, keepdims=True) for k in range(TOP_K)]
    dest_ref[...] = jnp.concatenate(base, axis=0) + rank_ref[...]


def _dest(idx_t, rank_t, pad_start):
    t_all = idx_t.shape[1]
    tm = TM_DEST
    col = lambda i: (0, i)
    return pl.pallas_call(
        _dest_kernel,
        out_shape=jax.ShapeDtypeStruct((TOP_K, t_all), jnp.int32),
        grid=(t_all // tm,),
        in_specs=[pl.BlockSpec((TOP_K, tm), col), pl.BlockSpec((TOP_K, tm), col),
                  pl.BlockSpec((N_EXPERTS, 1), lambda i: (0, 0))],
        out_specs=pl.BlockSpec((TOP_K, tm), col),
        compiler_params=_cparams(("parallel",)),
        name="slot_index",
    )(idx_t, rank_t, pad_start.reshape(N_EXPERTS, 1))


def _sc_mesh():
    return plsc.VectorSubcoreMesh(core_axis_name="core", subcore_axis_name="subcore")


def _dispatch_rows(rows, dest, n_slots):
    t_all, width = rows.shape
    win = SC_WINDOW
    info = plsc.get_sparse_core_info()
    n_workers = info.num_cores * info.num_subcores
    per_worker = t_all // n_workers
    assert per_worker % win == 0

    @functools.partial(
        pl.kernel, out_type=jax.ShapeDtypeStruct((n_slots, width), rows.dtype), mesh=_sc_mesh(),
        scratch_types=[pltpu.VMEM((TOP_K, win), jnp.int32), pltpu.VMEM((win, width), rows.dtype),
                       pltpu.SemaphoreType.DMA])
    def scatter_kernel(x_hbm, i_hbm, o_hbm, idx_v, rows_v, sem):
        base = (lax.axis_index("subcore") * info.num_cores + lax.axis_index("core")) * per_worker

        @pl.loop(0, per_worker // win)
        def _(w):
            off = base + w * win
            for k in range(TOP_K):
                pltpu.sync_copy(i_hbm.at[pl.ds(k * t_all + off, win)], idx_v.at[k])
            pltpu.sync_copy(x_hbm.at[pl.ds(off, win)], rows_v)
            copies = [pltpu.async_copy(rows_v, o_hbm.at[idx_v.at[k]], sem) for k in range(TOP_K)]
            for c in copies:
                c.wait()

    return scatter_kernel(rows, dest)


def _gather_rows(slots, index):
    n = index.shape[0]
    width = slots.shape[1]
    win = SC_WINDOW
    info = plsc.get_sparse_core_info()
    n_workers = info.num_cores * info.num_subcores
    per_worker = n // n_workers
    assert per_worker % win == 0

    @functools.partial(
        pl.kernel, out_type=jax.ShapeDtypeStruct((n, width), slots.dtype), mesh=_sc_mesh(),
        scratch_types=[pltpu.VMEM((win,), jnp.int32), pltpu.VMEM((win, width), slots.dtype),
                       pltpu.SemaphoreType.DMA])
    def gather_kernel(y_hbm, i_hbm, o_hbm, idx_v, rows_v, sem):
        base = (lax.axis_index("subcore") * info.num_cores + lax.axis_index("core")) * per_worker

        @pl.loop(0, per_worker // win)
        def _(w):
            off = base + w * win
            pltpu.sync_copy(i_hbm.at[pl.ds(off, win)], idx_v)
            pltpu.async_copy(y_hbm.at[idx_v], rows_v, sem).wait()
            pltpu.sync_copy(rows_v, o_hbm.at[pl.ds(off, win)])

    return gather_kernel(slots, index)


def _expert_kernel(be_ref, bv_ref, nx_ref, nu_ref, x_ref, wg_hbm, wu_hbm, wd_hbm, y_ref,
                   wg_f, wu_f, wd_f, wgu_s, wd_s, sems):
    i = pl.program_id(0)
    used = i < nu_ref[0]
    expert = be_ref[i]
    new_expert = (i == 0) | (expert != be_ref[jnp.maximum(i - 1, 0)])

    def weight_copies(e):
        return (pltpu.make_async_copy(wg_hbm.at[e], wg_f, sems.at[0]),
                pltpu.make_async_copy(wu_hbm.at[e], wu_f, sems.at[1]),
                pltpu.make_async_copy(wd_hbm.at[e], wd_f, sems.at[2]))

    @pl.when(used & (i == 0))
    def _():
        for c in weight_copies(expert):
            c.start()

    @pl.when(used & new_expert)
    def _():
        for c in weight_copies(expert):
            c.wait()
        wgu_s[:, 0:D_EXPERT] = wg_f[...].astype(BF16)
        wgu_s[:, D_EXPERT:2 * D_EXPERT] = wu_f[...].astype(BF16)
        wd_s[...] = wd_f[...].astype(BF16)
        nxt = nx_ref[i]

        @pl.when(nxt >= 0)
        def _():
            for c in weight_copies(nxt):
                c.start()

    @pl.when(used)
    def _():
        live = lax.broadcasted_iota(jnp.int32, x_ref.shape, 0) < bv_ref[i]
        xa, xb = _unpack_rows(jnp.where(live, x_ref[...], 0))
        half = D_MODEL // 2
        gu = _dot(xa.astype(BF16), wgu_s[0:half, :]) + _dot(xb.astype(BF16), wgu_s[half:D_MODEL, :])
        g, u = gu[:, 0:D_EXPERT], gu[:, D_EXPERT:2 * D_EXPERT]
        y_ref[...] = _pack_rows(_dot((g * _sigmoid(g) * u).astype(BF16), wd_s[...]))

    @pl.when(jnp.logical_not(used))
    def _():
        y_ref[...] = jnp.zeros_like(y_ref)


def _experts(block_expert, block_valid, block_next, n_used, x_slots, wg, wu, wd):
    p_rows = x_slots.shape[0]
    blk = EXPERT_BLOCK
    hbm = pl.BlockSpec(memory_space=pl.ANY)
    grid_spec = pltpu.PrefetchScalarGridSpec(
        num_scalar_prefetch=4,
        grid=(p_rows // blk,),
        in_specs=[pl.BlockSpec((blk, D_MODEL // 2),
                               lambda i, be, bv, nx, nu: (jnp.maximum(jnp.minimum(i, nu[0] - 1), 0), 0)),
                  hbm, hbm, hbm],
        out_specs=pl.BlockSpec((blk, D_MODEL // 2), lambda i, be, bv, nx, nu: (i, 0)),
        scratch_shapes=[pltpu.VMEM((D_MODEL, D_EXPERT), F32), pltpu.VMEM((D_MODEL, D_EXPERT), F32),
                        pltpu.VMEM((D_EXPERT, D_MODEL), F32),
                        pltpu.VMEM((D_MODEL, 2 * D_EXPERT), BF16), pltpu.VMEM((D_EXPERT, D_MODEL), BF16),
                        pltpu.SemaphoreType.DMA((3,))])
    return pl.pallas_call(
        _expert_kernel,
        out_shape=jax.ShapeDtypeStruct((p_rows, D_MODEL // 2), jnp.int32),
        grid_spec=grid_spec,
        compiler_params=_cparams(("arbitrary",)),
        name="expert_ffn",
    )(block_expert, block_valid, block_next, n_used, x_slots, wg, wu, wd)


def _combine_kernel(yg_ref, gate_ref, sh_ref, x1_ref, ada_ref, g_ref, yp_ref, ys_ref, *, nt1):
    gates = gate_ref[...]
    half = D_MODEL // 2
    acc_a = sh_ref[:, 0:half]
    acc_b = sh_ref[:, half:D_MODEL]
    for k in range(TOP_K):
        ya, yb = _unpack_rows(yg_ref[k])
        acc_a = acc_a + gates[:, k:k + 1] * ya
        acc_b = acc_b + gates[:, k:k + 1] * yb
    gt2 = ada_ref[0, 5:6, :]
    y = x1_ref[...] + gt2 * _rms(jnp.concatenate([acc_a, acc_b], axis=1), g_ref[...])
    first = pl.program_id(0) < nt1

    @pl.when(first)
    def _():
        yp_ref[...] = y

    @pl.when(jnp.logical_not(first))
    def _():
        ys_ref[...] = y


def _combine(yg, gates_t, sh, x1, ada8, chunk, g_post, n1):
    t_all = x1.shape[0]
    tm = TM_COMB
    nt1 = n1 // tm
    row = lambda i: (i, 0)
    return pl.pallas_call(
        functools.partial(_combine_kernel, nt1=nt1),
        out_shape=[jax.ShapeDtypeStruct((n1, D_MODEL), F32), jax.ShapeDtypeStruct((t_all - n1, D_MODEL), F32)],
        grid=(t_all // tm,),
        in_specs=[pl.BlockSpec((TOP_K, tm, D_MODEL // 2), lambda i: (0, i, 0)),
                  pl.BlockSpec((tm, TOP_K), row),
                  pl.BlockSpec((tm, D_MODEL), row),
                  pl.BlockSpec((tm, D_MODEL), row),
                  pl.BlockSpec((1, 6, D_MODEL), lambda i: (i * tm // chunk, 0, 0)),
                  pl.BlockSpec((1, D_MODEL), lambda i: (0, 0))],
        out_specs=_group_specs(tm, D_MODEL, nt1),
        compiler_params=_cparams(("arbitrary",)),
        name="moe_combine",
    )(yg, gates_t, sh, x1, ada8, g_post)


def _rope_partner(w):
    half = QK_ROPE // 2
    return jnp.concatenate([-w[..., half:], w[..., :half]], axis=-1)


def _prep_weights(w_in, w_uq, w_ukv):
    d = w_in.shape[0]
    zeros = lambda r, c: jnp.zeros((r, c), F32)
    kr = w_in[:, 384:416]
    r_main = jnp.concatenate([zeros(d, QK_NOPE), kr, zeros(d, HEAD_PAD - QK_NOPE - QK_ROPE)], axis=1)
    r_part = jnp.concatenate([zeros(d, QK_NOPE), _rope_partner(kr), zeros(d, HEAD_PAD - QK_NOPE - QK_ROPE)], axis=1)
    win2 = jnp.concatenate([w_in[:, :384], r_main, r_part, w_in[:, 416:]], axis=1).astype(BF16)

    wq = w_uq.reshape(Q_LORA, N_HEADS, QK_NOPE + QK_ROPE)
    zq = jnp.zeros((Q_LORA, N_HEADS, HEAD_PAD - QK_NOPE - QK_ROPE), F32)
    q_main = jnp.concatenate([wq, zq], axis=-1).reshape(Q_LORA, N_HEADS * HEAD_PAD)
    q_part = jnp.concatenate([jnp.zeros((Q_LORA, N_HEADS, QK_NOPE), F32), _rope_partner(wq[..., QK_NOPE:]), zq],
                             axis=-1).reshape(Q_LORA, N_HEADS * HEAD_PAD)
    wuq2 = jnp.concatenate([q_main, q_part], axis=1).astype(BF16)

    wkv = w_ukv.reshape(KV_LORA, N_HEADS, QK_NOPE + HEAD_DIM)
    k_pad = jnp.concatenate([wkv[..., :QK_NOPE], jnp.zeros((KV_LORA, N_HEADS, HEAD_PAD - QK_NOPE), F32)],
                            axis=-1).reshape(KV_LORA, N_HEADS * HEAD_PAD)
    v_cols = wkv[..., QK_NOPE:].reshape(KV_LORA, N_HEADS * HEAD_DIM)
    wukv2 = jnp.concatenate([k_pad, v_cols], axis=1).astype(BF16)
    return win2, wuq2, wukv2


def _rope_table(s_max):
    half = QK_ROPE // 2
    inv = ROPE_BASE ** (-jnp.arange(half, dtype=F32) / half)
    ang = jnp.arange(s_max, dtype=F32)[:, None] * inv[None, :]
    cos = jnp.concatenate([jnp.cos(ang), jnp.cos(ang)], axis=1)
    sin = jnp.concatenate([jnp.sin(ang), jnp.sin(ang)], axis=1)
    scale = (QK_NOPE + QK_ROPE) ** -0.5 * math.log2(math.e)
    pad = jnp.zeros((s_max, HEAD_PAD - QK_NOPE - QK_ROPE), F32)
    zn = jnp.zeros((s_max, QK_NOPE), F32)
    cosq = jnp.concatenate([jnp.full((s_max, QK_NOPE), scale, F32), cos * scale, pad], axis=1)
    sinq = jnp.concatenate([zn, sin * scale, pad], axis=1)
    cosk = jnp.concatenate([zn, cos, pad], axis=1)
    sink = jnp.concatenate([zn, sin, pad], axis=1)
    return jnp.concatenate([cosq, sinq, cosk, sink], axis=1)


def _layer(x_prompt, x_sample, c_prompt, c_sample, w_ada, b_ada, g_pre_mix, w_in, g_qa, w_uq, g_kva, w_ukv,
           g_out_a, g_out_b, w_o, g_post_mix, g_pre_ffn, router_w, router_bias, w_exp_gate, w_exp_up, w_exp_down,
           w_sh_gate, w_sh_up, w_sh_down, g_post_ffn):
    b1, s1, d = x_prompt.shape
    b2, s2, _ = x_sample.shape
    n1, n2 = b1 * s1, b2 * s2
    t_all = n1 + n2
    chunk = math.gcd(s1, s2)
    assert n1 % s2 == 0 and chunk % max(TM_PRE, TM_POST, TM_COMB) == 0

    x_p, x_s = x_prompt.reshape(n1, d), x_sample.reshape(n2, d)
    c_all = jnp.concatenate([c_prompt, c_sample, jnp.zeros((8 - b1 - b2, d), F32)], axis=0)
    ada = _ada(c_all, w_ada, b_ada)
    chunk_batch = jnp.concatenate([jnp.repeat(jnp.arange(b1), s1 // chunk),
                                   b1 + jnp.repeat(jnp.arange(b2), s2 // chunk)])
    ada8 = ada.reshape(8, 6, d)[chunk_batch]

    win2, wuq2, wukv2 = _prep_weights(w_in, w_uq, w_ukv)
    tab = _rope_table(max(s1, s2))
    r2 = lambda g: g.reshape(1, -1)
    qa, ka, va, *dil_qkv = _premix(x_p, x_s, ada8, chunk, r2(g_pre_mix), win2, r2(g_qa), wuq2, r2(g_kva), wukv2, tab,
                                   n1, s1, s2)

    oa_p, oa_s = _mla(qa, ka, va, 0, b1, s1), _mla(qa, ka, va, n1, b2, s2)
    os_, ls_ = [], []
    for p, dil in enumerate(DILATIONS):
        o, lse = _dilated(*dil_qkv[3 * p:3 * p + 3], dil, n1, s1, s2)
        os_.append(o)
        ls_.append(lse)

    x1, h2p, sh, idx_t, gate_t, rank_t, cnt = _postmix(
        x_p, x_s, oa_p, oa_s, os_, ls_, ada8, chunk, r2(g_out_a), r2(g_out_b), w_o.astype(BF16), r2(g_post_mix), r2(g_pre_ffn),
        router_w.T, router_bias.reshape(N_EXPERTS, 1), w_sh_gate.astype(BF16), w_sh_up.astype(BF16),
        w_sh_down.astype(BF16))

    blk = EXPERT_BLOCK
    n_assign = t_all * TOP_K
    n_blocks = -(-(n_assign + N_EXPERTS * (blk - 1)) // blk)
    counts = cnt[:, 0].astype(jnp.int32)
    padded = (counts + blk - 1) // blk * blk
    pad_end = jnp.cumsum(padded)
    pad_start = pad_end - padded
    dest = _dest(idx_t, rank_t, pad_start)
    block_first = jnp.arange(n_blocks, dtype=jnp.int32) * blk
    block_expert = jnp.minimum(jnp.sum((pad_end[None, :] <= block_first[:, None]).astype(jnp.int32), axis=1),
                               N_EXPERTS - 1)
    n_used = (pad_end[-1] // blk).astype(jnp.int32).reshape(1)
    block_valid = jnp.clip(counts[block_expert] - (block_first - pad_start[block_expert]), 0, blk)
    eid = jnp.arange(N_EXPERTS, dtype=jnp.int32)
    used_from = lax.cummin(jnp.where(counts > 0, eid, N_EXPERTS), axis=0, reverse=True)
    next_used = jnp.concatenate([used_from[1:], jnp.full((1,), N_EXPERTS, jnp.int32)])
    block_next = jnp.where(next_used < N_EXPERTS, next_used, -1)[block_expert]
    dest_flat = dest.reshape(n_assign)
    x_slots = _dispatch_rows(h2p, dest_flat, n_blocks * blk)
    y_slots = _experts(block_expert, block_valid, block_next, n_used, x_slots, w_exp_gate, w_exp_up, w_exp_down)
    yg = _gather_rows(y_slots, dest_flat).reshape(TOP_K, t_all, d // 2)

    y_p, y_s = _combine(yg, gate_t.T, sh, x1, ada8, chunk, r2(g_post_ffn), n1)
    return y_p.reshape(b1, s1, d), y_s.reshape(b2, s2, d)


def kernel(x_prompt, x_sample, c_prompt, c_sample, w_ada, b_ada, g_pre_mix, w_in, g_qa, w_uq, g_kva, w_ukv, g_out_a, g_out_b, w_o, g_post_mix, g_pre_ffn, router_w, router_bias, w_exp_gate, w_exp_up, w_exp_down, w_sh_gate, w_sh_up, w_sh_down, g_post_ffn):
    layer = [p[0] for p in (w_ada, b_ada, g_pre_mix, w_in, g_qa, w_uq, g_kva, w_ukv, g_out_a, g_out_b, w_o,
                            g_post_mix, g_pre_ffn, router_w, router_bias, w_exp_gate, w_exp_up, w_exp_down,
                            w_sh_gate, w_sh_up, w_sh_down, g_post_ffn)]
    return _layer(x_prompt, x_sample, c_prompt, c_sample, *layer)
```

```python
import functools
import math

import jax
import jax.numpy as jnp
from jax import lax
from jax.experimental import pallas as pl
from jax.experimental.pallas import tpu as pltpu
from jax.experimental.pallas import tpu_sc as plsc

F32 = jnp.float32
BF16 = jnp.bfloat16

D_MODEL = 1024
HEAD_DIM = 64
N_HEADS = 8
Q_LORA = 256
KV_LORA = 128
QK_NOPE = 64
QK_ROPE = 32
ROPE_BASE = 10000.0
DIL_PATTERNS = ((128, 1), (512, 4), (2048, 16))
DILATIONS = tuple(d for _, d in DIL_PATTERNS)
DIL_HALF = 64
assert all(w // (2 * d) == DIL_HALF for w, d in DIL_PATTERNS) and DILATIONS[0] == 1
N_EXPERTS = 256
TOP_K = 8
N_GROUPS = 8
GROUP_SIZE = N_EXPERTS // N_GROUPS
TOPK_GROUPS = 4
D_EXPERT = 256
ROUTE_SCALE = 2.5
EPS = 1e-6
NEG = -1e30

LANES = 128
HEAD_PAD = 128

TM_PRE = 512
TM_POST = 512
TM_COMB = 512
TQ_MLA = 256
TK_MLA = 512
TM_DEST = 2048
SC_WINDOW = 128
TQ_DIL = 128
EXPERT_BLOCK = 512
VMEM_LIMIT = 56 * 1024 * 1024


def _cparams(sem):
    return pltpu.CompilerParams(dimension_semantics=sem, vmem_limit_bytes=VMEM_LIMIT)


def _rms(x, g):
    return x * lax.rsqrt(jnp.mean(x * x, axis=-1, keepdims=True) + EPS) * g


def _sigmoid(x):
    return 1.0 / (1.0 + jnp.exp(-x))


def _dot(a, b):
    return jnp.dot(a, b, preferred_element_type=F32)


def _group_specs(tm, width, nt1):
    return [pl.BlockSpec((tm, width), lambda i: (jnp.minimum(i, nt1 - 1), 0)),
            pl.BlockSpec((tm, width), lambda i: (jnp.maximum(i - nt1, 0), 0))]


def _group_tile(first_ref, second_ref, nt1):
    return jnp.where(pl.program_id(0) < nt1, first_ref[...], second_ref[...])


def _pack_rows(x):
    n = x.shape[1] // 2
    bits = lax.bitcast_convert_type(x.astype(BF16).astype(F32), jnp.int32)
    return bits[:, :n] | lax.shift_right_logical(bits[:, n:], 16)


def _unpack_rows(u):
    hi = lax.bitcast_convert_type(u & jnp.int32(-65536), F32)
    lo = lax.bitcast_convert_type(lax.shift_left(u, 16), F32)
    return hi, lo


def _dot_nt(a, b, precision=None):
    return lax.dot_general(a, b, (((1,), (1,)), ((), ())), preferred_element_type=F32, precision=precision)


def _ada_kernel(c_ref, w_ref, b_ref, o_ref):
    c = c_ref[...]
    s = c * _sigmoid(c)
    o_ref[...] = jnp.dot(s, w_ref[...], preferred_element_type=F32, precision=lax.Precision.HIGHEST) + b_ref[...]


def _ada(c_all, w_ada, b_ada):
    nb, d = c_all.shape
    n_out = w_ada.shape[1]
    tn = 1024
    return pl.pallas_call(
        _ada_kernel,
        out_shape=jax.ShapeDtypeStruct((nb, n_out), F32),
        grid=(n_out // tn,),
        in_specs=[pl.BlockSpec((nb, d), lambda j: (0, 0)),
                  pl.BlockSpec((d, tn), lambda j: (0, j)),
                  pl.BlockSpec((1, tn), lambda j: (0, j))],
        out_specs=pl.BlockSpec((nb, tn), lambda j: (0, j)),
        compiler_params=_cparams(("arbitrary",)),
        name="ada",
    )(c_all, w_ada, b_ada.reshape(1, n_out))


def _premix_kernel(xp_ref, xs_ref, ada_ref, g_ref, win_ref, gqa_ref, wuq_ref, gkva_ref, wukv_ref, tab_ref,
                   qa_ref, ka_ref, va_ref, *rest, nt1):
    dil_refs, zs_ref = rest[:-1], rest[-1]
    x = _group_tile(xp_ref, xs_ref, nt1)
    sh1 = ada_ref[0, 0:1, :]
    sc1 = ada_ref[0, 1:2, :]
    h = _rms(x, g_ref[...]) * (1.0 + sc1) + sh1
    z = _dot(h.astype(BF16), win_ref[...])
    tab = tab_ref[...]
    cosq, sinq = tab[:, 0:128], tab[:, 128:256]
    cosk, sink = tab[:, 256:384], tab[:, 384:512]
    nh = N_HEADS * HEAD_PAD

    cq = _rms(z[:, 0:Q_LORA], gqa_ref[...]).astype(BF16)
    qq = _dot(cq, wuq_ref[...])
    for h_i in range(N_HEADS):
        lo = h_i * HEAD_PAD
        qa_ref[:, lo:lo + HEAD_PAD] = (qq[:, lo:lo + HEAD_PAD] * cosq
                                       + qq[:, nh + lo:nh + lo + HEAD_PAD] * sinq).astype(BF16)

    ckv = _rms(z[:, 256:384], gkva_ref[...]).astype(BF16)
    kk = _dot(ckv, wukv_ref[...])
    rr = z[:, 384:512] * cosk + z[:, 512:640] * sink
    for h_i in range(N_HEADS):
        lo = h_i * HEAD_PAD
        ka_ref[:, lo:lo + HEAD_PAD] = (kk[:, lo:lo + HEAD_PAD] + rr).astype(BF16)
    low = lax.broadcasted_iota(jnp.int32, (x.shape[0], LANES), 1) < HEAD_DIM
    for j in range(N_HEADS // 2):
        vpair = kk[:, nh + j * LANES:nh + (j + 1) * LANES]
        va_ref[:, (2 * j) * LANES:(2 * j + 1) * LANES] = jnp.where(low, vpair, 1.0).astype(BF16)
        va_ref[:, (2 * j + 1) * LANES:(2 * j + 2) * LANES] = jnp.where(low, 1.0, vpair).astype(BF16)

    tm = x.shape[0]
    wd = N_HEADS * HEAD_DIM
    n_slab = 3 * wd // LANES
    for c in range(n_slab):
        scale = HEAD_DIM ** -0.5 * math.log2(math.e) if c < wd // LANES else 1.0
        zs_ref[c] = z[:, 640 + c * LANES:640 + (c + 1) * LANES] * scale
    for dil, refs in zip(DILATIONS, (dil_refs[0:3], dil_refs[3:6], dil_refs[6:9])):
        n = tm // dil
        for r in range(dil):
            for c in range(n_slab):
                rows = zs_ref[c] if dil == 1 else zs_ref.at[c][pl.ds(r, n, stride=dil), :]
                col = r * wd + (c % (wd // LANES)) * LANES
                refs[c // (wd // LANES)][:, col:col + LANES] = rows.astype(BF16)


def _premix(x_p, x_s, ada8, chunk, g_pre, win2, g_qa, wuq2, g_kva, wukv2, tab, n1, s1, s2):
    t_all = x_p.shape[0] + x_s.shape[0]
    tm = TM_PRE
    nt1 = n1 // tm
    tb1, tb2 = s1 // tm, s2 // tm

    def tab_map(i):
        return (jnp.where(i < nt1, i % tb1, (i - nt1) % tb2), 0)

    row = lambda i: (i, 0)
    const = lambda i: (0, 0)
    wd = N_HEADS * HEAD_DIM
    outs = [jax.ShapeDtypeStruct((t_all, N_HEADS * HEAD_PAD), BF16)] * 3
    out_specs = [pl.BlockSpec((tm, N_HEADS * HEAD_PAD), row)] * 3
    for dil in DILATIONS:
        outs += [jax.ShapeDtypeStruct((t_all // dil, dil * wd), BF16)] * 3
        out_specs += [pl.BlockSpec((tm // dil, dil * wd), row)] * 3
    return pl.pallas_call(
        functools.partial(_premix_kernel, nt1=nt1),
        out_shape=outs,
        grid=(t_all // tm,),
        in_specs=_group_specs(tm, D_MODEL, nt1) + [
                  pl.BlockSpec((1, 6, D_MODEL), lambda i: (i * tm // chunk, 0, 0)),
                  pl.BlockSpec((1, D_MODEL), const),
                  pl.BlockSpec(win2.shape, const),
                  pl.BlockSpec((1, Q_LORA), const),
                  pl.BlockSpec(wuq2.shape, const),
                  pl.BlockSpec((1, KV_LORA), const),
                  pl.BlockSpec(wukv2.shape, const),
                  pl.BlockSpec((tm, 512), tab_map)],
        out_specs=out_specs,
        scratch_shapes=[pltpu.VMEM((3 * wd // LANES, tm, LANES), F32)],
        compiler_params=_cparams(("parallel",)),
        name="premix",
    )(x_p, x_s, ada8, g_pre, win2, g_qa, wuq2, g_kva, wukv2, tab)


def _mla_kernel(q_ref, k_ref, v_ref, o_ref, s_ref, m_ref, *, tk):
    tq, seq = s_ref.shape[1], s_ref.shape[2]
    lane = lax.broadcasted_iota(jnp.int32, (tq, LANES), 1)

    def scores(hh):
        s = _dot_nt(q_ref[:, hh * HEAD_PAD:(hh + 1) * HEAD_PAD], k_ref[:, hh * HEAD_PAD:(hh + 1) * HEAD_PAD])
        s_ref[hh] = s
        m_ref[hh] = jnp.max(s, axis=-1, keepdims=True)

    def values(hh):
        acc = jnp.zeros((tq, LANES), F32)
        m = m_ref[hh]
        for c0 in range(0, seq, tk):
            p = jnp.exp2(s_ref[hh, :, c0:c0 + tk] - m)
            acc = acc + _dot(p.astype(BF16), v_ref[c0:c0 + tk, hh * LANES:(hh + 1) * LANES])
        return acc / pltpu.roll(acc, HEAD_DIM, axis=1)

    scores(0)
    scores(1)
    o0 = values(0)
    o1 = values(1)
    o_ref[...] = jnp.where(lane < HEAD_DIM, o0, o1).astype(o_ref.dtype)


def _mla(qa, ka, va, row_off, batch, seq):
    tq = TQ_MLA
    nq = seq // tq
    qoff = row_off // tq
    soff = row_off // seq
    return pl.pallas_call(
        functools.partial(_mla_kernel, tk=TK_MLA),
        out_shape=jax.ShapeDtypeStruct((batch * seq, N_HEADS * HEAD_DIM), BF16),
        grid=(batch, N_HEADS // 2, nq),
        in_specs=[pl.BlockSpec((tq, 2 * HEAD_PAD), lambda b, j, qi: (qoff + b * nq + qi, j)),
                  pl.BlockSpec((seq, 2 * HEAD_PAD), lambda b, j, qi: (soff + b, j)),
                  pl.BlockSpec((seq, 2 * LANES), lambda b, j, qi: (soff + b, j))],
        out_specs=pl.BlockSpec((tq, LANES), lambda b, j, qi: (b * nq + qi, j)),
        scratch_shapes=[pltpu.VMEM((2, tq, seq), F32), pltpu.VMEM((2, tq, 1), F32)],
        compiler_params=_cparams(("parallel", "parallel", "parallel")),
        name="mla_attention",
    )(qa, ka, va)


def _dil_kernel(q_ref, kp_ref, kc_ref, kn_ref, vp_ref, vc_ref, vn_ref, bias_ref, o_ref, lse_ref, *,
                n1_rows, len1, len2):
    tq = q_ref.shape[0]
    half = DIL_HALF
    row0 = pl.program_id(0) * tq
    in1 = row0 < n1_rows
    seq_len = jnp.where(in1, len1, len2)
    q0 = jnp.where(in1, row0 % len1, (row0 - n1_rows) % len2)
    variant = (q0 == 0).astype(jnp.int32) + 2 * (q0 + tq == seq_len).astype(jnp.int32)

    kw = jnp.concatenate([kp_ref[tq - half:tq, :], kc_ref[...], kn_ref[0:half, :]], axis=0)
    vw = jnp.concatenate([vp_ref[tq - half:tq, :], vc_ref[...], vn_ref[0:half, :]], axis=0)
    low = lax.broadcasted_iota(jnp.int32, (tq, LANES), 1) < HEAD_DIM
    low_w = lax.broadcasted_iota(jnp.int32, (kw.shape[0], LANES), 1) < HEAD_DIM
    heads = range(N_HEADS)
    pair = lambda x, h: x[:, (h // 2) * LANES:(h // 2 + 1) * LANES]
    qs = [jnp.where(low, pair(q_ref, h), 0) if h % 2 == 0 else jnp.where(low, 0, pair(q_ref, h)) for h in heads]
    vs = [jnp.where(low_w, pair(vw, h), 1) if h % 2 == 0 else jnp.where(low_w, 1, pair(vw, h)) for h in heads]
    ss = [_dot_nt(qs[h].astype(BF16), pair(kw, h)) + bias_ref[variant, h] for h in heads]
    ms = [jnp.max(s, axis=-1, keepdims=True) for s in ss]
    ps = [jnp.exp2(s - m).astype(BF16) for s, m in zip(ss, ms)]
    accs = [_dot(p, v.astype(BF16)) for p, v in zip(ps, vs)]
    ls = [pltpu.roll(acc, HEAD_DIM, axis=1) for acc in accs]
    outs = [acc / l for acc, l in zip(accs, ls)]
    lses = [m * math.log(2.0) + jnp.log(l) for m, l in zip(ms, ls)]
    for j in range(N_HEADS // 2):
        o_ref[:, j * LANES:(j + 1) * LANES] = jnp.where(low, outs[2 * j], outs[2 * j + 1]).astype(o_ref.dtype)
        lse_ref[:, j * LANES:(j + 1) * LANES] = jnp.where(low, lses[2 * j], lses[2 * j + 1])


def _dil_bias(dil, tq):
    half = DIL_HALF
    col = jnp.arange(tq + 2 * half)[None, :]
    dist = jnp.abs(col - half - jnp.arange(tq)[:, None])
    slopes = 2.0 ** (-8.0 * jnp.arange(1, N_HEADS + 1, dtype=F32) / N_HEADS)
    alibi = -math.log2(math.e) * slopes[:, None, None] * (dil * dist).astype(F32)[None]
    band = dist <= half
    first, last = col >= half, col < tq + half
    masks = [band, band & first, band & last, band & first & last]
    return jnp.stack([jnp.where(mk[None], alibi, NEG) for mk in masks])


def _dilated(qv, kv, vv, dil, n1, s1, s2):
    rows = qv.shape[0]
    tq = TQ_DIL
    nt = rows // tq
    wk = N_HEADS * HEAD_DIM
    cur = lambda i, r: (i, r)
    prv = lambda i, r: (jnp.maximum(i - 1, 0), r)
    nxt = lambda i, r: (jnp.minimum(i + 1, nt - 1), r)
    kspecs = [pl.BlockSpec((tq, wk), f) for f in (prv, cur, nxt)]
    bias = _dil_bias(dil, tq)
    return pl.pallas_call(
        functools.partial(_dil_kernel, n1_rows=n1 // dil, len1=s1 // dil, len2=s2 // dil),
        out_shape=[jax.ShapeDtypeStruct((rows, dil * wk), BF16), jax.ShapeDtypeStruct((rows, dil * wk), F32)],
        grid=(nt, dil),
        in_specs=[pl.BlockSpec((tq, wk), cur)] + kspecs + kspecs
                 + [pl.BlockSpec(bias.shape, lambda i, r: (0, 0, 0, 0))],
        out_specs=[pl.BlockSpec((tq, wk), cur), pl.BlockSpec((tq, wk), cur)],
        compiler_params=_cparams(("parallel", "parallel")),
        name="dilated_attention_d%d" % dil,
    )(qv, kv, kv, kv, vv, vv, vv, bias)


def _postmix_kernel(xp_ref, xs_ref, oap_ref, oas_ref, o1_ref, o2_ref, o3_ref, l1_ref, l2_ref, l3_ref, ada_ref, goa_ref, gob_ref,
                    wo_ref, gpm_ref, gpf_ref, rwt_ref, rb_ref, wsg_ref, wsu_ref, wsd_ref,
                    x1_ref, h2_ref, sh_ref, idx_ref, gate_ref, rank_ref, cnt_ref, carry_ref, nat_ref, *, nt1):
    tm = xp_ref.shape[0]

    @pl.when(pl.program_id(0) == 0)
    def _():
        carry_ref[...] = jnp.zeros_like(carry_ref)

    gt1, sh2, sc2 = ada_ref[0, 2:3, :], ada_ref[0, 3:4, :], ada_ref[0, 4:5, :]

    wd = N_HEADS * HEAD_DIM
    n_slab = wd // LANES

    def row_order(src_ref, dil, base):
        if dil == 1:
            return src_ref[...].astype(F32)
        n = tm // dil
        for r in range(dil):
            for c in range(n_slab):
                col = r * wd + c * LANES
                nat_ref.at[base + c][pl.ds(r, n, stride=dil), :] = src_ref[:, col:col + LANES].astype(F32)
        return jnp.concatenate([nat_ref[base + c] for c in range(n_slab)], axis=1)

    o_pat = [row_order(ref, dil, (2 * p) * n_slab) for p, (ref, dil) in enumerate(zip((o1_ref, o2_ref, o3_ref), DILATIONS))]
    la, lb, lc = [row_order(ref, dil, (2 * p + 1) * n_slab)
                  for p, (ref, dil) in enumerate(zip((l1_ref, l2_ref, l3_ref), DILATIONS))]

    mx = jnp.maximum(jnp.maximum(la, lb), lc)
    ea, eb, ec = jnp.exp(la - mx), jnp.exp(lb - mx), jnp.exp(lc - mx)
    ob = (ea * o_pat[0] + eb * o_pat[1] + ec * o_pat[2]) / (ea + eb + ec)

    na = _rms(_group_tile(oap_ref, oas_ref, nt1).astype(F32), goa_ref[...]).astype(BF16)
    nb = _rms(ob, gob_ref[...]).astype(BF16)
    half_w = N_HEADS * HEAD_DIM
    o = _dot(na, wo_ref[0:half_w, :]) + _dot(nb, wo_ref[half_w:2 * half_w, :])
    x1 = _group_tile(xp_ref, xs_ref, nt1) + gt1 * _rms(o, gpm_ref[...])
    x1_ref[...] = x1
    h2 = _rms(x1, gpf_ref[...]) * (1.0 + sc2) + sh2
    h2b = h2.astype(BF16)
    h2_ref[...] = _pack_rows(h2)

    g = _dot(h2b, wsg_ref[...])
    u = _dot(h2b, wsu_ref[...])
    sh_ref[...] = _dot((g * _sigmoid(g) * u).astype(BF16), wsd_ref[...])

    scores = _sigmoid(_dot_nt(rwt_ref[...], h2, precision=lax.Precision.HIGHEST))
    biased = scores + rb_ref[...]
    ninf = -jnp.inf
    row = lax.broadcasted_iota(jnp.int32, (N_EXPERTS, tm), 0)
    rwg = lax.broadcasted_iota(jnp.int32, (GROUP_SIZE, tm), 0)
    gsc = []
    for gi in range(N_GROUPS):
        blk = biased[gi * GROUP_SIZE:(gi + 1) * GROUP_SIZE]
        m1 = jnp.max(blk, axis=0, keepdims=True)
        i1 = jnp.min(jnp.where(blk == m1, rwg, N_EXPERTS), axis=0, keepdims=True)
        m2 = jnp.max(jnp.where(rwg == i1, ninf, blk), axis=0, keepdims=True)
        gsc.append(m1 + m2)
    gsel = [jnp.zeros((1, tm), F32) for _ in range(N_GROUPS)]
    for _ in range(TOPK_GROUPS):
        m = functools.reduce(jnp.maximum, gsc)
        free = jnp.ones((1, tm), F32)
        for gi in range(N_GROUPS):
            hit = jnp.where(gsc[gi] == m, free, 0.0)
            free = free - hit
            gsel[gi] = gsel[gi] + hit
            gsc[gi] = jnp.where(hit > 0.0, ninf, gsc[gi])
    masked = jnp.concatenate(
        [jnp.where(gsel[gi] > 0.0, biased[gi * GROUP_SIZE:(gi + 1) * GROUP_SIZE], ninf)
         for gi in range(N_GROUPS)], axis=0)
    cur = masked
    idxs, gts = [], []
    for _ in range(TOP_K):
        m = jnp.max(cur, axis=0, keepdims=True)
        ik = jnp.min(jnp.where(cur == m, row, N_EXPERTS), axis=0, keepdims=True)
        hit = row == ik
        idxs.append(ik)
        gts.append(jnp.sum(jnp.where(hit, scores, 0.0), axis=0, keepdims=True))
        cur = jnp.where(hit, ninf, cur)
    idx = jnp.concatenate(idxs, axis=0)
    gates = jnp.concatenate(gts, axis=0)
    idx_ref[...] = idx
    gate_ref[...] = gates / jnp.sum(gates, axis=0, keepdims=True) * ROUTE_SCALE

    sel = jnp.where(cur != masked, 1.0, 0.0)
    tr = lax.broadcasted_iota(jnp.int32, (tm, tm), 0)
    tc = lax.broadcasted_iota(jnp.int32, (tm, tm), 1)
    before = jnp.where(tr < tc, 1.0, 0.0).astype(BF16)
    pos = _dot(sel.astype(BF16), before) + carry_ref[...]
    rank_ref[...] = jnp.concatenate(
        [jnp.sum(jnp.where(row == idxs[k], pos, 0.0), axis=0, keepdims=True) for k in range(TOP_K)],
        axis=0).astype(jnp.int32)
    carry = carry_ref[...] + jnp.sum(sel, axis=1, keepdims=True)
    carry_ref[...] = carry
    cnt_ref[...] = jnp.broadcast_to(carry, cnt_ref.shape)


def _postmix(x_p, x_s, oa_p, oa_s, os_, ls_, ada8, chunk, g_oa, g_ob, wo, g_pm, g_pf, rwt, rb, wsg, wsu, wsd):
    t_all = x_p.shape[0] + x_s.shape[0]
    tm = TM_POST
    nt1 = x_p.shape[0] // tm
    row = lambda i: (i, 0)
    col = lambda i: (0, i)
    const = lambda i: (0, 0)
    hw = N_HEADS * HEAD_DIM
    pat_specs = [pl.BlockSpec((tm // dil, dil * hw), row) for dil in DILATIONS]
    in_specs = (_group_specs(tm, D_MODEL, nt1) + _group_specs(tm, hw, nt1)
                + pat_specs + pat_specs
                + [pl.BlockSpec((1, 6, D_MODEL), lambda i: (i * tm // chunk, 0, 0)),
                   pl.BlockSpec((1, hw), const), pl.BlockSpec((1, hw), const),
                   pl.BlockSpec(wo.shape, const),
                   pl.BlockSpec((1, D_MODEL), const), pl.BlockSpec((1, D_MODEL), const),
                   pl.BlockSpec(rwt.shape, const), pl.BlockSpec((N_EXPERTS, 1), const),
                   pl.BlockSpec(wsg.shape, const), pl.BlockSpec(wsu.shape, const), pl.BlockSpec(wsd.shape, const)])
    out_shape = [jax.ShapeDtypeStruct((t_all, D_MODEL), F32),
                 jax.ShapeDtypeStruct((t_all, D_MODEL // 2), jnp.int32),
                 jax.ShapeDtypeStruct((t_all, D_MODEL), F32),
                 jax.ShapeDtypeStruct((TOP_K, t_all), jnp.int32),
                 jax.ShapeDtypeStruct((TOP_K, t_all), F32),
                 jax.ShapeDtypeStruct((TOP_K, t_all), jnp.int32),
                 jax.ShapeDtypeStruct((N_EXPERTS, LANES), F32)]
    out_specs = [pl.BlockSpec((tm, D_MODEL), row), pl.BlockSpec((tm, D_MODEL // 2), row), pl.BlockSpec((tm, D_MODEL), row),
                 pl.BlockSpec((TOP_K, tm), col), pl.BlockSpec((TOP_K, tm), col), pl.BlockSpec((TOP_K, tm), col),
                 pl.BlockSpec((N_EXPERTS, LANES), const)]
    return pl.pallas_call(
        functools.partial(_postmix_kernel, nt1=nt1),
        out_shape=out_shape,
        grid=(t_all // tm,),
        in_specs=in_specs,
        out_specs=out_specs,
        scratch_shapes=[pltpu.VMEM((N_EXPERTS, 1), F32),
                        pltpu.VMEM((2 * len(DILATIONS) * hw // LANES, tm, LANES), F32)],
        compiler_params=_cparams(("arbitrary",)),
        name="postmix_router",
    )(x_p, x_s, oa_p, oa_s, *os_, *ls_, ada8, g_oa, g_ob, wo, g_pm, g_pf, rwt, rb, wsg, wsu, wsd)


def _dest_kernel(idx_ref, rank_ref, start_ref, dest_ref):
    tm = idx_ref.shape[1]
    row = lax.broadcasted_iota(jnp.int32, (N_EXPERTS, tm), 0)
    start = start_ref[...]
    base = [jnp.sum(jnp.where(row == idx_ref[k:k + 1, :], start, 0), axis=0, keepdims=True) for k in range(TOP_K)]
    dest_ref[...] = jnp.concatenate(base, axis=0) + rank_ref[...]


def _dest(idx_t, rank_t, pad_start):
    t_all = idx_t.shape[1]
    tm = TM_DEST
    col = lambda i: (0, i)
    return pl.pallas_call(
        _dest_kernel,
        out_shape=jax.ShapeDtypeStruct((TOP_K, t_all), jnp.int32),
        grid=(t_all // tm,),
        in_specs=[pl.BlockSpec((TOP_K, tm), col), pl.BlockSpec((TOP_K, tm), col),
                  pl.BlockSpec((N_EXPERTS, 1), lambda i: (0, 0))],
        out_specs=pl.BlockSpec((TOP_K, tm), col),
        compiler_params=_cparams(("parallel",)),
        name="slot_index",
    )(idx_t, rank_t, pad_start.reshape(N_EXPERTS, 1))


def _sc_mesh():
    return plsc.VectorSubcoreMesh(core_axis_name="core", subcore_axis_name="subcore")


def _dispatch_rows(rows, dest, n_slots):
    t_all, width = rows.shape
    win = SC_WINDOW
    info = plsc.get_sparse_core_info()
    n_workers = info.num_cores * info.num_subcores
    per_worker = t_all // n_workers
    assert per_worker % win == 0

    @functools.partial(
        pl.kernel, out_type=jax.ShapeDtypeStruct((n_slots, width), rows.dtype), mesh=_sc_mesh(),
        scratch_types=[pltpu.VMEM((TOP_K, win), jnp.int32), pltpu.VMEM((win, width), rows.dtype),
                       pltpu.SemaphoreType.DMA])
    def scatter_kernel(x_hbm, i_hbm, o_hbm, idx_v, rows_v, sem):
        base = (lax.axis_index("subcore") * info.num_cores + lax.axis_index("core")) * per_worker

        @pl.loop(0, per_worker // win)
        def _(w):
            off = base + w * win
            for k in range(TOP_K):
                pltpu.sync_copy(i_hbm.at[pl.ds(k * t_all + off, win)], idx_v.at[k])
            pltpu.sync_copy(x_hbm.at[pl.ds(off, win)], rows_v)
            copies = [pltpu.async_copy(rows_v, o_hbm.at[idx_v.at[k]], sem) for k in range(TOP_K)]
            for c in copies:
                c.wait()

    return scatter_kernel(rows, dest)


def _gather_rows(slots, index):
    n = index.shape[0]
    width = slots.shape[1]
    win = SC_WINDOW
    info = plsc.get_sparse_core_info()
    n_workers = info.num_cores * info.num_subcores
    per_worker = n // n_workers
    assert per_worker % win == 0

    @functools.partial(
        pl.kernel, out_type=jax.ShapeDtypeStruct((n, width), slots.dtype), mesh=_sc_mesh(),
        scratch_types=[pltpu.VMEM((win,), jnp.int32), pltpu.VMEM((win, width), slots.dtype),
                       pltpu.SemaphoreType.DMA])
    def gather_kernel(y_hbm, i_hbm, o_hbm, idx_v, rows_v, sem):
        base = (lax.axis_index("subcore") * info.num_cores + lax.axis_index("core")) * per_worker

        @pl.loop(0, per_worker // win)
        def _(w):
            off = base + w * win
            pltpu.sync_copy(i_hbm.at[pl.ds(off, win)], idx_v)
            pltpu.async_copy(y_hbm.at[idx_v], rows_v, sem).wait()
            pltpu.sync_copy(rows_v, o_hbm.at[pl.ds(off, win)])

    return gather_kernel(slots, index)


def _expert_kernel(be_ref, bv_ref, nx_ref, nu_ref, x_ref, wg_hbm, wu_hbm, wd_hbm, y_ref,
                   wg_f, wu_f, wd_f, wgu_s, wd_s, sems):
    i = pl.program_id(0)
    used = i < nu_ref[0]
    expert = be_ref[i]
    new_expert = (i == 0) | (expert != be_ref[jnp.maximum(i - 1, 0)])

    def weight_copies(e):
        return (pltpu.make_async_copy(wg_hbm.at[e], wg_f, sems.at[0]),
                pltpu.make_async_copy(wu_hbm.at[e], wu_f, sems.at[1]),
                pltpu.make_async_copy(wd_hbm.at[e], wd_f, sems.at[2]))

    @pl.when(used & (i == 0))
    def _():
        for c in weight_copies(expert):
            c.start()

    @pl.when(used & new_expert)
    def _():
        for c in weight_copies(expert):
            c.wait()
        wgu_s[:, 0:D_EXPERT] = wg_f[...].astype(BF16)
        wgu_s[:, D_EXPERT:2 * D_EXPERT] = wu_f[...].astype(BF16)
        wd_s[...] = wd_f[...].astype(BF16)
        nxt = nx_ref[i]

        @pl.when(nxt >= 0)
        def _():
            for c in weight_copies(nxt):
                c.start()

    @pl.when(used)
    def _():
        live = lax.broadcasted_iota(jnp.int32, x_ref.shape, 0) < bv_ref[i]
        xa, xb = _unpack_rows(jnp.where(live, x_ref[...], 0))
        half = D_MODEL // 2
        gu = _dot(xa.astype(BF16), wgu_s[0:half, :]) + _dot(xb.astype(BF16), wgu_s[half:D_MODEL, :])
        g, u = gu[:, 0:D_EXPERT], gu[:, D_EXPERT:2 * D_EXPERT]
        y_ref[...] = _pack_rows(_dot((g * _sigmoid(g) * u).astype(BF16), wd_s[...]))

    @pl.when(jnp.logical_not(used))
    def _():
        y_ref[...] = jnp.zeros_like(y_ref)


def _experts(block_expert, block_valid, block_next, n_used, x_slots, wg, wu, wd):
    p_rows = x_slots.shape[0]
    blk = EXPERT_BLOCK
    hbm = pl.BlockSpec(memory_space=pl.ANY)
    grid_spec = pltpu.PrefetchScalarGridSpec(
        num_scalar_prefetch=4,
        grid=(p_rows // blk,),
        in_specs=[pl.BlockSpec((blk, D_MODEL // 2),
                               lambda i, be, bv, nx, nu: (jnp.maximum(jnp.minimum(i, nu[0] - 1), 0), 0)),
                  hbm, hbm, hbm],
        out_specs=pl.BlockSpec((blk, D_MODEL // 2), lambda i, be, bv, nx, nu: (i, 0)),
        scratch_shapes=[pltpu.VMEM((D_MODEL, D_EXPERT), F32), pltpu.VMEM((D_MODEL, D_EXPERT), F32),
                        pltpu.VMEM((D_EXPERT, D_MODEL), F32),
                        pltpu.VMEM((D_MODEL, 2 * D_EXPERT), BF16), pltpu.VMEM((D_EXPERT, D_MODEL), BF16),
                        pltpu.SemaphoreType.DMA((3,))])
    return pl.pallas_call(
        _expert_kernel,
        out_shape=jax.ShapeDtypeStruct((p_rows, D_MODEL // 2), jnp.int32),
        grid_spec=grid_spec,
        compiler_params=_cparams(("arbitrary",)),
        name="expert_ffn",
    )(block_expert, block_valid, block_next, n_used, x_slots, wg, wu, wd)


def _combine_kernel(yg_ref, gate_ref, sh_ref, x1_ref, ada_ref, g_ref, yp_ref, ys_ref, *, nt1):
    gates = gate_ref[...]
    half = D_MODEL // 2
    acc_a = sh_ref[:, 0:half]
    acc_b = sh_ref[:, half:D_MODEL]
    for k in range(TOP_K):
        ya, yb = _unpack_rows(yg_ref[k])
        acc_a = acc_a + gates[:, k:k + 1] * ya
        acc_b = acc_b + gates[:, k:k + 1] * yb
    gt2 = ada_ref[0, 5:6, :]
    y = x1_ref[...] + gt2 * _rms(jnp.concatenate([acc_a, acc_b], axis=1), g_ref[...])
    first = pl.program_id(0) < nt1

    @pl.when(first)
    def _():
        yp_ref[...] = y

    @pl.when(jnp.logical_not(first))
    def _():
        ys_ref[...] = y


def _combine(yg, gates_t, sh, x1, ada8, chunk, g_post, n1):
    t_all = x1.shape[0]
    tm = TM_COMB
    nt1 = n1 // tm
    row = lambda i: (i, 0)
    return pl.pallas_call(
        functools.partial(_combine_kernel, nt1=nt1),
        out_shape=[jax.ShapeDtypeStruct((n1, D_MODEL), F32), jax.ShapeDtypeStruct((t_all - n1, D_MODEL), F32)],
        grid=(t_all // tm,),
        in_specs=[pl.BlockSpec((TOP_K, tm, D_MODEL // 2), lambda i: (0, i, 0)),
                  pl.BlockSpec((tm, TOP_K), row),
                  pl.BlockSpec((tm, D_MODEL), row),
                  pl.BlockSpec((tm, D_MODEL), row),
                  pl.BlockSpec((1, 6, D_MODEL), lambda i: (i * tm // chunk, 0, 0)),
                  pl.BlockSpec((1, D_MODEL), lambda i: (0, 0))],
        out_specs=_group_specs(tm, D_MODEL, nt1),
        compiler_params=_cparams(("arbitrary",)),
        name="moe_combine",
    )(yg, gates_t, sh, x1, ada8, g_post)


def _rope_partner(w):
    half = QK_ROPE // 2
    return jnp.concatenate([-w[..., half:], w[..., :half]], axis=-1)


def _prep_weights(w_in, w_uq, w_ukv):
    d = w_in.shape[0]
    zeros = lambda r, c: jnp.zeros((r, c), F32)
    kr = w_in[:, 384:416]
    r_main = jnp.concatenate([zeros(d, QK_NOPE), kr, zeros(d, HEAD_PAD - QK_NOPE - QK_ROPE)], axis=1)
    r_part = jnp.concatenate([zeros(d, QK_NOPE), _rope_partner(kr), zeros(d, HEAD_PAD - QK_NOPE - QK_ROPE)], axis=1)
    win2 = jnp.concatenate([w_in[:, :384], r_main, r_part, w_in[:, 416:]], axis=1).astype(BF16)

    wq = w_uq.reshape(Q_LORA, N_HEADS, QK_NOPE + QK_ROPE)
    zq = jnp.zeros((Q_LORA, N_HEADS, HEAD_PAD - QK_NOPE - QK_ROPE), F32)
    q_main = jnp.concatenate([wq, zq], axis=-1).reshape(Q_LORA, N_HEADS * HEAD_PAD)
    q_part = jnp.concatenate([jnp.zeros((Q_LORA, N_HEADS, QK_NOPE), F32), _rope_partner(wq[..., QK_NOPE:]), zq],
                             axis=-1).reshape(Q_LORA, N_HEADS * HEAD_PAD)
    wuq2 = jnp.concatenate([q_main, q_part], axis=1).astype(BF16)

    wkv = w_ukv.reshape(KV_LORA, N_HEADS, QK_NOPE + HEAD_DIM)
    k_pad = jnp.concatenate([wkv[..., :QK_NOPE], jnp.zeros((KV_LORA, N_HEADS, HEAD_PAD - QK_NOPE), F32)],
                            axis=-1).reshape(KV_LORA, N_HEADS * HEAD_PAD)
    v_cols = wkv[..., QK_NOPE:].reshape(KV_LORA, N_HEADS * HEAD_DIM)
    wukv2 = jnp.concatenate([k_pad, v_cols], axis=1).astype(BF16)
    return win2, wuq2, wukv2


def _rope_table(s_max):
    half = QK_ROPE // 2
    inv = ROPE_BASE ** (-jnp.arange(half, dtype=F32) / half)
    ang = jnp.arange(s_max, dtype=F32)[:, None] * inv[None, :]
    cos = jnp.concatenate([jnp.cos(ang), jnp.cos(ang)], axis=1)
    sin = jnp.concatenate([jnp.sin(ang), jnp.sin(ang)], axis=1)
    scale = (QK_NOPE + QK_ROPE) ** -0.5 * math.log2(math.e)
    pad = jnp.zeros((s_max, HEAD_PAD - QK_NOPE - QK_ROPE), F32)
    zn = jnp.zeros((s_max, QK_NOPE), F32)
    cosq = jnp.concatenate([jnp.full((s_max, QK_NOPE), scale, F32), cos * scale, pad], axis=1)
    sinq = jnp.concatenate([zn, sin * scale, pad], axis=1)
    cosk = jnp.concatenate([zn, cos, pad], axis=1)
    sink = jnp.concatenate([zn, sin, pad], axis=1)
    return jnp.concatenate([cosq, sinq, cosk, sink], axis=1)


def _layer(x_prompt, x_sample, c_prompt, c_sample, w_ada, b_ada, g_pre_mix, w_in, g_qa, w_uq, g_kva, w_ukv,
           g_out_a, g_out_b, w_o, g_post_mix, g_pre_ffn, router_w, router_bias, w_exp_gate, w_exp_up, w_exp_down,
           w_sh_gate, w_sh_up, w_sh_down, g_post_ffn):
    b1, s1, d = x_prompt.shape
    b2, s2, _ = x_sample.shape
    n1, n2 = b1 * s1, b2 * s2
    t_all = n1 + n2
    chunk = math.gcd(s1, s2)
    assert n1 % s2 == 0 and chunk % max(TM_PRE, TM_POST, TM_COMB) == 0

    x_p, x_s = x_prompt.reshape(n1, d), x_sample.reshape(n2, d)
    c_all = jnp.concatenate([c_prompt, c_sample, jnp.zeros((8 - b1 - b2, d), F32)], axis=0)
    ada = _ada(c_all, w_ada, b_ada)
    chunk_batch = jnp.concatenate([jnp.repeat(jnp.arange(b1), s1 // chunk),
                                   b1 + jnp.repeat(jnp.arange(b2), s2 // chunk)])
    ada8 = ada.reshape(8, 6, d)[chunk_batch]

    win2, wuq2, wukv2 = _prep_weights(w_in, w_uq, w_ukv)
    tab = _rope_table(max(s1, s2))
    r2 = lambda g: g.reshape(1, -1)
    qa, ka, va, *dil_qkv = _premix(x_p, x_s, ada8, chunk, r2(g_pre_mix), win2, r2(g_qa), wuq2, r2(g_kva), wukv2, tab,
                                   n1, s1, s2)

    oa_p, oa_s = _mla(qa, ka, va, 0, b1, s1), _mla(qa, ka, va, n1, b2, s2)
    os_, ls_ = [], []
    for p, dil in enumerate(DILATIONS):
        o, lse = _dilated(*dil_qkv[3 * p:3 * p + 3], dil, n1, s1, s2)
        os_.append(o)
        ls_.append(lse)

    x1, h2p, sh, idx_t, gate_t, rank_t, cnt = _postmix(
        x_p, x_s, oa_p, oa_s, os_, ls_, ada8, chunk, r2(g_out_a), r2(g_out_b), w_o.astype(BF16), r2(g_post_mix), r2(g_pre_ffn),
        router_w.T, router_bias.reshape(N_EXPERTS, 1), w_sh_gate.astype(BF16), w_sh_up.astype(BF16),
        w_sh_down.astype(BF16))

    blk = EXPERT_BLOCK
    n_assign = t_all * TOP_K
    n_blocks = -(-(n_assign + N_EXPERTS * (blk - 1)) // blk)
    counts = cnt[:, 0].astype(jnp.int32)
    padded = (counts + blk - 1) // blk * blk
    pad_end = jnp.cumsum(padded)
    pad_start = pad_end - padded
    dest = _dest(idx_t, rank_t, pad_start)
    block_first = jnp.arange(n_blocks, dtype=jnp.int32) * blk
    block_expert = jnp.minimum(jnp.sum((pad_end[None, :] <= block_first[:, None]).astype(jnp.int32), axis=1),
                               N_EXPERTS - 1)
    n_used = (pad_end[-1] // blk).astype(jnp.int32).reshape(1)
    block_valid = jnp.clip(counts[block_expert] - (block_first - pad_start[block_expert]), 0, blk)
    eid = jnp.arange(N_EXPERTS, dtype=jnp.int32)
    used_from = lax.cummin(jnp.where(counts > 0, eid, N_EXPERTS), axis=0, reverse=True)
    next_used = jnp.concatenate([used_from[1:], jnp.full((1,), N_EXPERTS, jnp.int32)])
    block_next = jnp.where(next_used < N_EXPERTS, next_used, -1)[block_expert]
    dest_flat = dest.reshape(n_assign)
    x_slots = _dispatch_rows(h2p, dest_flat, n_blocks * blk)
    y_slots = _experts(block_expert, block_valid, block_next, n_used, x_slots, w_exp_gate, w_exp_up, w_exp_down)
    yg = _gather_rows(y_slots, dest_flat).reshape(TOP_K, t_all, d // 2)

    y_p, y_s = _combine(yg, gate_t.T, sh, x1, ada8, chunk, r2(g_post_ffn), n1)
    return y_p.reshape(b1, s1, d), y_s.reshape(b2, s2, d)


def kernel(x_prompt, x_sample, c_prompt, c_sample, w_ada, b_ada, g_pre_mix, w_in, g_qa, w_uq, g_kva, w_ukv, g_out_a, g_out_b, w_o, g_post_mix, g_pre_ffn, router_w, router_bias, w_exp_gate, w_exp_up, w_exp_down, w_sh_gate, w_sh_up, w_sh_down, g_post_ffn):
    layer = [p[0] for p in (w_ada, b_ada, g_pre_mix, w_in, g_qa, w_uq, g_kva, w_ukv, g_out_a, g_out_b, w_o,
                            g_post_mix, g_pre_ffn, router_w, router_bias, w_exp_gate, w_exp_up, w_exp_down,
                            w_sh_gate, w_sh_up, w_sh_down, g_post_ffn)]
    return _layer(x_prompt, x_sample, c_prompt, c_sample, *layer)
```

```python
import functools
import math

import jax
import jax.numpy as jnp
from jax import lax
from jax.experimental import pallas as pl
from jax.experimental.pallas import tpu as pltpu
from jax.experimental.pallas import tpu_sc as plsc

F32 = jnp.float32
BF16 = jnp.bfloat16

D_MODEL = 1024
HEAD_DIM = 64
N_HEADS = 8
Q_LORA = 256
KV_LORA = 128
QK_NOPE = 64
QK_ROPE = 32
ROPE_BASE = 10000.0
DIL_PATTERNS = ((128, 1), (512, 4), (2048, 16))
DILATIONS = tuple(d for _, d in DIL_PATTERNS)
DIL_HALF = 64
assert all(w // (2 * d) == DIL_HALF for w, d in DIL_PATTERNS) and DILATIONS[0] == 1
N_EXPERTS = 256
TOP_K = 8
N_GROUPS = 8
GROUP_SIZE = N_EXPERTS // N_GROUPS
TOPK_GROUPS = 4
D_EXPERT = 256
ROUTE_SCALE = 2.5
EPS = 1e-6
NEG = -1e30

LANES = 128
HEAD_PAD = 128

TM_PRE = 512
TM_POST = 512
TM_COMB = 512
TQ_MLA = 256
TK_MLA = 512
TM_DEST = 2048
SC_WINDOW = 128
TQ_DIL = 128
EXPERT_BLOCK = 512
RING = 3
VMEM_LIMIT = 56 * 1024 * 1024


def _cparams(sem):
    return pltpu.CompilerParams(dimension_semantics=sem, vmem_limit_bytes=VMEM_LIMIT)


def _rms(x, g):
    return x * lax.rsqrt(jnp.mean(x * x, axis=-1, keepdims=True) + EPS) * g


def _sigmoid(x):
    return 1.0 / (1.0 + jnp.exp(-x))


def _dot(a, b):
    return jnp.dot(a, b, preferred_element_type=F32)


def _group_specs(tm, width, nt1):
    return [pl.BlockSpec((tm, width), lambda i: (jnp.minimum(i, nt1 - 1), 0)),
            pl.BlockSpec((tm, width), lambda i: (jnp.maximum(i - nt1, 0), 0))]


def _group_tile(first_ref, second_ref, nt1):
    return jnp.where(pl.program_id(0) < nt1, first_ref[...], second_ref[...])


def _pack_rows(x):
    n = x.shape[1] // 2
    bits = lax.bitcast_convert_type(x.astype(BF16).astype(F32), jnp.int32)
    return bits[:, :n] | lax.shift_right_logical(bits[:, n:], 16)


def _unpack_rows(u):
    hi = lax.bitcast_convert_type(u & jnp.int32(-65536), F32)
    lo = lax.bitcast_convert_type(lax.shift_left(u, 16), F32)
    return hi, lo


def _dot_nt(a, b, precision=None):
    return lax.dot_general(a, b, (((1,), (1,)), ((), ())), preferred_element_type=F32, precision=precision)


def _ada_kernel(c_ref, w_ref, b_ref, o_ref):
    c = c_ref[...]
    s = c * _sigmoid(c)
    o_ref[...] = jnp.dot(s, w_ref[...], preferred_element_type=F32, precision=lax.Precision.HIGHEST) + b_ref[...]


def _ada(c_all, w_ada, b_ada):
    nb, d = c_all.shape
    n_out = w_ada.shape[1]
    tn = 1024
    return pl.pallas_call(
        _ada_kernel,
        out_shape=jax.ShapeDtypeStruct((nb, n_out), F32),
        grid=(n_out // tn,),
        in_specs=[pl.BlockSpec((nb, d), lambda j: (0, 0)),
                  pl.BlockSpec((d, tn), lambda j: (0, j)),
                  pl.BlockSpec((1, tn), lambda j: (0, j))],
        out_specs=pl.BlockSpec((nb, tn), lambda j: (0, j)),
        compiler_params=_cparams(("arbitrary",)),
        name="ada",
    )(c_all, w_ada, b_ada.reshape(1, n_out))


def _premix_kernel(xp_ref, xs_ref, ada_ref, g_ref, win_ref, gqa_ref, wuq_ref, gkva_ref, wukv_ref, tab_ref,
                   qa_ref, ka_ref, va_ref, *rest, nt1):
    dil_refs, zs_ref = rest[:-1], rest[-1]
    x = _group_tile(xp_ref, xs_ref, nt1)
    sh1 = ada_ref[0, 0:1, :]
    sc1 = ada_ref[0, 1:2, :]
    h = _rms(x, g_ref[...]) * (1.0 + sc1) + sh1
    z = _dot(h.astype(BF16), win_ref[...])
    tab = tab_ref[...]
    cosq, sinq = tab[:, 0:128], tab[:, 128:256]
    cosk, sink = tab[:, 256:384], tab[:, 384:512]
    nh = N_HEADS * HEAD_PAD

    cq = _rms(z[:, 0:Q_LORA], gqa_ref[...]).astype(BF16)
    qq = _dot(cq, wuq_ref[...])
    for h_i in range(N_HEADS):
        lo = h_i * HEAD_PAD
        qa_ref[:, lo:lo + HEAD_PAD] = (qq[:, lo:lo + HEAD_PAD] * cosq
                                       + qq[:, nh + lo:nh + lo + HEAD_PAD] * sinq).astype(BF16)

    ckv = _rms(z[:, 256:384], gkva_ref[...]).astype(BF16)
    kk = _dot(ckv, wukv_ref[...])
    rr = z[:, 384:512] * cosk + z[:, 512:640] * sink
    for h_i in range(N_HEADS):
        lo = h_i * HEAD_PAD
        ka_ref[:, lo:lo + HEAD_PAD] = (kk[:, lo:lo + HEAD_PAD] + rr).astype(BF16)
    low = lax.broadcasted_iota(jnp.int32, (x.shape[0], LANES), 1) < HEAD_DIM
    for j in range(N_HEADS // 2):
        vpair = kk[:, nh + j * LANES:nh + (j + 1) * LANES]
        va_ref[:, (2 * j) * LANES:(2 * j + 1) * LANES] = jnp.where(low, vpair, 1.0).astype(BF16)
        va_ref[:, (2 * j + 1) * LANES:(2 * j + 2) * LANES] = jnp.where(low, 1.0, vpair).astype(BF16)

    tm = x.shape[0]
    wd = N_HEADS * HEAD_DIM
    n_slab = 3 * wd // LANES
    for c in range(n_slab):
        scale = HEAD_DIM ** -0.5 * math.log2(math.e) if c < wd // LANES else 1.0
        zs_ref[c] = z[:, 640 + c * LANES:640 + (c + 1) * LANES] * scale
    for dil, refs in zip(DILATIONS, (dil_refs[0:3], dil_refs[3:6], dil_refs[6:9])):
        n = tm // dil
        for r in range(dil):
            for c in range(n_slab):
                rows = zs_ref[c] if dil == 1 else zs_ref.at[c][pl.ds(r, n, stride=dil), :]
                col = r * wd + (c % (wd // LANES)) * LANES
                refs[c // (wd // LANES)][:, col:col + LANES] = rows.astype(BF16)


def _premix(x_p, x_s, ada8, chunk, g_pre, win2, g_qa, wuq2, g_kva, wukv2, tab, n1, s1, s2):
    t_all = x_p.shape[0] + x_s.shape[0]
    tm = TM_PRE
    nt1 = n1 // tm
    tb1, tb2 = s1 // tm, s2 // tm

    def tab_map(i):
        return (jnp.where(i < nt1, i % tb1, (i - nt1) % tb2), 0)

    row = lambda i: (i, 0)
    const = lambda i: (0, 0)
    wd = N_HEADS * HEAD_DIM
    outs = [jax.ShapeDtypeStruct((t_all, N_HEADS * HEAD_PAD), BF16)] * 3
    out_specs = [pl.BlockSpec((tm, N_HEADS * HEAD_PAD), row)] * 3
    for dil in DILATIONS:
        outs += [jax.ShapeDtypeStruct((t_all // dil, dil * wd), BF16)] * 3
        out_specs += [pl.BlockSpec((tm // dil, dil * wd), row)] * 3
    return pl.pallas_call(
        functools.partial(_premix_kernel, nt1=nt1),
        out_shape=outs,
        grid=(t_all // tm,),
        in_specs=_group_specs(tm, D_MODEL, nt1) + [
                  pl.BlockSpec((1, 6, D_MODEL), lambda i: (i * tm // chunk, 0, 0)),
                  pl.BlockSpec((1, D_MODEL), const),
                  pl.BlockSpec(win2.shape, const),
                  pl.BlockSpec((1, Q_LORA), const),
                  pl.BlockSpec(wuq2.shape, const),
                  pl.BlockSpec((1, KV_LORA), const),
                  pl.BlockSpec(wukv2.shape, const),
                  pl.BlockSpec((tm, 512), tab_map)],
        out_specs=out_specs,
        scratch_shapes=[pltpu.VMEM((3 * wd // LANES, tm, LANES), F32)],
        compiler_params=_cparams(("parallel",)),
        name="premix",
    )(x_p, x_s, ada8, g_pre, win2, g_qa, wuq2, g_kva, wukv2, tab)


def _mla_kernel(q_ref, k_ref, v_ref, o_ref, s_ref, m_ref, *, tk):
    tq, seq = s_ref.shape[1], s_ref.shape[2]
    lane = lax.broadcasted_iota(jnp.int32, (tq, LANES), 1)

    def scores(hh):
        s = _dot_nt(q_ref[:, hh * HEAD_PAD:(hh + 1) * HEAD_PAD], k_ref[:, hh * HEAD_PAD:(hh + 1) * HEAD_PAD])
        s_ref[hh] = s
        m_ref[hh] = jnp.max(s, axis=-1, keepdims=True)

    def values(hh):
        acc = jnp.zeros((tq, LANES), F32)
        m = m_ref[hh]
        for c0 in range(0, seq, tk):
            p = jnp.exp2(s_ref[hh, :, c0:c0 + tk] - m)
            acc = acc + _dot(p.astype(BF16), v_ref[c0:c0 + tk, hh * LANES:(hh + 1) * LANES])
        return acc / pltpu.roll(acc, HEAD_DIM, axis=1)

    scores(0)
    scores(1)
    o0 = values(0)
    o1 = values(1)
    o_ref[...] = jnp.where(lane < HEAD_DIM, o0, o1).astype(o_ref.dtype)


def _mla(qa, ka, va, row_off, batch, seq):
    tq = TQ_MLA
    nq = seq // tq
    qoff = row_off // tq
    soff = row_off // seq
    return pl.pallas_call(
        functools.partial(_mla_kernel, tk=TK_MLA),
        out_shape=jax.ShapeDtypeStruct((batch * seq, N_HEADS * HEAD_DIM), BF16),
        grid=(batch, N_HEADS // 2, nq),
        in_specs=[pl.BlockSpec((tq, 2 * HEAD_PAD), lambda b, j, qi: (qoff + b * nq + qi, j)),
                  pl.BlockSpec((seq, 2 * HEAD_PAD), lambda b, j, qi: (soff + b, j)),
                  pl.BlockSpec((seq, 2 * LANES), lambda b, j, qi: (soff + b, j))],
        out_specs=pl.BlockSpec((tq, LANES), lambda b, j, qi: (b * nq + qi, j)),
        scratch_shapes=[pltpu.VMEM((2, tq, seq), F32), pltpu.VMEM((2, tq, 1), F32)],
        compiler_params=_cparams(("parallel", "parallel", "parallel")),
        name="mla_attention",
    )(qa, ka, va)


def _dil_kernel(q_ref, kp_ref, kc_ref, kn_ref, vp_ref, vc_ref, vn_ref, bias_ref, o_ref, lse_ref, *,
                n1_rows, len1, len2):
    tq = q_ref.shape[0]
    half = DIL_HALF
    row0 = pl.program_id(0) * tq
    in1 = row0 < n1_rows
    seq_len = jnp.where(in1, len1, len2)
    q0 = jnp.where(in1, row0 % len1, (row0 - n1_rows) % len2)
    variant = (q0 == 0).astype(jnp.int32) + 2 * (q0 + tq == seq_len).astype(jnp.int32)

    kw = jnp.concatenate([kp_ref[...], kc_ref[...], kn_ref[...]], axis=0)
    vw = jnp.concatenate([vp_ref[...], vc_ref[...], vn_ref[...]], axis=0)
    low = lax.broadcasted_iota(jnp.int32, (tq, LANES), 1) < HEAD_DIM
    low_w = lax.broadcasted_iota(jnp.int32, (kw.shape[0], LANES), 1) < HEAD_DIM
    heads = range(N_HEADS)
    pair = lambda x, h: x[:, (h // 2) * LANES:(h // 2 + 1) * LANES]
    qs = [jnp.where(low, pair(q_ref, h), 0) if h % 2 == 0 else jnp.where(low, 0, pair(q_ref, h)) for h in heads]
    vs = [jnp.where(low_w, pair(vw, h), 1) if h % 2 == 0 else jnp.where(low_w, 1, pair(vw, h)) for h in heads]
    ss = [_dot_nt(qs[h].astype(BF16), pair(kw, h)) + bias_ref[variant, h] for h in heads]
    ms = [jnp.max(s, axis=-1, keepdims=True) for s in ss]
    ps = [jnp.exp2(s - m).astype(BF16) for s, m in zip(ss, ms)]
    accs = [_dot(p, v.astype(BF16)) for p, v in zip(ps, vs)]
    ls = [pltpu.roll(acc, HEAD_DIM, axis=1) for acc in accs]
    outs = [acc / l for acc, l in zip(accs, ls)]
    lses = [m * math.log(2.0) + jnp.log(l) for m, l in zip(ms, ls)]
    for j in range(N_HEADS // 2):
        o_ref[:, j * LANES:(j + 1) * LANES] = jnp.where(low, outs[2 * j], outs[2 * j + 1]).astype(o_ref.dtype)
        lse_ref[:, j * LANES:(j + 1) * LANES] = jnp.where(low, lses[2 * j], lses[2 * j + 1])


def _dil_bias(dil, tq):
    half = DIL_HALF
    col = jnp.arange(tq + 2 * half)[None, :]
    dist = jnp.abs(col - half - jnp.arange(tq)[:, None])
    slopes = 2.0 ** (-8.0 * jnp.arange(1, N_HEADS + 1, dtype=F32) / N_HEADS)
    alibi = -math.log2(math.e) * slopes[:, None, None] * (dil * dist).astype(F32)[None]
    band = dist <= half
    first, last = col >= half, col < tq + half
    masks = [band, band & first, band & last, band & first & last]
    return jnp.stack([jnp.where(mk[None], alibi, NEG) for mk in masks])


def _dilated(qv, kv, vv, dil, n1, s1, s2):
    rows = qv.shape[0]
    tq = TQ_DIL
    nt = rows // tq
    wk = N_HEADS * HEAD_DIM
    cur = lambda i, r: (i, r)
    per = tq // DIL_HALF
    prv = lambda i, r: (jnp.maximum(i * per - 1, 0), r)
    nxt = lambda i, r: (jnp.minimum((i + 1) * per, nt * per - 1), r)
    kspecs = [pl.BlockSpec((DIL_HALF, wk), prv), pl.BlockSpec((tq, wk), cur), pl.BlockSpec((DIL_HALF, wk), nxt)]
    bias = _dil_bias(dil, tq)
    return pl.pallas_call(
        functools.partial(_dil_kernel, n1_rows=n1 // dil, len1=s1 // dil, len2=s2 // dil),
        out_shape=[jax.ShapeDtypeStruct((rows, dil * wk), BF16), jax.ShapeDtypeStruct((rows, dil * wk), F32)],
        grid=(nt, dil),
        in_specs=[pl.BlockSpec((tq, wk), cur)] + kspecs + kspecs
                 + [pl.BlockSpec(bias.shape, lambda i, r: (0, 0, 0, 0))],
        out_specs=[pl.BlockSpec((tq, wk), cur), pl.BlockSpec((tq, wk), cur)],
        compiler_params=_cparams(("parallel", "parallel")),
        name="dilated_attention_d%d" % dil,
    )(qv, kv, kv, kv, vv, vv, vv, bias)


def _postmix_kernel(xp_ref, xs_ref, oap_ref, oas_ref, o1_ref, o2_ref, o3_ref, l1_ref, l2_ref, l3_ref, ada_ref, goa_ref, gob_ref,
                    wo_ref, gpm_ref, gpf_ref, rwt_ref, rb_ref, wsg_ref, wsu_ref, wsd_ref,
                    x1_ref, h2_ref, sh_ref, idx_ref, gate_ref, rank_ref, cnt_ref, carry_ref, nat_ref, *, nt1):
    tm = xp_ref.shape[0]

    @pl.when(pl.program_id(0) == 0)
    def _():
        carry_ref[...] = jnp.zeros_like(carry_ref)

    gt1, sh2, sc2 = ada_ref[0, 2:3, :], ada_ref[0, 3:4, :], ada_ref[0, 4:5, :]

    wd = N_HEADS * HEAD_DIM
    n_slab = wd // LANES

    def row_order(src_ref, dil, base):
        if dil == 1:
            return src_ref[...].astype(F32)
        n = tm // dil
        for r in range(dil):
            for c in range(n_slab):
                col = r * wd + c * LANES
                nat_ref.at[base + c][pl.ds(r, n, stride=dil), :] = src_ref[:, col:col + LANES].astype(F32)
        return jnp.concatenate([nat_ref[base + c] for c in range(n_slab)], axis=1)

    o_pat = [row_order(ref, dil, (2 * p) * n_slab) for p, (ref, dil) in enumerate(zip((o1_ref, o2_ref, o3_ref), DILATIONS))]
    la, lb, lc = [row_order(ref, dil, (2 * p + 1) * n_slab)
                  for p, (ref, dil) in enumerate(zip((l1_ref, l2_ref, l3_ref), DILATIONS))]

    mx = jnp.maximum(jnp.maximum(la, lb), lc)
    ea, eb, ec = jnp.exp(la - mx), jnp.exp(lb - mx), jnp.exp(lc - mx)
    ob = (ea * o_pat[0] + eb * o_pat[1] + ec * o_pat[2]) / (ea + eb + ec)

    na = _rms(_group_tile(oap_ref, oas_ref, nt1).astype(F32), goa_ref[...]).astype(BF16)
    nb = _rms(ob, gob_ref[...]).astype(BF16)
    half_w = N_HEADS * HEAD_DIM
    o = _dot(na, wo_ref[0:half_w, :]) + _dot(nb, wo_ref[half_w:2 * half_w, :])
    x1 = _group_tile(xp_ref, xs_ref, nt1) + gt1 * _rms(o, gpm_ref[...])
    x1_ref[...] = x1
    h2 = _rms(x1, gpf_ref[...]) * (1.0 + sc2) + sh2
    h2b = h2.astype(BF16)
    h2_ref[...] = _pack_rows(h2)

    g = _dot(h2b, wsg_ref[...])
    u = _dot(h2b, wsu_ref[...])
    sh_ref[...] = _dot((g * _sigmoid(g) * u).astype(BF16), wsd_ref[...])

    scores = _sigmoid(_dot_nt(rwt_ref[...], h2, precision=lax.Precision.HIGHEST))
    biased = scores + rb_ref[...]
    ninf = -jnp.inf
    row = lax.broadcasted_iota(jnp.int32, (N_EXPERTS, tm), 0)
    rwg = lax.broadcasted_iota(jnp.int32, (GROUP_SIZE, tm), 0)
    gsc = []
    for gi in range(N_GROUPS):
        blk = biased[gi * GROUP_SIZE:(gi + 1) * GROUP_SIZE]
        m1 = jnp.max(blk, axis=0, keepdims=True)
        i1 = jnp.min(jnp.where(blk == m1, rwg, N_EXPERTS), axis=0, keepdims=True)
        m2 = jnp.max(jnp.where(rwg == i1, ninf, blk), axis=0, keepdims=True)
        gsc.append(m1 + m2)
    gsel = [jnp.zeros((1, tm), F32) for _ in range(N_GROUPS)]
    for _ in range(TOPK_GROUPS):
        m = functools.reduce(jnp.maximum, gsc)
        free = jnp.ones((1, tm), F32)
        for gi in range(N_GROUPS):
            hit = jnp.where(gsc[gi] == m, free, 0.0)
            free = free - hit
            gsel[gi] = gsel[gi] + hit
            gsc[gi] = jnp.where(hit > 0.0, ninf, gsc[gi])
    masked = jnp.concatenate(
        [jnp.where(gsel[gi] > 0.0, biased[gi * GROUP_SIZE:(gi + 1) * GROUP_SIZE], ninf)
         for gi in range(N_GROUPS)], axis=0)
    cur = masked
    idxs, gts = [], []
    for _ in range(TOP_K):
        m = jnp.max(cur, axis=0, keepdims=True)
        ik = jnp.min(jnp.where(cur == m, row, N_EXPERTS), axis=0, keepdims=True)
        hit = row == ik
        idxs.append(ik)
        gts.append(jnp.sum(jnp.where(hit, scores, 0.0), axis=0, keepdims=True))
        cur = jnp.where(hit, ninf, cur)
    idx = jnp.concatenate(idxs, axis=0)
    gates = jnp.concatenate(gts, axis=0)
    idx_ref[...] = idx
    gate_ref[...] = gates / jnp.sum(gates, axis=0, keepdims=True) * ROUTE_SCALE

    sel = jnp.where(cur != masked, 1.0, 0.0)
    tr = lax.broadcasted_iota(jnp.int32, (tm, tm), 0)
    tc = lax.broadcasted_iota(jnp.int32, (tm, tm), 1)
    before = jnp.where(tr < tc, 1.0, 0.0).astype(BF16)
    pos = _dot(sel.astype(BF16), before) + carry_ref[...]
    rank_ref[...] = jnp.concatenate(
        [jnp.sum(jnp.where(row == idxs[k], pos, 0.0), axis=0, keepdims=True) for k in range(TOP_K)],
        axis=0).astype(jnp.int32)
    carry = carry_ref[...] + jnp.sum(sel, axis=1, keepdims=True)
    carry_ref[...] = carry
    cnt_ref[...] = jnp.broadcast_to(carry, cnt_ref.shape)


def _postmix(x_p, x_s, oa_p, oa_s, os_, ls_, ada8, chunk, g_oa, g_ob, wo, g_pm, g_pf, rwt, rb, wsg, wsu, wsd):
    t_all = x_p.shape[0] + x_s.shape[0]
    tm = TM_POST
    nt1 = x_p.shape[0] // tm
    row = lambda i: (i, 0)
    col = lambda i: (0, i)
    const = lambda i: (0, 0)
    hw = N_HEADS * HEAD_DIM
    pat_specs = [pl.BlockSpec((tm // dil, dil * hw), row) for dil in DILATIONS]
    in_specs = (_group_specs(tm, D_MODEL, nt1) + _group_specs(tm, hw, nt1)
                + pat_specs + pat_specs
                + [pl.BlockSpec((1, 6, D_MODEL), lambda i: (i * tm // chunk, 0, 0)),
                   pl.BlockSpec((1, hw), const), pl.BlockSpec((1, hw), const),
                   pl.BlockSpec(wo.shape, const),
                   pl.BlockSpec((1, D_MODEL), const), pl.BlockSpec((1, D_MODEL), const),
                   pl.BlockSpec(rwt.shape, const), pl.BlockSpec((N_EXPERTS, 1), const),
                   pl.BlockSpec(wsg.shape, const), pl.BlockSpec(wsu.shape, const), pl.BlockSpec(wsd.shape, const)])
    out_shape = [jax.ShapeDtypeStruct((t_all, D_MODEL), F32),
                 jax.ShapeDtypeStruct((t_all, D_MODEL // 2), jnp.int32),
                 jax.ShapeDtypeStruct((t_all, D_MODEL), F32),
                 jax.ShapeDtypeStruct((TOP_K, t_all), jnp.int32),
                 jax.ShapeDtypeStruct((TOP_K, t_all), F32),
                 jax.ShapeDtypeStruct((TOP_K, t_all), jnp.int32),
                 jax.ShapeDtypeStruct((N_EXPERTS, LANES), F32)]
    out_specs = [pl.BlockSpec((tm, D_MODEL), row), pl.BlockSpec((tm, D_MODEL // 2), row), pl.BlockSpec((tm, D_MODEL), row),
                 pl.BlockSpec((TOP_K, tm), col), pl.BlockSpec((TOP_K, tm), col), pl.BlockSpec((TOP_K, tm), col),
                 pl.BlockSpec((N_EXPERTS, LANES), const)]
    return pl.pallas_call(
        functools.partial(_postmix_kernel, nt1=nt1),
        out_shape=out_shape,
        grid=(t_all // tm,),
        in_specs=in_specs,
        out_specs=out_specs,
        scratch_shapes=[pltpu.VMEM((N_EXPERTS, 1), F32),
                        pltpu.VMEM((2 * len(DILATIONS) * hw // LANES, tm, LANES), F32)],
        compiler_params=_cparams(("arbitrary",)),
        name="postmix_router",
    )(x_p, x_s, oa_p, oa_s, *os_, *ls_, ada8, g_oa, g_ob, wo, g_pm, g_pf, rwt, rb, wsg, wsu, wsd)


def _dest_kernel(idx_ref, rank_ref, start_ref, dest_ref):
    tm = idx_ref.shape[1]
    row = lax.broadcasted_iota(jnp.int32, (N_EXPERTS, tm), 0)
    start = start_ref[...]
    base = [jnp.sum(jnp.where(row == idx_ref[k:k + 1, :], start, 0), axis=0, keepdims=True) for k in range(TOP_K)]
    dest_ref[...] = jnp.concatenate(base, axis=0) + rank_ref[...]


def _dest(idx_t, rank_t, pad_start):
    t_all = idx_t.shape[1]
    tm = TM_DEST
    col = lambda i: (0, i)
    return pl.pallas_call(
        _dest_kernel,
        out_shape=jax.ShapeDtypeStruct((TOP_K, t_all), jnp.int32),
        grid=(t_all // tm,),
        in_specs=[pl.BlockSpec((TOP_K, tm), col), pl.BlockSpec((TOP_K, tm), col),
                  pl.BlockSpec((N_EXPERTS, 1), lambda i: (0, 0))],
        out_specs=pl.BlockSpec((TOP_K, tm), col),
        compiler_params=_cparams(("parallel",)),
        name="slot_index",
    )(idx_t, rank_t, pad_start.reshape(N_EXPERTS, 1))


def _sc_mesh():
    return plsc.VectorSubcoreMesh(core_axis_name="core", subcore_axis_name="subcore")


def _dispatch_rows(rows, dest, n_slots):
    t_all, width = rows.shape
    win = SC_WINDOW
    info = plsc.get_sparse_core_info()
    n_workers = info.num_cores * info.num_subcores
    per_worker = t_all // n_workers
    assert per_worker % win == 0

    @functools.partial(
        pl.kernel, out_type=jax.ShapeDtypeStruct((n_slots, width), rows.dtype), mesh=_sc_mesh(),
        scratch_types=[pltpu.VMEM((TOP_K, win), jnp.int32), pltpu.VMEM((win, width), rows.dtype),
                       pltpu.SemaphoreType.DMA])
    def scatter_kernel(x_hbm, i_hbm, o_hbm, idx_v, rows_v, sem):
        base = (lax.axis_index("subcore") * info.num_cores + lax.axis_index("core")) * per_worker

        @pl.loop(0, per_worker // win)
        def _(w):
            off = base + w * win
            for k in range(TOP_K):
                pltpu.sync_copy(i_hbm.at[pl.ds(k * t_all + off, win)], idx_v.at[k])
            pltpu.sync_copy(x_hbm.at[pl.ds(off, win)], rows_v)
            copies = [pltpu.async_copy(rows_v, o_hbm.at[idx_v.at[k]], sem) for k in range(TOP_K)]
            for c in copies:
                c.wait()

    return scatter_kernel(rows, dest)


def _gather_rows(slots, index):
    n = index.shape[0]
    width = slots.shape[1]
    win = SC_WINDOW
    info = plsc.get_sparse_core_info()
    n_workers = info.num_cores * info.num_subcores
    per_worker = n // n_workers
    assert per_worker % win == 0

    @functools.partial(
        pl.kernel, out_type=jax.ShapeDtypeStruct((n, width), slots.dtype), mesh=_sc_mesh(),
        scratch_types=[pltpu.VMEM((win,), jnp.int32), pltpu.VMEM((win, width), slots.dtype),
                       pltpu.SemaphoreType.DMA])
    def gather_kernel(y_hbm, i_hbm, o_hbm, idx_v, rows_v, sem):
        base = (lax.axis_index("subcore") * info.num_cores + lax.axis_index("core")) * per_worker

        @pl.loop(0, per_worker // win)
        def _(w):
            off = base + w * win
            pltpu.sync_copy(i_hbm.at[pl.ds(off, win)], idx_v)
            pltpu.async_copy(y_hbm.at[idx_v], rows_v, sem).wait()
            pltpu.sync_copy(rows_v, o_hbm.at[pl.ds(off, win)])

    return gather_kernel(slots, index)


def _expert_kernel(be_ref, bv_ref, nx_ref, nu_ref, x_hbm, wg_hbm, wu_hbm, wd_hbm, y_hbm,
                   x_buf, y_buf, wg_f, wu_f, wd_f, wgu_s, wd_s, sems, x_sems, y_sems):
    i = pl.program_id(0)
    n_used = nu_ref[0]
    used = i < n_used
    expert = be_ref[i]
    new_expert = (i == 0) | (expert != be_ref[jnp.maximum(i - 1, 0)])
    blk = x_buf.shape[1]
    slot = lax.rem(i, RING)

    def x_copy(b):
        s = lax.rem(b, RING)
        return pltpu.make_async_copy(x_hbm.at[pl.ds(pl.multiple_of(b * blk, blk), blk)], x_buf.at[s], x_sems.at[s])

    def y_copy(b):
        s = lax.rem(b, RING)
        return pltpu.make_async_copy(y_buf.at[s], y_hbm.at[pl.ds(pl.multiple_of(b * blk, blk), blk)], y_sems.at[s])

    def weight_copies(e):
        return (pltpu.make_async_copy(wg_hbm.at[e], wg_f, sems.at[0]),
                pltpu.make_async_copy(wu_hbm.at[e], wu_f, sems.at[1]),
                pltpu.make_async_copy(wd_hbm.at[e], wd_f, sems.at[2]))

    @pl.when(used & (i == 0))
    def _():
        for c in weight_copies(expert):
            c.start()
        for b in range(RING - 1):
            @pl.when(b < n_used)
            def _():
                x_copy(b).start()

    @pl.when(i + RING - 1 < n_used)
    def _():
        x_copy(i + RING - 1).start()

    @pl.when(used & new_expert)
    def _():
        for c in weight_copies(expert):
            c.wait()
        wgu_s[:, 0:D_EXPERT] = wg_f[...].astype(BF16)
        wgu_s[:, D_EXPERT:2 * D_EXPERT] = wu_f[...].astype(BF16)
        wd_s[...] = wd_f[...].astype(BF16)
        nxt = nx_ref[i]

        @pl.when(nxt >= 0)
        def _():
            for c in weight_copies(nxt):
                c.start()

    @pl.when(used)
    def _():
        x_copy(i).wait()

        @pl.when(i >= RING)
        def _():
            y_copy(i - RING).wait()

        live = lax.broadcasted_iota(jnp.int32, (blk, D_MODEL // 2), 0) < bv_ref[i]
        xa, xb = _unpack_rows(jnp.where(live, x_buf[slot], 0))
        half = D_MODEL // 2
        gu = _dot(xa.astype(BF16), wgu_s[0:half, :]) + _dot(xb.astype(BF16), wgu_s[half:D_MODEL, :])
        g, u = gu[:, 0:D_EXPERT], gu[:, D_EXPERT:2 * D_EXPERT]
        y_buf[slot] = _pack_rows(_dot((g * _sigmoid(g) * u).astype(BF16), wd_s[...]))
        y_copy(i).start()

    @pl.when(i == n_used - 1)
    def _():
        for back in range(RING):
            @pl.when(i - back >= 0)
            def _():
                y_copy(i - back).wait()


def _experts(block_expert, block_valid, block_next, n_used, x_slots, wg, wu, wd):
    p_rows = x_slots.shape[0]
    blk = EXPERT_BLOCK
    hbm = pl.BlockSpec(memory_space=pl.ANY)
    grid_spec = pltpu.PrefetchScalarGridSpec(
        num_scalar_prefetch=4,
        grid=(p_rows // blk,),
        in_specs=[hbm, hbm, hbm, hbm],
        out_specs=hbm,
        scratch_shapes=[pltpu.VMEM((RING, blk, D_MODEL // 2), jnp.int32),
                        pltpu.VMEM((RING, blk, D_MODEL // 2), jnp.int32),
                        pltpu.VMEM((D_MODEL, D_EXPERT), F32), pltpu.VMEM((D_MODEL, D_EXPERT), F32),
                        pltpu.VMEM((D_EXPERT, D_MODEL), F32),
                        pltpu.VMEM((D_MODEL, 2 * D_EXPERT), BF16), pltpu.VMEM((D_EXPERT, D_MODEL), BF16),
                        pltpu.SemaphoreType.DMA((3,)), pltpu.SemaphoreType.DMA((RING,)),
                        pltpu.SemaphoreType.DMA((RING,))])
    return pl.pallas_call(
        _expert_kernel,
        out_shape=jax.ShapeDtypeStruct((p_rows, D_MODEL // 2), jnp.int32),
        grid_spec=grid_spec,
        compiler_params=_cparams(("arbitrary",)),
        name="expert_ffn",
    )(block_expert, block_valid, block_next, n_used, x_slots, wg, wu, wd)


def _combine_kernel(yg_ref, gate_ref, sh_ref, x1_ref, ada_ref, g_ref, yp_ref, ys_ref, *, nt1):
    gates = gate_ref[...]
    half = D_MODEL // 2
    acc_a = sh_ref[:, 0:half]
    acc_b = sh_ref[:, half:D_MODEL]
    for k in range(TOP_K):
        ya, yb = _unpack_rows(yg_ref[k])
        acc_a = acc_a + gates[:, k:k + 1] * ya
        acc_b = acc_b + gates[:, k:k + 1] * yb
    gt2 = ada_ref[0, 5:6, :]
    y = x1_ref[...] + gt2 * _rms(jnp.concatenate([acc_a, acc_b], axis=1), g_ref[...])
    first = pl.program_id(0) < nt1

    @pl.when(first)
    def _():
        yp_ref[...] = y

    @pl.when(jnp.logical_not(first))
    def _():
        ys_ref[...] = y


def _combine(yg, gates_t, sh, x1, ada8, chunk, g_post, n1):
    t_all = x1.shape[0]
    tm = TM_COMB
    nt1 = n1 // tm
    row = lambda i: (i, 0)
    return pl.pallas_call(
        functools.partial(_combine_kernel, nt1=nt1),
        out_shape=[jax.ShapeDtypeStruct((n1, D_MODEL), F32), jax.ShapeDtypeStruct((t_all - n1, D_MODEL), F32)],
        grid=(t_all // tm,),
        in_specs=[pl.BlockSpec((TOP_K, tm, D_MODEL // 2), lambda i: (0, i, 0)),
                  pl.BlockSpec((tm, TOP_K), row),
                  pl.BlockSpec((tm, D_MODEL), row),
                  pl.BlockSpec((tm, D_MODEL), row),
                  pl.BlockSpec((1, 6, D_MODEL), lambda i: (i * tm // chunk, 0, 0)),
                  pl.BlockSpec((1, D_MODEL), lambda i: (0, 0))],
        out_specs=_group_specs(tm, D_MODEL, nt1),
        compiler_params=_cparams(("arbitrary",)),
        name="moe_combine",
    )(yg, gates_t, sh, x1, ada8, g_post)


def _rope_partner(w):
    half = QK_ROPE // 2
    return jnp.concatenate([-w[..., half:], w[..., :half]], axis=-1)


def _prep_weights(w_in, w_uq, w_ukv):
    d = w_in.shape[0]
    zeros = lambda r, c: jnp.zeros((r, c), F32)
    kr = w_in[:, 384:416]
    r_main = jnp.concatenate([zeros(d, QK_NOPE), kr, zeros(d, HEAD_PAD - QK_NOPE - QK_ROPE)], axis=1)
    r_part = jnp.concatenate([zeros(d, QK_NOPE), _rope_partner(kr), zeros(d, HEAD_PAD - QK_NOPE - QK_ROPE)], axis=1)
    win2 = jnp.concatenate([w_in[:, :384], r_main, r_part, w_in[:, 416:]], axis=1).astype(BF16)

    wq = w_uq.reshape(Q_LORA, N_HEADS, QK_NOPE + QK_ROPE)
    zq = jnp.zeros((Q_LORA, N_HEADS, HEAD_PAD - QK_NOPE - QK_ROPE), F32)
    q_main = jnp.concatenate([wq, zq], axis=-1).reshape(Q_LORA, N_HEADS * HEAD_PAD)
    q_part = jnp.concatenate([jnp.zeros((Q_LORA, N_HEADS, QK_NOPE), F32), _rope_partner(wq[..., QK_NOPE:]), zq],
                             axis=-1).reshape(Q_LORA, N_HEADS * HEAD_PAD)
    wuq2 = jnp.concatenate([q_main, q_part], axis=1).astype(BF16)

    wkv = w_ukv.reshape(KV_LORA, N_HEADS, QK_NOPE + HEAD_DIM)
    k_pad = jnp.concatenate([wkv[..., :QK_NOPE], jnp.zeros((KV_LORA, N_HEADS, HEAD_PAD - QK_NOPE), F32)],
                            axis=-1).reshape(KV_LORA, N_HEADS * HEAD_PAD)
    v_cols = wkv[..., QK_NOPE:].reshape(KV_LORA, N_HEADS * HEAD_DIM)
    wukv2 = jnp.concatenate([k_pad, v_cols], axis=1).astype(BF16)
    return win2, wuq2, wukv2


def _rope_table(s_max):
    half = QK_ROPE // 2
    inv = ROPE_BASE ** (-jnp.arange(half, dtype=F32) / half)
    ang = jnp.arange(s_max, dtype=F32)[:, None] * inv[None, :]
    cos = jnp.concatenate([jnp.cos(ang), jnp.cos(ang)], axis=1)
    sin = jnp.concatenate([jnp.sin(ang), jnp.sin(ang)], axis=1)
    scale = (QK_NOPE + QK_ROPE) ** -0.5 * math.log2(math.e)
    pad = jnp.zeros((s_max, HEAD_PAD - QK_NOPE - QK_ROPE), F32)
    zn = jnp.zeros((s_max, QK_NOPE), F32)
    cosq = jnp.concatenate([jnp.full((s_max, QK_NOPE), scale, F32), cos * scale, pad], axis=1)
    sinq = jnp.concatenate([zn, sin * scale, pad], axis=1)
    cosk = jnp.concatenate([zn, cos, pad], axis=1)
    sink = jnp.concatenate([zn, sin, pad], axis=1)
    return jnp.concatenate([cosq, sinq, cosk, sink], axis=1)


def _layer(x_prompt, x_sample, c_prompt, c_sample, w_ada, b_ada, g_pre_mix, w_in, g_qa, w_uq, g_kva, w_ukv,
           g_out_a, g_out_b, w_o, g_post_mix, g_pre_ffn, router_w, router_bias, w_exp_gate, w_exp_up, w_exp_down,
           w_sh_gate, w_sh_up, w_sh_down, g_post_ffn):
    b1, s1, d = x_prompt.shape
    b2, s2, _ = x_sample.shape
    n1, n2 = b1 * s1, b2 * s2
    t_all = n1 + n2
    chunk = math.gcd(s1, s2)
    assert n1 % s2 == 0 and chunk % max(TM_PRE, TM_POST, TM_COMB) == 0

    x_p, x_s = x_prompt.reshape(n1, d), x_sample.reshape(n2, d)
    c_all = jnp.concatenate([c_prompt, c_sample, jnp.zeros((8 - b1 - b2, d), F32)], axis=0)
    ada = _ada(c_all, w_ada, b_ada)
    chunk_batch = jnp.concatenate([jnp.repeat(jnp.arange(b1), s1 // chunk),
                                   b1 + jnp.repeat(jnp.arange(b2), s2 // chunk)])
    ada8 = ada.reshape(8, 6, d)[chunk_batch]

    win2, wuq2, wukv2 = _prep_weights(w_in, w_uq, w_ukv)
    tab = _rope_table(max(s1, s2))
    r2 = lambda g: g.reshape(1, -1)
    qa, ka, va, *dil_qkv = _premix(x_p, x_s, ada8, chunk, r2(g_pre_mix), win2, r2(g_qa), wuq2, r2(g_kva), wukv2, tab,
                                   n1, s1, s2)

    oa_p, oa_s = _mla(qa, ka, va, 0, b1, s1), _mla(qa, ka, va, n1, b2, s2)
    os_, ls_ = [], []
    for p, dil in enumerate(DILATIONS):
        o, lse = _dilated(*dil_qkv[3 * p:3 * p + 3], dil, n1, s1, s2)
        os_.append(o)
        ls_.append(lse)

    x1, h2p, sh, idx_t, gate_t, rank_t, cnt = _postmix(
        x_p, x_s, oa_p, oa_s, os_, ls_, ada8, chunk, r2(g_out_a), r2(g_out_b), w_o.astype(BF16), r2(g_post_mix), r2(g_pre_ffn),
        router_w.T, router_bias.reshape(N_EXPERTS, 1), w_sh_gate.astype(BF16), w_sh_up.astype(BF16),
        w_sh_down.astype(BF16))

    blk = EXPERT_BLOCK
    n_assign = t_all * TOP_K
    n_blocks = -(-(n_assign + N_EXPERTS * (blk - 1)) // blk)
    counts = cnt[:, 0].astype(jnp.int32)
    padded = (counts + blk - 1) // blk * blk
    pad_end = jnp.cumsum(padded)
    pad_start = pad_end - padded
    dest = _dest(idx_t, rank_t, pad_start)
    block_first = jnp.arange(n_blocks, dtype=jnp.int32) * blk
    block_expert = jnp.minimum(jnp.sum((pad_end[None, :] <= block_first[:, None]).astype(jnp.int32), axis=1),
                               N_EXPERTS - 1)
    n_used = (pad_end[-1] // blk).astype(jnp.int32).reshape(1)
    block_valid = jnp.clip(counts[block_expert] - (block_first - pad_start[block_expert]), 0, blk)
    eid = jnp.arange(N_EXPERTS, dtype=jnp.int32)
    used_from = lax.cummin(jnp.where(counts > 0, eid, N_EXPERTS), axis=0, reverse=True)
    next_used = jnp.concatenate([used_from[1:], jnp.full((1,), N_EXPERTS, jnp.int32)])
    block_next = jnp.where(next_used < N_EXPERTS, next_used, -1)[block_expert]
    dest_flat = dest.reshape(n_assign)
    x_slots = _dispatch_rows(h2p, dest_flat, n_blocks * blk)
    y_slots = _experts(block_expert, block_valid, block_next, n_used, x_slots, w_exp_gate, w_exp_up, w_exp_down)
    yg = _gather_rows(y_slots, dest_flat).reshape(TOP_K, t_all, d // 2)

    y_p, y_s = _combine(yg, gate_t.T, sh, x1, ada8, chunk, r2(g_post_ffn), n1)
    return y_p.reshape(b1, s1, d), y_s.reshape(b2, s2, d)


def kernel(x_prompt, x_sample, c_prompt, c_sample, w_ada, b_ada, g_pre_mix, w_in, g_qa, w_uq, g_kva, w_ukv, g_out_a, g_out_b, w_o, g_post_mix, g_pre_ffn, router_w, router_bias, w_exp_gate, w_exp_up, w_exp_down, w_sh_gate, w_sh_up, w_sh_down, g_post_ffn):
    layer = [p[0] for p in (w_ada, b_ada, g_pre_mix, w_in, g_qa, w_uq, g_kva, w_ukv, g_out_a, g_out_b, w_o,
                            g_post_mix, g_pre_ffn, router_w, router_bias, w_exp_gate, w_exp_up, w_exp_down,
                            w_sh_gate, w_sh_up, w_sh_down, g_post_ffn)]
    return _layer(x_prompt, x_sample, c_prompt, c_sample, *layer)
```

```python
import functools
import math

import jax
import jax.numpy as jnp
from jax import lax
from jax.experimental import pallas as pl
from jax.experimental.pallas import tpu as pltpu
from jax.experimental.pallas import tpu_sc as plsc

F32 = jnp.float32
BF16 = jnp.bfloat16

D_MODEL = 1024
HEAD_DIM = 64
N_HEADS = 8
Q_LORA = 256
KV_LORA = 128
QK_NOPE = 64
QK_ROPE = 32
ROPE_BASE = 10000.0
DIL_PATTERNS = ((128, 1), (512, 4), (2048, 16))
DILATIONS = tuple(d for _, d in DIL_PATTERNS)
DIL_HALF = 64
assert all(w // (2 * d) == DIL_HALF for w, d in DIL_PATTERNS) and DILATIONS[0] == 1
N_EXPERTS = 256
TOP_K = 8
N_GROUPS = 8
GROUP_SIZE = N_EXPERTS // N_GROUPS
TOPK_GROUPS = 4
D_EXPERT = 256
ROUTE_SCALE = 2.5
EPS = 1e-6
NEG = -1e30

LANES = 128
HEAD_PAD = 128

TM_PRE = 512
TM_POST = 512
TM_COMB = 512
TQ_MLA = 256
TK_MLA = 512
TM_DEST = 2048
SC_WINDOW = 128
TQ_DIL = 128
EXPERT_BLOCK = 512
RING = 3
VMEM_LIMIT = 56 * 1024 * 1024


def _cparams(sem):
    return pltpu.CompilerParams(dimension_semantics=sem, vmem_limit_bytes=VMEM_LIMIT)


def _rms(x, g):
    return x * lax.rsqrt(jnp.mean(x * x, axis=-1, keepdims=True) + EPS) * g


def _sigmoid(x):
    return 1.0 / (1.0 + jnp.exp(-x))


def _dot(a, b):
    return jnp.dot(a, b, preferred_element_type=F32)


def _group_specs(tm, width, nt1):
    return [pl.BlockSpec((tm, width), lambda i: (jnp.minimum(i, nt1 - 1), 0)),
            pl.BlockSpec((tm, width), lambda i: (jnp.maximum(i - nt1, 0), 0))]


def _group_tile(first_ref, second_ref, nt1):
    return jnp.where(pl.program_id(0) < nt1, first_ref[...], second_ref[...])


def _pack_rows(x):
    n = x.shape[1] // 2
    bits = lax.bitcast_convert_type(x.astype(BF16).astype(F32), jnp.int32)
    return bits[:, :n] | lax.shift_right_logical(bits[:, n:], 16)


def _unpack_rows(u):
    hi = lax.bitcast_convert_type(u & jnp.int32(-65536), F32)
    lo = lax.bitcast_convert_type(lax.shift_left(u, 16), F32)
    return hi, lo


def _dot_nt(a, b, precision=None):
    return lax.dot_general(a, b, (((1,), (1,)), ((), ())), preferred_element_type=F32, precision=precision)


def _ada_kernel(c_ref, w_ref, b_ref, o_ref):
    c = c_ref[...]
    s = c * _sigmoid(c)
    o_ref[...] = jnp.dot(s, w_ref[...], preferred_element_type=F32, precision=lax.Precision.HIGHEST) + b_ref[...]


def _ada(c_all, w_ada, b_ada):
    nb, d = c_all.shape
    n_out = w_ada.shape[1]
    tn = 1024
    return pl.pallas_call(
        _ada_kernel,
        out_shape=jax.ShapeDtypeStruct((nb, n_out), F32),
        grid=(n_out // tn,),
        in_specs=[pl.BlockSpec((nb, d), lambda j: (0, 0)),
                  pl.BlockSpec((d, tn), lambda j: (0, j)),
                  pl.BlockSpec((1, tn), lambda j: (0, j))],
        out_specs=pl.BlockSpec((nb, tn), lambda j: (0, j)),
        compiler_params=_cparams(("arbitrary",)),
        name="ada",
    )(c_all, w_ada, b_ada.reshape(1, n_out))


def _premix_kernel(xp_ref, xs_ref, ada_ref, g_ref, win_ref, gqa_ref, wuq_ref, gkva_ref, wukv_ref, tab_ref,
                   qa_ref, ka_ref, va_ref, *rest, nt1):
    dil_refs, zs_ref = rest[:-1], rest[-1]
    x = _group_tile(xp_ref, xs_ref, nt1)
    sh1 = ada_ref[0, 0:1, :]
    sc1 = ada_ref[0, 1:2, :]
    h = _rms(x, g_ref[...]) * (1.0 + sc1) + sh1
    z = _dot(h.astype(BF16), win_ref[...])
    tab = tab_ref[...]
    cosq, sinq = tab[:, 0:128], tab[:, 128:256]
    cosk, sink = tab[:, 256:384], tab[:, 384:512]
    nh = N_HEADS * HEAD_PAD

    cq = _rms(z[:, 0:Q_LORA], gqa_ref[...]).astype(BF16)
    qq = _dot(cq, wuq_ref[...])
    for h_i in range(N_HEADS):
        lo = h_i * HEAD_PAD
        qa_ref[:, lo:lo + HEAD_PAD] = (qq[:, lo:lo + HEAD_PAD] * cosq
                                       + qq[:, nh + lo:nh + lo + HEAD_PAD] * sinq).astype(BF16)

    ckv = _rms(z[:, 256:384], gkva_ref[...]).astype(BF16)
    kk = _dot(ckv, wukv_ref[...])
    rr = z[:, 384:512] * cosk + z[:, 512:640] * sink
    for h_i in range(N_HEADS):
        lo = h_i * HEAD_PAD
        ka_ref[:, lo:lo + HEAD_PAD] = (kk[:, lo:lo + HEAD_PAD] + rr).astype(BF16)
    low = lax.broadcasted_iota(jnp.int32, (x.shape[0], LANES), 1) < HEAD_DIM
    for j in range(N_HEADS // 2):
        vpair = kk[:, nh + j * LANES:nh + (j + 1) * LANES]
        va_ref[:, (2 * j) * LANES:(2 * j + 1) * LANES] = jnp.where(low, vpair, 1.0).astype(BF16)
        va_ref[:, (2 * j + 1) * LANES:(2 * j + 2) * LANES] = jnp.where(low, 1.0, vpair).astype(BF16)

    tm = x.shape[0]
    wd = N_HEADS * HEAD_DIM
    n_slab = 3 * wd // LANES
    for c in range(n_slab):
        scale = HEAD_DIM ** -0.5 * math.log2(math.e) if c < wd // LANES else 1.0
        zs_ref[c] = z[:, 640 + c * LANES:640 + (c + 1) * LANES] * scale
    for dil, refs in zip(DILATIONS, (dil_refs[0:3], dil_refs[3:6], dil_refs[6:9])):
        n = tm // dil
        for r in range(dil):
            for c in range(n_slab):
                rows = zs_ref[c] if dil == 1 else zs_ref.at[c][pl.ds(r, n, stride=dil), :]
                col = r * wd + (c % (wd // LANES)) * LANES
                refs[c // (wd // LANES)][:, col:col + LANES] = rows.astype(BF16)


def _premix(x_p, x_s, ada8, chunk, g_pre, win2, g_qa, wuq2, g_kva, wukv2, tab, n1, s1, s2):
    t_all = x_p.shape[0] + x_s.shape[0]
    tm = TM_PRE
    nt1 = n1 // tm
    tb1, tb2 = s1 // tm, s2 // tm

    def tab_map(i):
        return (jnp.where(i < nt1, i % tb1, (i - nt1) % tb2), 0)

    row = lambda i: (i, 0)
    const = lambda i: (0, 0)
    wd = N_HEADS * HEAD_DIM
    outs = [jax.ShapeDtypeStruct((t_all, N_HEADS * HEAD_PAD), BF16)] * 3
    out_specs = [pl.BlockSpec((tm, N_HEADS * HEAD_PAD), row)] * 3
    for dil in DILATIONS:
        outs += [jax.ShapeDtypeStruct((t_all // dil, dil * wd), BF16)] * 3
        out_specs += [pl.BlockSpec((tm // dil, dil * wd), row)] * 3
    return pl.pallas_call(
        functools.partial(_premix_kernel, nt1=nt1),
        out_shape=outs,
        grid=(t_all // tm,),
        in_specs=_group_specs(tm, D_MODEL, nt1) + [
                  pl.BlockSpec((1, 6, D_MODEL), lambda i: (i * tm // chunk, 0, 0)),
                  pl.BlockSpec((1, D_MODEL), const),
                  pl.BlockSpec(win2.shape, const),
                  pl.BlockSpec((1, Q_LORA), const),
                  pl.BlockSpec(wuq2.shape, const),
                  pl.BlockSpec((1, KV_LORA), const),
                  pl.BlockSpec(wukv2.shape, const),
                  pl.BlockSpec((tm, 512), tab_map)],
        out_specs=out_specs,
        scratch_shapes=[pltpu.VMEM((3 * wd // LANES, tm, LANES), F32)],
        compiler_params=_cparams(("parallel",)),
        name="premix",
    )(x_p, x_s, ada8, g_pre, win2, g_qa, wuq2, g_kva, wukv2, tab)


def _mla_kernel(q_ref, k_ref, v_ref, o_ref, s_ref, m_ref, *, tk):
    tq, seq = s_ref.shape[1], s_ref.shape[2]
    lane = lax.broadcasted_iota(jnp.int32, (tq, LANES), 1)

    def scores(hh):
        s = _dot_nt(q_ref[:, hh * HEAD_PAD:(hh + 1) * HEAD_PAD], k_ref[:, hh * HEAD_PAD:(hh + 1) * HEAD_PAD])
        s_ref[hh] = s
        m_ref[hh] = jnp.max(s, axis=-1, keepdims=True)

    def values(hh):
        acc = jnp.zeros((tq, LANES), F32)
        m = m_ref[hh]
        for c0 in range(0, seq, tk):
            p = jnp.exp2(s_ref[hh, :, c0:c0 + tk] - m)
            acc = acc + _dot(p.astype(BF16), v_ref[c0:c0 + tk, hh * LANES:(hh + 1) * LANES])
        return acc / pltpu.roll(acc, HEAD_DIM, axis=1)

    scores(0)
    scores(1)
    o0 = values(0)
    o1 = values(1)
    o_ref[...] = jnp.where(lane < HEAD_DIM, o0, o1).astype(o_ref.dtype)


def _mla(qa, ka, va, row_off, batch, seq):
    tq = TQ_MLA
    nq = seq // tq
    qoff = row_off // tq
    soff = row_off // seq
    return pl.pallas_call(
        functools.partial(_mla_kernel, tk=TK_MLA),
        out_shape=jax.ShapeDtypeStruct((batch * seq, N_HEADS * HEAD_DIM), BF16),
        grid=(batch, N_HEADS // 2, nq),
        in_specs=[pl.BlockSpec((tq, 2 * HEAD_PAD), lambda b, j, qi: (qoff + b * nq + qi, j)),
                  pl.BlockSpec((seq, 2 * HEAD_PAD), lambda b, j, qi: (soff + b, j)),
                  pl.BlockSpec((seq, 2 * LANES), lambda b, j, qi: (soff + b, j))],
        out_specs=pl.BlockSpec((tq, LANES), lambda b, j, qi: (b * nq + qi, j)),
        scratch_shapes=[pltpu.VMEM((2, tq, seq), F32), pltpu.VMEM((2, tq, 1), F32)],
        compiler_params=_cparams(("parallel", "parallel", "parallel")),
        name="mla_attention",
    )(qa, ka, va)


def _dil_kernel(q_ref, kp_ref, kc_ref, kn_ref, vp_ref, vc_ref, vn_ref, bias_ref, o_ref, lse_ref, *,
                n1_rows, len1, len2):
    tq = q_ref.shape[0]
    half = DIL_HALF
    row0 = pl.program_id(0) * tq
    in1 = row0 < n1_rows
    seq_len = jnp.where(in1, len1, len2)
    q0 = jnp.where(in1, row0 % len1, (row0 - n1_rows) % len2)
    variant = (q0 == 0).astype(jnp.int32) + 2 * (q0 + tq == seq_len).astype(jnp.int32)

    kw = jnp.concatenate([kp_ref[...], kc_ref[...], kn_ref[...]], axis=0)
    vw = jnp.concatenate([vp_ref[...], vc_ref[...], vn_ref[...]], axis=0)
    low = lax.broadcasted_iota(jnp.int32, (tq, LANES), 1) < HEAD_DIM
    low_w = lax.broadcasted_iota(jnp.int32, (kw.shape[0], LANES), 1) < HEAD_DIM
    heads = range(N_HEADS)
    pair = lambda x, h: x[:, (h // 2) * LANES:(h // 2 + 1) * LANES]
    qs = [jnp.where(low, pair(q_ref, h), 0) if h % 2 == 0 else jnp.where(low, 0, pair(q_ref, h)) for h in heads]
    vs = [jnp.where(low_w, pair(vw, h), 1) if h % 2 == 0 else jnp.where(low_w, 1, pair(vw, h)) for h in heads]
    ss = [_dot_nt(qs[h].astype(BF16), pair(kw, h)) + bias_ref[variant, h] for h in heads]
    ms = [jnp.max(s, axis=-1, keepdims=True) for s in ss]
    ps = [jnp.exp2(s - m).astype(BF16) for s, m in zip(ss, ms)]
    accs = [_dot(p, v.astype(BF16)) for p, v in zip(ps, vs)]
    ls = [pltpu.roll(acc, HEAD_DIM, axis=1) for acc in accs]
    outs = [acc / l for acc, l in zip(accs, ls)]
    lses = [m * math.log(2.0) + jnp.log(l) for m, l in zip(ms, ls)]
    for j in range(N_HEADS // 2):
        o_ref[:, j * LANES:(j + 1) * LANES] = jnp.where(low, outs[2 * j], outs[2 * j + 1]).astype(o_ref.dtype)
        lse_ref[:, j * LANES:(j + 1) * LANES] = jnp.where(low, lses[2 * j], lses[2 * j + 1])


def _dil_bias(dil, tq):
    half = DIL_HALF
    col = jnp.arange(tq + 2 * half)[None, :]
    dist = jnp.abs(col - half - jnp.arange(tq)[:, None])
    slopes = 2.0 ** (-8.0 * jnp.arange(1, N_HEADS + 1, dtype=F32) / N_HEADS)
    alibi = -math.log2(math.e) * slopes[:, None, None] * (dil * dist).astype(F32)[None]
    band = dist <= half
    first, last = col >= half, col < tq + half
    masks = [band, band & first, band & last, band & first & last]
    return jnp.stack([jnp.where(mk[None], alibi, NEG) for mk in masks])


def _dilated(qv, kv, vv, dil, n1, s1, s2):
    rows = qv.shape[0]
    tq = TQ_DIL
    nt = rows // tq
    wk = N_HEADS * HEAD_DIM
    cur = lambda i, r: (i, r)
    per = tq // DIL_HALF
    prv = lambda i, r: (jnp.maximum(i * per - 1, 0), r)
    nxt = lambda i, r: (jnp.minimum((i + 1) * per, nt * per - 1), r)
    kspecs = [pl.BlockSpec((DIL_HALF, wk), prv), pl.BlockSpec((tq, wk), cur), pl.BlockSpec((DIL_HALF, wk), nxt)]
    bias = _dil_bias(dil, tq)
    return pl.pallas_call(
        functools.partial(_dil_kernel, n1_rows=n1 // dil, len1=s1 // dil, len2=s2 // dil),
        out_shape=[jax.ShapeDtypeStruct((rows, dil * wk), BF16), jax.ShapeDtypeStruct((rows, dil * wk), F32)],
        grid=(nt, dil),
        in_specs=[pl.BlockSpec((tq, wk), cur)] + kspecs + kspecs
                 + [pl.BlockSpec(bias.shape, lambda i, r: (0, 0, 0, 0))],
        out_specs=[pl.BlockSpec((tq, wk), cur), pl.BlockSpec((tq, wk), cur)],
        compiler_params=_cparams(("parallel", "parallel")),
        name="dilated_attention_d%d" % dil,
    )(qv, kv, kv, kv, vv, vv, vv, bias)


def _postmix_kernel(xp_ref, xs_ref, oap_ref, oas_ref, o1_ref, o2_ref, o3_ref, l1_ref, l2_ref, l3_ref, ada_ref, goa_ref, gob_ref,
                    wo_ref, gpm_ref, gpf_ref, rwt_ref, rb_ref, wsg_ref, wsu_ref, wsd_ref,
                    x1_ref, h2_ref, sh_ref, idx_ref, gate_ref, rank_ref, cnt_ref, carry_ref, nat_ref, *, nt1):
    tm = xp_ref.shape[0]

    @pl.when(pl.program_id(0) == 0)
    def _():
        carry_ref[...] = jnp.zeros_like(carry_ref)

    gt1, sh2, sc2 = ada_ref[0, 2:3, :], ada_ref[0, 3:4, :], ada_ref[0, 4:5, :]

    wd = N_HEADS * HEAD_DIM
    n_slab = wd // LANES

    def row_order(src_ref, dil, base):
        if dil == 1:
            return src_ref[...].astype(F32)
        n = tm // dil
        for r in range(dil):
            for c in range(n_slab):
                col = r * wd + c * LANES
                nat_ref.at[base + c][pl.ds(r, n, stride=dil), :] = src_ref[:, col:col + LANES].astype(F32)
        return jnp.concatenate([nat_ref[base + c] for c in range(n_slab)], axis=1)

    o_pat = [row_order(ref, dil, (2 * p) * n_slab) for p, (ref, dil) in enumerate(zip((o1_ref, o2_ref, o3_ref), DILATIONS))]
    la, lb, lc = [row_order(ref, dil, (2 * p + 1) * n_slab)
                  for p, (ref, dil) in enumerate(zip((l1_ref, l2_ref, l3_ref), DILATIONS))]

    mx = jnp.maximum(jnp.maximum(la, lb), lc)
    ea, eb, ec = jnp.exp(la - mx), jnp.exp(lb - mx), jnp.exp(lc - mx)
    ob = (ea * o_pat[0] + eb * o_pat[1] + ec * o_pat[2]) / (ea + eb + ec)

    na = _rms(_group_tile(oap_ref, oas_ref, nt1).astype(F32), goa_ref[...]).astype(BF16)
    nb = _rms(ob, gob_ref[...]).astype(BF16)
    half_w = N_HEADS * HEAD_DIM
    o = _dot(na, wo_ref[0:half_w, :]) + _dot(nb, wo_ref[half_w:2 * half_w, :])
    x1 = _group_tile(xp_ref, xs_ref, nt1) + gt1 * _rms(o, gpm_ref[...])
    x1_ref[...] = x1
    h2 = _rms(x1, gpf_ref[...]) * (1.0 + sc2) + sh2
    h2b = h2.astype(BF16)
    h2_ref[...] = _pack_rows(h2)

    g = _dot(h2b, wsg_ref[...])
    u = _dot(h2b, wsu_ref[...])
    sh_ref[...] = _dot((g * _sigmoid(g) * u).astype(BF16), wsd_ref[...])

    scores = _sigmoid(_dot_nt(rwt_ref[...], h2, precision=lax.Precision.HIGHEST))
    biased = scores + rb_ref[...]
    ninf = -jnp.inf
    row = lax.broadcasted_iota(jnp.int32, (N_EXPERTS, tm), 0)
    rwg = lax.broadcasted_iota(jnp.int32, (GROUP_SIZE, tm), 0)
    gsc = []
    for gi in range(N_GROUPS):
        blk = biased[gi * GROUP_SIZE:(gi + 1) * GROUP_SIZE]
        m1 = jnp.max(blk, axis=0, keepdims=True)
        i1 = jnp.min(jnp.where(blk == m1, rwg, N_EXPERTS), axis=0, keepdims=True)
        m2 = jnp.max(jnp.where(rwg == i1, ninf, blk), axis=0, keepdims=True)
        gsc.append(m1 + m2)
    gsel = [jnp.zeros((1, tm), F32) for _ in range(N_GROUPS)]
    for _ in range(TOPK_GROUPS):
        m = functools.reduce(jnp.maximum, gsc)
        free = jnp.ones((1, tm), F32)
        for gi in range(N_GROUPS):
            hit = jnp.where(gsc[gi] == m, free, 0.0)
            free = free - hit
            gsel[gi] = gsel[gi] + hit
            gsc[gi] = jnp.where(hit > 0.0, ninf, gsc[gi])
    masked = jnp.concatenate(
        [jnp.where(gsel[gi] > 0.0, biased[gi * GROUP_SIZE:(gi + 1) * GROUP_SIZE], ninf)
         for gi in range(N_GROUPS)], axis=0)
    cur = masked
    idxs, gts = [], []
    for _ in range(TOP_K):
        m = jnp.max(cur, axis=0, keepdims=True)
        ik = jnp.min(jnp.where(cur == m, row, N_EXPERTS), axis=0, keepdims=True)
        hit = row == ik
        idxs.append(ik)
        gts.append(jnp.sum(jnp.where(hit, scores, 0.0), axis=0, keepdims=True))
        cur = jnp.where(hit, ninf, cur)
    idx = jnp.concatenate(idxs, axis=0)
    gates = jnp.concatenate(gts, axis=0)
    idx_ref[...] = idx
    gate_ref[...] = gates / jnp.sum(gates, axis=0, keepdims=True) * ROUTE_SCALE

    sel = jnp.where(cur != masked, 1.0, 0.0)
    tr = lax.broadcasted_iota(jnp.int32, (tm, tm), 0)
    tc = lax.broadcasted_iota(jnp.int32, (tm, tm), 1)
    before = jnp.where(tr < tc, 1.0, 0.0).astype(BF16)
    pos = _dot(sel.astype(BF16), before) + carry_ref[...]
    rank_ref[...] = jnp.concatenate(
        [jnp.sum(jnp.where(row == idxs[k], pos, 0.0), axis=0, keepdims=True) for k in range(TOP_K)],
        axis=0).astype(jnp.int32)
    carry = carry_ref[...] + jnp.sum(sel, axis=1, keepdims=True)
    carry_ref[...] = carry
    cnt_ref[...] = jnp.broadcast_to(carry, cnt_ref.shape)


def _postmix(x_p, x_s, oa_p, oa_s, os_, ls_, ada8, chunk, g_oa, g_ob, wo, g_pm, g_pf, rwt, rb, wsg, wsu, wsd):
    t_all = x_p.shape[0] + x_s.shape[0]
    tm = TM_POST
    nt1 = x_p.shape[0] // tm
    row = lambda i: (i, 0)
    col = lambda i: (0, i)
    const = lambda i: (0, 0)
    hw = N_HEADS * HEAD_DIM
    pat_specs = [pl.BlockSpec((tm // dil, dil * hw), row) for dil in DILATIONS]
    in_specs = (_group_specs(tm, D_MODEL, nt1) + _group_specs(tm, hw, nt1)
                + pat_specs + pat_specs
                + [pl.BlockSpec((1, 6, D_MODEL), lambda i: (i * tm // chunk, 0, 0)),
                   pl.BlockSpec((1, hw), const), pl.BlockSpec((1, hw), const),
                   pl.BlockSpec(wo.shape, const),
                   pl.BlockSpec((1, D_MODEL), const), pl.BlockSpec((1, D_MODEL), const),
                   pl.BlockSpec(rwt.shape, const), pl.BlockSpec((N_EXPERTS, 1), const),
                   pl.BlockSpec(wsg.shape, const), pl.BlockSpec(wsu.shape, const), pl.BlockSpec(wsd.shape, const)])
    out_shape = [jax.ShapeDtypeStruct((t_all, D_MODEL), F32),
                 jax.ShapeDtypeStruct((t_all, D_MODEL // 2), jnp.int32),
                 jax.ShapeDtypeStruct((t_all, D_MODEL), F32),
                 jax.ShapeDtypeStruct((TOP_K, t_all), jnp.int32),
                 jax.ShapeDtypeStruct((TOP_K, t_all), F32),
                 jax.ShapeDtypeStruct((TOP_K, t_all), jnp.int32),
                 jax.ShapeDtypeStruct((N_EXPERTS, LANES), F32)]
    out_specs = [pl.BlockSpec((tm, D_MODEL), row), pl.BlockSpec((tm, D_MODEL // 2), row), pl.BlockSpec((tm, D_MODEL), row),
                 pl.BlockSpec((TOP_K, tm), col), pl.BlockSpec((TOP_K, tm), col), pl.BlockSpec((TOP_K, tm), col),
                 pl.BlockSpec((N_EXPERTS, LANES), const)]
    return pl.pallas_call(
        functools.partial(_postmix_kernel, nt1=nt1),
        out_shape=out_shape,
        grid=(t_all // tm,),
        in_specs=in_specs,
        out_specs=out_specs,
        scratch_shapes=[pltpu.VMEM((N_EXPERTS, 1), F32),
                        pltpu.VMEM((2 * len(DILATIONS) * hw // LANES, tm, LANES), F32)],
        compiler_params=_cparams(("arbitrary",)),
        name="postmix_router",
    )(x_p, x_s, oa_p, oa_s, *os_, *ls_, ada8, g_oa, g_ob, wo, g_pm, g_pf, rwt, rb, wsg, wsu, wsd)


def _dest_kernel(idx_ref, rank_ref, start_ref, dest_ref):
    tm = idx_ref.shape[1]
    row = lax.broadcasted_iota(jnp.int32, (N_EXPERTS, tm), 0)
    start = start_ref[...]
    base = [jnp.sum(jnp.where(row == idx_ref[k:k + 1, :], start, 0), axis=0, keepdims=True) for k in range(TOP_K)]
    dest_ref[...] = jnp.concatenate(base, axis=0) + rank_ref[...]


def _dest(idx_t, rank_t, pad_start):
    t_all = idx_t.shape[1]
    tm = TM_DEST
    col = lambda i: (0, i)
    return pl.pallas_call(
        _dest_kernel,
        out_shape=jax.ShapeDtypeStruct((TOP_K, t_all), jnp.int32),
        grid=(t_all // tm,),
        in_specs=[pl.BlockSpec((TOP_K, tm), col), pl.BlockSpec((TOP_K, tm), col),
                  pl.BlockSpec((N_EXPERTS, 1), lambda i: (0, 0))],
        out_specs=pl.BlockSpec((TOP_K, tm), col),
        compiler_params=_cparams(("parallel",)),
        name="slot_index",
    )(idx_t, rank_t, pad_start.reshape(N_EXPERTS, 1))


def _sc_mesh():
    return plsc.VectorSubcoreMesh(core_axis_name="core", subcore_axis_name="subcore")


def _dispatch_rows(rows, dest, n_slots):
    t_all, width = rows.shape
    win = SC_WINDOW
    info = plsc.get_sparse_core_info()
    n_workers = info.num_cores * info.num_subcores
    per_worker = t_all // n_workers
    assert per_worker % win == 0

    @functools.partial(
        pl.kernel, out_type=jax.ShapeDtypeStruct((n_slots, width), rows.dtype), mesh=_sc_mesh(),
        scratch_types=[pltpu.VMEM((TOP_K, win), jnp.int32), pltpu.VMEM((win, width), rows.dtype),
                       pltpu.SemaphoreType.DMA])
    def scatter_kernel(x_hbm, i_hbm, o_hbm, idx_v, rows_v, sem):
        base = (lax.axis_index("subcore") * info.num_cores + lax.axis_index("core")) * per_worker

        @pl.loop(0, per_worker // win)
        def _(w):
            off = base + w * win
            for k in range(TOP_K):
                pltpu.sync_copy(i_hbm.at[pl.ds(k * t_all + off, win)], idx_v.at[k])
            pltpu.sync_copy(x_hbm.at[pl.ds(off, win)], rows_v)
            copies = [pltpu.async_copy(rows_v, o_hbm.at[idx_v.at[k]], sem) for k in range(TOP_K)]
            for c in copies:
                c.wait()

    return scatter_kernel(rows, dest)


def _gather_rows(slots, index):
    n = index.shape[0]
    width = slots.shape[1]
    win = SC_WINDOW
    info = plsc.get_sparse_core_info()
    n_workers = info.num_cores * info.num_subcores
    per_worker = n // n_workers
    assert per_worker % win == 0

    @functools.partial(
        pl.kernel, out_type=jax.ShapeDtypeStruct((n, width), slots.dtype), mesh=_sc_mesh(),
        scratch_types=[pltpu.VMEM((win,), jnp.int32), pltpu.VMEM((win, width), slots.dtype),
                       pltpu.SemaphoreType.DMA])
    def gather_kernel(y_hbm, i_hbm, o_hbm, idx_v, rows_v, sem):
        base = (lax.axis_index("subcore") * info.num_cores + lax.axis_index("core")) * per_worker

        @pl.loop(0, per_worker // win)
        def _(w):
            off = base + w * win
            pltpu.sync_copy(i_hbm.at[pl.ds(off, win)], idx_v)
            pltpu.async_copy(y_hbm.at[idx_v], rows_v, sem).wait()
            pltpu.sync_copy(rows_v, o_hbm.at[pl.ds(off, win)])

    return gather_kernel(slots, index)


def _expert_kernel(be_ref, bv_ref, nx_ref, nu_ref, x_hbm, wg_hbm, wu_hbm, wd_hbm, y_hbm,
                   x_buf, y_buf, wg_f, wu_f, wd_f, wgu_s, wd_s, sems, x_sems, y_sems):
    i = pl.program_id(0)
    n_used = nu_ref[0]
    used = i < n_used
    expert = be_ref[i]
    new_expert = (i == 0) | (expert != be_ref[jnp.maximum(i - 1, 0)])
    blk = x_buf.shape[1]
    slot = lax.rem(i, RING)

    def x_copy(b):
        s = lax.rem(b, RING)
        return pltpu.make_async_copy(x_hbm.at[pl.ds(pl.multiple_of(b * blk, blk), blk)], x_buf.at[s], x_sems.at[s])

    def y_copy(b):
        s = lax.rem(b, RING)
        return pltpu.make_async_copy(y_buf.at[s], y_hbm.at[pl.ds(pl.multiple_of(b * blk, blk), blk)], y_sems.at[s])

    def weight_copies(e):
        return (pltpu.make_async_copy(wg_hbm.at[e], wg_f, sems.at[0]),
                pltpu.make_async_copy(wu_hbm.at[e], wu_f, sems.at[1]),
                pltpu.make_async_copy(wd_hbm.at[e], wd_f, sems.at[2]))

    @pl.when(used & (i == 0))
    def _():
        for c in weight_copies(expert):
            c.start()
        for b in range(RING - 1):
            @pl.when(b < n_used)
            def _():
                x_copy(b).start()

    @pl.when(i + RING - 1 < n_used)
    def _():
        x_copy(i + RING - 1).start()

    @pl.when(used & new_expert)
    def _():
        for c in weight_copies(expert):
            c.wait()
        wgu_s[:, 0:D_EXPERT] = wg_f[...].astype(BF16)
        wgu_s[:, D_EXPERT:2 * D_EXPERT] = wu_f[...].astype(BF16)
        wd_s[...] = wd_f[...].astype(BF16)
        nxt = nx_ref[i]

        @pl.when(nxt >= 0)
        def _():
            for c in weight_copies(nxt):
                c.start()

    @pl.when(used)
    def _():
        x_copy(i).wait()

        @pl.when(i >= RING)
        def _():
            y_copy(i - RING).wait()

        def ffn(rows):
            live = lax.broadcasted_iota(jnp.int32, (rows, D_MODEL // 2), 0) < bv_ref[i]
            xa, xb = _unpack_rows(jnp.where(live, x_buf[slot, 0:rows, :], 0))
            half = D_MODEL // 2
            gu = _dot(xa.astype(BF16), wgu_s[0:half, :]) + _dot(xb.astype(BF16), wgu_s[half:D_MODEL, :])
            g, u = gu[:, 0:D_EXPERT], gu[:, D_EXPERT:2 * D_EXPERT]
            y_buf[slot, 0:rows, :] = _pack_rows(_dot((g * _sigmoid(g) * u).astype(BF16), wd_s[...]))

        @pl.when(bv_ref[i] > blk // 2)
        def _():
            ffn(blk)

        @pl.when(bv_ref[i] <= blk // 2)
        def _():
            ffn(blk // 2)
            y_buf[slot, blk // 2:blk, :] = jnp.zeros((blk // 2, D_MODEL // 2), jnp.int32)

        y_copy(i).start()

    @pl.when(i == n_used - 1)
    def _():
        for back in range(RING):
            @pl.when(i - back >= 0)
            def _():
                y_copy(i - back).wait()


def _experts(block_expert, block_valid, block_next, n_used, x_slots, wg, wu, wd):
    p_rows = x_slots.shape[0]
    blk = EXPERT_BLOCK
    hbm = pl.BlockSpec(memory_space=pl.ANY)
    grid_spec = pltpu.PrefetchScalarGridSpec(
        num_scalar_prefetch=4,
        grid=(p_rows // blk,),
        in_specs=[hbm, hbm, hbm, hbm],
        out_specs=hbm,
        scratch_shapes=[pltpu.VMEM((RING, blk, D_MODEL // 2), jnp.int32),
                        pltpu.VMEM((RING, blk, D_MODEL // 2), jnp.int32),
                        pltpu.VMEM((D_MODEL, D_EXPERT), F32), pltpu.VMEM((D_MODEL, D_EXPERT), F32),
                        pltpu.VMEM((D_EXPERT, D_MODEL), F32),
                        pltpu.VMEM((D_MODEL, 2 * D_EXPERT), BF16), pltpu.VMEM((D_EXPERT, D_MODEL), BF16),
                        pltpu.SemaphoreType.DMA((3,)), pltpu.SemaphoreType.DMA((RING,)),
                        pltpu.SemaphoreType.DMA((RING,))])
    return pl.pallas_call(
        _expert_kernel,
        out_shape=jax.ShapeDtypeStruct((p_rows, D_MODEL // 2), jnp.int32),
        grid_spec=grid_spec,
        compiler_params=_cparams(("arbitrary",)),
        name="expert_ffn",
    )(block_expert, block_valid, block_next, n_used, x_slots, wg, wu, wd)


def _combine_kernel(yg_ref, gate_ref, sh_ref, x1_ref, ada_ref, g_ref, y_ref):
    gates = gate_ref[...]
    half = D_MODEL // 2
    acc_a = sh_ref[:, 0:half]
    acc_b = sh_ref[:, half:D_MODEL]
    for k in range(TOP_K):
        ya, yb = _unpack_rows(yg_ref[k])
        acc_a = acc_a + gates[:, k:k + 1] * ya
        acc_b = acc_b + gates[:, k:k + 1] * yb
    gt2 = ada_ref[0, 5:6, :]
    y_ref[...] = x1_ref[...] + gt2 * _rms(jnp.concatenate([acc_a, acc_b], axis=1), g_ref[...])


def _combine(yg, gates_t, sh, x1, ada8, chunk, g_post, row_off):
    n_rows = yg.shape[1]
    tm = TM_COMB
    off = row_off // tm
    row = lambda i: (off + i, 0)
    return pl.pallas_call(
        _combine_kernel,
        out_shape=jax.ShapeDtypeStruct((n_rows, D_MODEL), F32),
        grid=(n_rows // tm,),
        in_specs=[pl.BlockSpec((TOP_K, tm, D_MODEL // 2), lambda i: (0, i, 0)),
                  pl.BlockSpec((tm, TOP_K), row),
                  pl.BlockSpec((tm, D_MODEL), row),
                  pl.BlockSpec((tm, D_MODEL), row),
                  pl.BlockSpec((1, 6, D_MODEL), lambda i: ((off + i) * tm // chunk, 0, 0)),
                  pl.BlockSpec((1, D_MODEL), lambda i: (0, 0))],
        out_specs=pl.BlockSpec((tm, D_MODEL), lambda i: (i, 0)),
        compiler_params=_cparams(("parallel",)),
        name="moe_combine",
    )(yg, gates_t, sh, x1, ada8, g_post)


def _rope_partner(w):
    half = QK_ROPE // 2
    return jnp.concatenate([-w[..., half:], w[..., :half]], axis=-1)


def _prep_weights(w_in, w_uq, w_ukv):
    d = w_in.shape[0]
    zeros = lambda r, c: jnp.zeros((r, c), F32)
    kr = w_in[:, 384:416]
    r_main = jnp.concatenate([zeros(d, QK_NOPE), kr, zeros(d, HEAD_PAD - QK_NOPE - QK_ROPE)], axis=1)
    r_part = jnp.concatenate([zeros(d, QK_NOPE), _rope_partner(kr), zeros(d, HEAD_PAD - QK_NOPE - QK_ROPE)], axis=1)
    win2 = jnp.concatenate([w_in[:, :384], r_main, r_part, w_in[:, 416:]], axis=1).astype(BF16)

    wq = w_uq.reshape(Q_LORA, N_HEADS, QK_NOPE + QK_ROPE)
    zq = jnp.zeros((Q_LORA, N_HEADS, HEAD_PAD - QK_NOPE - QK_ROPE), F32)
    q_main = jnp.concatenate([wq, zq], axis=-1).reshape(Q_LORA, N_HEADS * HEAD_PAD)
    q_part = jnp.concatenate([jnp.zeros((Q_LORA, N_HEADS, QK_NOPE), F32), _rope_partner(wq[..., QK_NOPE:]), zq],
                             axis=-1).reshape(Q_LORA, N_HEADS * HEAD_PAD)
    wuq2 = jnp.concatenate([q_main, q_part], axis=1).astype(BF16)

    wkv = w_ukv.reshape(KV_LORA, N_HEADS, QK_NOPE + HEAD_DIM)
    k_pad = jnp.concatenate([wkv[..., :QK_NOPE], jnp.zeros((KV_LORA, N_HEADS, HEAD_PAD - QK_NOPE), F32)],
                            axis=-1).reshape(KV_LORA, N_HEADS * HEAD_PAD)
    v_cols = wkv[..., QK_NOPE:].reshape(KV_LORA, N_HEADS * HEAD_DIM)
    wukv2 = jnp.concatenate([k_pad, v_cols], axis=1).astype(BF16)
    return win2, wuq2, wukv2


def _rope_table(s_max):
    half = QK_ROPE // 2
    inv = ROPE_BASE ** (-jnp.arange(half, dtype=F32) / half)
    ang = jnp.arange(s_max, dtype=F32)[:, None] * inv[None, :]
    cos = jnp.concatenate([jnp.cos(ang), jnp.cos(ang)], axis=1)
    sin = jnp.concatenate([jnp.sin(ang), jnp.sin(ang)], axis=1)
    scale = (QK_NOPE + QK_ROPE) ** -0.5 * math.log2(math.e)
    pad = jnp.zeros((s_max, HEAD_PAD - QK_NOPE - QK_ROPE), F32)
    zn = jnp.zeros((s_max, QK_NOPE), F32)
    cosq = jnp.concatenate([jnp.full((s_max, QK_NOPE), scale, F32), cos * scale, pad], axis=1)
    sinq = jnp.concatenate([zn, sin * scale, pad], axis=1)
    cosk = jnp.concatenate([zn, cos, pad], axis=1)
    sink = jnp.concatenate([zn, sin, pad], axis=1)
    return jnp.concatenate([cosq, sinq, cosk, sink], axis=1)


def _layer(x_prompt, x_sample, c_prompt, c_sample, w_ada, b_ada, g_pre_mix, w_in, g_qa, w_uq, g_kva, w_ukv,
           g_out_a, g_out_b, w_o, g_post_mix, g_pre_ffn, router_w, router_bias, w_exp_gate, w_exp_up, w_exp_down,
           w_sh_gate, w_sh_up, w_sh_down, g_post_ffn):
    b1, s1, d = x_prompt.shape
    b2, s2, _ = x_sample.shape
    n1, n2 = b1 * s1, b2 * s2
    t_all = n1 + n2
    chunk = math.gcd(s1, s2)
    assert n1 % s2 == 0 and chunk % max(TM_PRE, TM_POST, TM_COMB) == 0

    x_p, x_s = x_prompt.reshape(n1, d), x_sample.reshape(n2, d)
    c_all = jnp.concatenate([c_prompt, c_sample, jnp.zeros((8 - b1 - b2, d), F32)], axis=0)
    ada = _ada(c_all, w_ada, b_ada)
    chunk_batch = jnp.concatenate([jnp.repeat(jnp.arange(b1), s1 // chunk),
                                   b1 + jnp.repeat(jnp.arange(b2), s2 // chunk)])
    ada8 = ada.reshape(8, 6, d)[chunk_batch]

    win2, wuq2, wukv2 = _prep_weights(w_in, w_uq, w_ukv)
    tab = _rope_table(max(s1, s2))
    r2 = lambda g: g.reshape(1, -1)
    qa, ka, va, *dil_qkv = _premix(x_p, x_s, ada8, chunk, r2(g_pre_mix), win2, r2(g_qa), wuq2, r2(g_kva), wukv2, tab,
                                   n1, s1, s2)

    oa_p, oa_s = _mla(qa, ka, va, 0, b1, s1), _mla(qa, ka, va, n1, b2, s2)
    os_, ls_ = [], []
    for p, dil in enumerate(DILATIONS):
        o, lse = _dilated(*dil_qkv[3 * p:3 * p + 3], dil, n1, s1, s2)
        os_.append(o)
        ls_.append(lse)

    x1, h2p, sh, idx_t, gate_t, rank_t, cnt = _postmix(
        x_p, x_s, oa_p, oa_s, os_, ls_, ada8, chunk, r2(g_out_a), r2(g_out_b), w_o.astype(BF16), r2(g_post_mix), r2(g_pre_ffn),
        router_w.T, router_bias.reshape(N_EXPERTS, 1), w_sh_gate.astype(BF16), w_sh_up.astype(BF16),
        w_sh_down.astype(BF16))

    blk = EXPERT_BLOCK
    n_assign = t_all * TOP_K
    n_blocks = -(-(n_assign + N_EXPERTS * (blk - 1)) // blk)
    counts = cnt[:, 0].astype(jnp.int32)
    padded = (counts + blk - 1) // blk * blk
    pad_end = jnp.cumsum(padded)
    pad_start = pad_end - padded
    dest = _dest(idx_t, rank_t, pad_start)
    block_first = jnp.arange(n_blocks, dtype=jnp.int32) * blk
    block_expert = jnp.minimum(jnp.sum((pad_end[None, :] <= block_first[:, None]).astype(jnp.int32), axis=1),
                               N_EXPERTS - 1)
    n_used = (pad_end[-1] // blk).astype(jnp.int32).reshape(1)
    block_valid = jnp.clip(counts[block_expert] - (block_first - pad_start[block_expert]), 0, blk)
    eid = jnp.arange(N_EXPERTS, dtype=jnp.int32)
    used_from = lax.cummin(jnp.where(counts > 0, eid, N_EXPERTS), axis=0, reverse=True)
    next_used = jnp.concatenate([used_from[1:], jnp.full((1,), N_EXPERTS, jnp.int32)])
    block_next = jnp.where(next_used < N_EXPERTS, next_used, -1)[block_expert]
    dest_flat = dest.reshape(n_assign)
    x_slots = _dispatch_rows(h2p, dest_flat, n_blocks * blk)
    y_slots = _experts(block_expert, block_valid, block_next, n_used, x_slots, w_exp_gate, w_exp_up, w_exp_down)
    outs = []
    for off, n_g in ((0, n1), (n1, n2)):
        yg = _gather_rows(y_slots, dest[:, off:off + n_g].reshape(TOP_K * n_g)).reshape(TOP_K, n_g, d // 2)
        outs.append(_combine(yg, gate_t.T, sh, x1, ada8, chunk, r2(g_post_ffn), off))
    return outs[0].reshape(b1, s1, d), outs[1].reshape(b2, s2, d)


def kernel(x_prompt, x_sample, c_prompt, c_sample, w_ada, b_ada, g_pre_mix, w_in, g_qa, w_uq, g_kva, w_ukv, g_out_a, g_out_b, w_o, g_post_mix, g_pre_ffn, router_w, router_bias, w_exp_gate, w_exp_up, w_exp_down, w_sh_gate, w_sh_up, w_sh_down, g_post_ffn):
    layer = [p[0] for p in (w_ada, b_ada, g_pre_mix, w_in, g_qa, w_uq, g_kva, w_ukv, g_out_a, g_out_b, w_o,
                            g_post_mix, g_pre_ffn, router_w, router_bias, w_exp_gate, w_exp_up, w_exp_down,
                            w_sh_gate, w_sh_up, w_sh_down, g_post_ffn)]
    return _layer(x_prompt, x_sample, c_prompt, c_sample, *layer)
```

```python
import functools
import math

import jax
import jax.numpy as jnp
from jax import lax
from jax.experimental import pallas as pl
from jax.experimental.pallas import tpu as pltpu
from jax.experimental.pallas import tpu_sc as plsc

F32 = jnp.float32
BF16 = jnp.bfloat16

D_MODEL = 1024
HEAD_DIM = 64
N_HEADS = 8
Q_LORA = 256
KV_LORA = 128
QK_NOPE = 64
QK_ROPE = 32
ROPE_BASE = 10000.0
DIL_PATTERNS = ((128, 1), (512, 4), (2048, 16))
DILATIONS = tuple(d for _, d in DIL_PATTERNS)
DIL_HALF = 64
assert all(w // (2 * d) == DIL_HALF for w, d in DIL_PATTERNS) and DILATIONS[0] == 1
N_EXPERTS = 256
TOP_K = 8
N_GROUPS = 8
GROUP_SIZE = N_EXPERTS // N_GROUPS
TOPK_GROUPS = 4
D_EXPERT = 256
ROUTE_SCALE = 2.5
EPS = 1e-6
NEG = -1e30

LANES = 128
HEAD_PAD = 128

TM_PRE = 512
TM_POST = 512
TM_COMB = 512
TQ_MLA = 256
TK_MLA = 512
TM_DEST = 2048
SC_WINDOW = 128
TQ_DIL = 256
DIL_SUB = 128
EXPERT_BLOCK = 512
RING = 3
VMEM_LIMIT = 56 * 1024 * 1024


def _cparams(sem):
    return pltpu.CompilerParams(dimension_semantics=sem, vmem_limit_bytes=VMEM_LIMIT)


def _rms(x, g):
    return x * lax.rsqrt(jnp.mean(x * x, axis=-1, keepdims=True) + EPS) * g


def _sigmoid(x):
    return 1.0 / (1.0 + jnp.exp(-x))


def _dot(a, b):
    return jnp.dot(a, b, preferred_element_type=F32)


def _group_specs(tm, width, nt1):
    return [pl.BlockSpec((tm, width), lambda i: (jnp.minimum(i, nt1 - 1), 0)),
            pl.BlockSpec((tm, width), lambda i: (jnp.maximum(i - nt1, 0), 0))]


def _group_tile(first_ref, second_ref, nt1):
    return jnp.where(pl.program_id(0) < nt1, first_ref[...], second_ref[...])


def _pack_rows(x):
    n = x.shape[1] // 2
    bits = lax.bitcast_convert_type(x.astype(BF16).astype(F32), jnp.int32)
    return bits[:, :n] | lax.shift_right_logical(bits[:, n:], 16)


def _unpack_rows(u):
    hi = lax.bitcast_convert_type(u & jnp.int32(-65536), F32)
    lo = lax.bitcast_convert_type(lax.shift_left(u, 16), F32)
    return hi, lo


def _dot_nt(a, b, precision=None):
    return lax.dot_general(a, b, (((1,), (1,)), ((), ())), preferred_element_type=F32, precision=precision)


def _ada_kernel(c_ref, w_ref, b_ref, o_ref):
    c = c_ref[...]
    s = c * _sigmoid(c)
    o_ref[...] = jnp.dot(s, w_ref[...], preferred_element_type=F32, precision=lax.Precision.HIGHEST) + b_ref[...]


def _ada(c_all, w_ada, b_ada):
    nb, d = c_all.shape
    n_out = w_ada.shape[1]
    tn = 1024
    return pl.pallas_call(
        _ada_kernel,
        out_shape=jax.ShapeDtypeStruct((nb, n_out), F32),
        grid=(n_out // tn,),
        in_specs=[pl.BlockSpec((nb, d), lambda j: (0, 0)),
                  pl.BlockSpec((d, tn), lambda j: (0, j)),
                  pl.BlockSpec((1, tn), lambda j: (0, j))],
        out_specs=pl.BlockSpec((nb, tn), lambda j: (0, j)),
        compiler_params=_cparams(("arbitrary",)),
        name="ada",
    )(c_all, w_ada, b_ada.reshape(1, n_out))


def _premix_kernel(xp_ref, xs_ref, ada_ref, g_ref, win_ref, gqa_ref, wuq_ref, gkva_ref, wukv_ref, tab_ref,
                   qa_ref, ka_ref, va_ref, *rest, nt1):
    dil_refs, zs_ref = rest[:-1], rest[-1]
    x = _group_tile(xp_ref, xs_ref, nt1)
    sh1 = ada_ref[0, 0:1, :]
    sc1 = ada_ref[0, 1:2, :]
    h = _rms(x, g_ref[...]) * (1.0 + sc1) + sh1
    z = _dot(h.astype(BF16), win_ref[...])
    tab = tab_ref[...]
    cosq, sinq = tab[:, 0:128], tab[:, 128:256]
    cosk, sink = tab[:, 256:384], tab[:, 384:512]
    nh = N_HEADS * HEAD_PAD

    cq = _rms(z[:, 0:Q_LORA], gqa_ref[...]).astype(BF16)
    qq = _dot(cq, wuq_ref[...])
    for h_i in range(N_HEADS):
        lo = h_i * HEAD_PAD
        qa_ref[:, lo:lo + HEAD_PAD] = (qq[:, lo:lo + HEAD_PAD] * cosq
                                       + qq[:, nh + lo:nh + lo + HEAD_PAD] * sinq).astype(BF16)

    ckv = _rms(z[:, 256:384], gkva_ref[...]).astype(BF16)
    kk = _dot(ckv, wukv_ref[...])
    rr = z[:, 384:512] * cosk + z[:, 512:640] * sink
    for h_i in range(N_HEADS):
        lo = h_i * HEAD_PAD
        ka_ref[:, lo:lo + HEAD_PAD] = (kk[:, lo:lo + HEAD_PAD] + rr).astype(BF16)
    low = lax.broadcasted_iota(jnp.int32, (x.shape[0], LANES), 1) < HEAD_DIM
    for j in range(N_HEADS // 2):
        vpair = kk[:, nh + j * LANES:nh + (j + 1) * LANES]
        va_ref[:, (2 * j) * LANES:(2 * j + 1) * LANES] = jnp.where(low, vpair, 1.0).astype(BF16)
        va_ref[:, (2 * j + 1) * LANES:(2 * j + 2) * LANES] = jnp.where(low, 1.0, vpair).astype(BF16)

    tm = x.shape[0]
    wd = N_HEADS * HEAD_DIM
    n_slab = 3 * wd // LANES
    for c in range(n_slab):
        scale = HEAD_DIM ** -0.5 * math.log2(math.e) if c < wd // LANES else 1.0
        zs_ref[c] = z[:, 640 + c * LANES:640 + (c + 1) * LANES] * scale
    for dil, refs in zip(DILATIONS, (dil_refs[0:3], dil_refs[3:6], dil_refs[6:9])):
        n = tm // dil
        for r in range(dil):
            for c in range(n_slab):
                rows = zs_ref[c] if dil == 1 else zs_ref.at[c][pl.ds(r, n, stride=dil), :]
                col = r * wd + (c % (wd // LANES)) * LANES
                refs[c // (wd // LANES)][:, col:col + LANES] = rows.astype(BF16)


def _premix(x_p, x_s, ada8, chunk, g_pre, win2, g_qa, wuq2, g_kva, wukv2, tab, n1, s1, s2):
    t_all = x_p.shape[0] + x_s.shape[0]
    tm = TM_PRE
    nt1 = n1 // tm
    tb1, tb2 = s1 // tm, s2 // tm

    def tab_map(i):
        return (jnp.where(i < nt1, i % tb1, (i - nt1) % tb2), 0)

    row = lambda i: (i, 0)
    const = lambda i: (0, 0)
    wd = N_HEADS * HEAD_DIM
    outs = [jax.ShapeDtypeStruct((t_all, N_HEADS * HEAD_PAD), BF16)] * 3
    out_specs = [pl.BlockSpec((tm, N_HEADS * HEAD_PAD), row)] * 3
    for dil in DILATIONS:
        outs += [jax.ShapeDtypeStruct((t_all // dil, dil * wd), BF16)] * 3
        out_specs += [pl.BlockSpec((tm // dil, dil * wd), row)] * 3
    return pl.pallas_call(
        functools.partial(_premix_kernel, nt1=nt1),
        out_shape=outs,
        grid=(t_all // tm,),
        in_specs=_group_specs(tm, D_MODEL, nt1) + [
                  pl.BlockSpec((1, 6, D_MODEL), lambda i: (i * tm // chunk, 0, 0)),
                  pl.BlockSpec((1, D_MODEL), const),
                  pl.BlockSpec(win2.shape, const),
                  pl.BlockSpec((1, Q_LORA), const),
                  pl.BlockSpec(wuq2.shape, const),
                  pl.BlockSpec((1, KV_LORA), const),
                  pl.BlockSpec(wukv2.shape, const),
                  pl.BlockSpec((tm, 512), tab_map)],
        out_specs=out_specs,
        scratch_shapes=[pltpu.VMEM((3 * wd // LANES, tm, LANES), F32)],
        compiler_params=_cparams(("parallel",)),
        name="premix",
    )(x_p, x_s, ada8, g_pre, win2, g_qa, wuq2, g_kva, wukv2, tab)


def _mla_kernel(q_ref, k_ref, v_ref, o_ref, s_ref, m_ref, *, tk):
    tq, seq = s_ref.shape[1], s_ref.shape[2]
    lane = lax.broadcasted_iota(jnp.int32, (tq, LANES), 1)

    def scores(hh):
        s = _dot_nt(q_ref[:, hh * HEAD_PAD:(hh + 1) * HEAD_PAD], k_ref[:, hh * HEAD_PAD:(hh + 1) * HEAD_PAD])
        s_ref[hh] = s
        m_ref[hh] = jnp.max(s, axis=-1, keepdims=True)

    def values(hh):
        acc = jnp.zeros((tq, LANES), F32)
        m = m_ref[hh]
        for c0 in range(0, seq, tk):
            p = jnp.exp2(s_ref[hh, :, c0:c0 + tk] - m)
            acc = acc + _dot(p.astype(BF16), v_ref[c0:c0 + tk, hh * LANES:(hh + 1) * LANES])
        return acc / pltpu.roll(acc, HEAD_DIM, axis=1)

    scores(0)
    scores(1)
    o0 = values(0)
    o1 = values(1)
    o_ref[...] = jnp.where(lane < HEAD_DIM, o0, o1).astype(o_ref.dtype)


def _mla(qa, ka, va, row_off, batch, seq):
    tq = TQ_MLA
    nq = seq // tq
    qoff = row_off // tq
    soff = row_off // seq
    return pl.pallas_call(
        functools.partial(_mla_kernel, tk=TK_MLA),
        out_shape=jax.ShapeDtypeStruct((batch * seq, N_HEADS * HEAD_DIM), BF16),
        grid=(batch, N_HEADS // 2, nq),
        in_specs=[pl.BlockSpec((tq, 2 * HEAD_PAD), lambda b, j, qi: (qoff + b * nq + qi, j)),
                  pl.BlockSpec((seq, 2 * HEAD_PAD), lambda b, j, qi: (soff + b, j)),
                  pl.BlockSpec((seq, 2 * LANES), lambda b, j, qi: (soff + b, j))],
        out_specs=pl.BlockSpec((tq, LANES), lambda b, j, qi: (b * nq + qi, j)),
        scratch_shapes=[pltpu.VMEM((2, tq, seq), F32), pltpu.VMEM((2, tq, 1), F32)],
        compiler_params=_cparams(("parallel", "parallel", "parallel")),
        name="mla_attention",
    )(qa, ka, va)


def _dil_kernel(q_ref, kp_ref, kc_ref, kn_ref, vp_ref, vc_ref, vn_ref, bias_ref, o_ref, lse_ref, *,
                n1_rows, len1, len2):
    tq = q_ref.shape[0]
    sub = bias_ref.shape[2]
    w = bias_ref.shape[3]
    row0 = pl.program_id(0) * tq
    in1 = row0 < n1_rows
    seq_len = jnp.where(in1, len1, len2)
    q0 = jnp.where(in1, row0 % len1, (row0 - n1_rows) % len2)

    kw = jnp.concatenate([kp_ref[...], kc_ref[...], kn_ref[...]], axis=0)
    vw = jnp.concatenate([vp_ref[...], vc_ref[...], vn_ref[...]], axis=0)
    low = lax.broadcasted_iota(jnp.int32, (sub, LANES), 1) < HEAD_DIM
    low_w = lax.broadcasted_iota(jnp.int32, (w, LANES), 1) < HEAD_DIM
    chains = [(t, h) for t in range(tq // sub) for h in range(N_HEADS)]
    lanes = lambda h: slice((h // 2) * LANES, (h // 2 + 1) * LANES)
    variant = [(q0 + t * sub == 0).astype(jnp.int32) + 2 * (q0 + (t + 1) * sub == seq_len).astype(jnp.int32)
               for t in range(tq // sub)]
    qs = [q_ref[t * sub:(t + 1) * sub, lanes(h)] for t, h in chains]
    qs = [jnp.where(low, q, 0) if h % 2 == 0 else jnp.where(low, 0, q) for q, (t, h) in zip(qs, chains)]
    vs = [vw[t * sub:t * sub + w, lanes(h)] for t, h in chains]
    vs = [jnp.where(low_w, v, 1) if h % 2 == 0 else jnp.where(low_w, 1, v) for v, (t, h) in zip(vs, chains)]
    ss = [_dot_nt(q.astype(BF16), kw[t * sub:t * sub + w, lanes(h)]) + bias_ref[variant[t], h]
          for q, (t, h) in zip(qs, chains)]
    ms = [jnp.max(s, axis=-1, keepdims=True) for s in ss]
    ps = [jnp.exp2(s - m).astype(BF16) for s, m in zip(ss, ms)]
    accs = [_dot(p, v.astype(BF16)) for p, v in zip(ps, vs)]
    ls = [pltpu.roll(acc, HEAD_DIM, axis=1) for acc in accs]
    outs = [acc / l for acc, l in zip(accs, ls)]
    lses = [m * math.log(2.0) + jnp.log(l) for m, l in zip(ms, ls)]
    for c in range(0, len(chains), 2):
        t, h = chains[c]
        o_ref[t * sub:(t + 1) * sub, lanes(h)] = jnp.where(low, outs[c], outs[c + 1]).astype(o_ref.dtype)
        lse_ref[t * sub:(t + 1) * sub, lanes(h)] = jnp.where(low, lses[c], lses[c + 1])


def _dil_bias(dil, tq):
    half = DIL_HALF
    col = jnp.arange(tq + 2 * half)[None, :]
    dist = jnp.abs(col - half - jnp.arange(tq)[:, None])
    slopes = 2.0 ** (-8.0 * jnp.arange(1, N_HEADS + 1, dtype=F32) / N_HEADS)
    alibi = -math.log2(math.e) * slopes[:, None, None] * (dil * dist).astype(F32)[None]
    band = dist <= half
    first, last = col >= half, col < tq + half
    masks = [band, band & first, band & last, band & first & last]
    return jnp.stack([jnp.where(mk[None], alibi, NEG) for mk in masks])


def _dilated(qv, kv, vv, dil, n1, s1, s2):
    rows = qv.shape[0]
    tq = TQ_DIL
    nt = rows // tq
    wk = N_HEADS * HEAD_DIM
    cur = lambda i, r: (i, r)
    per = tq // DIL_HALF
    prv = lambda i, r: (jnp.maximum(i * per - 1, 0), r)
    nxt = lambda i, r: (jnp.minimum((i + 1) * per, nt * per - 1), r)
    kspecs = [pl.BlockSpec((DIL_HALF, wk), prv), pl.BlockSpec((tq, wk), cur), pl.BlockSpec((DIL_HALF, wk), nxt)]
    bias = _dil_bias(dil, DIL_SUB)
    assert tq % DIL_SUB == 0 and (min(s1, s2) // dil) % tq == 0
    return pl.pallas_call(
        functools.partial(_dil_kernel, n1_rows=n1 // dil, len1=s1 // dil, len2=s2 // dil),
        out_shape=[jax.ShapeDtypeStruct((rows, dil * wk), BF16), jax.ShapeDtypeStruct((rows, dil * wk), F32)],
        grid=(nt, dil),
        in_specs=[pl.BlockSpec((tq, wk), cur)] + kspecs + kspecs
                 + [pl.BlockSpec(bias.shape, lambda i, r: (0, 0, 0, 0))],
        out_specs=[pl.BlockSpec((tq, wk), cur), pl.BlockSpec((tq, wk), cur)],
        compiler_params=_cparams(("parallel", "parallel")),
        name="dilated_attention_d%d" % dil,
    )(qv, kv, kv, kv, vv, vv, vv, bias)


def _postmix_kernel(xp_ref, xs_ref, oap_ref, oas_ref, o1_ref, o2_ref, o3_ref, l1_ref, l2_ref, l3_ref, ada_ref, goa_ref, gob_ref,
                    wo_ref, gpm_ref, gpf_ref, rwt_ref, rb_ref, wsg_ref, wsu_ref, wsd_ref,
                    x1_ref, h2_ref, sh_ref, idx_ref, gate_ref, rank_ref, cnt_ref, carry_ref, nat_ref, *, nt1):
    tm = xp_ref.shape[0]

    @pl.when(pl.program_id(0) == 0)
    def _():
        carry_ref[...] = jnp.zeros_like(carry_ref)

    gt1, sh2, sc2 = ada_ref[0, 2:3, :], ada_ref[0, 3:4, :], ada_ref[0, 4:5, :]

    wd = N_HEADS * HEAD_DIM
    n_slab = wd // LANES

    def row_order(src_ref, dil, base):
        if dil == 1:
            return src_ref[...].astype(F32)
        n = tm // dil
        for r in range(dil):
            for c in range(n_slab):
                col = r * wd + c * LANES
                nat_ref.at[base + c][pl.ds(r, n, stride=dil), :] = src_ref[:, col:col + LANES].astype(F32)
        return jnp.concatenate([nat_ref[base + c] for c in range(n_slab)], axis=1)

    o_pat = [row_order(ref, dil, (2 * p) * n_slab) for p, (ref, dil) in enumerate(zip((o1_ref, o2_ref, o3_ref), DILATIONS))]
    la, lb, lc = [row_order(ref, dil, (2 * p + 1) * n_slab)
                  for p, (ref, dil) in enumerate(zip((l1_ref, l2_ref, l3_ref), DILATIONS))]

    mx = jnp.maximum(jnp.maximum(la, lb), lc)
    ea, eb, ec = jnp.exp(la - mx), jnp.exp(lb - mx), jnp.exp(lc - mx)
    ob = (ea * o_pat[0] + eb * o_pat[1] + ec * o_pat[2]) / (ea + eb + ec)

    na = _rms(_group_tile(oap_ref, oas_ref, nt1).astype(F32), goa_ref[...]).astype(BF16)
    nb = _rms(ob, gob_ref[...]).astype(BF16)
    half_w = N_HEADS * HEAD_DIM
    o = _dot(na, wo_ref[0:half_w, :]) + _dot(nb, wo_ref[half_w:2 * half_w, :])
    x1 = _group_tile(xp_ref, xs_ref, nt1) + gt1 * _rms(o, gpm_ref[...])
    x1_ref[...] = x1
    h2 = _rms(x1, gpf_ref[...]) * (1.0 + sc2) + sh2
    h2b = h2.astype(BF16)
    h2_ref[...] = _pack_rows(h2)

    g = _dot(h2b, wsg_ref[...])
    u = _dot(h2b, wsu_ref[...])
    sh_ref[...] = _dot((g * _sigmoid(g) * u).astype(BF16), wsd_ref[...])

    scores = _sigmoid(_dot_nt(rwt_ref[...], h2, precision=lax.Precision.HIGHEST))
    biased = scores + rb_ref[...]
    ninf = -jnp.inf
    row = lax.broadcasted_iota(jnp.int32, (N_EXPERTS, tm), 0)
    rwg = lax.broadcasted_iota(jnp.int32, (GROUP_SIZE, tm), 0)
    gsc = []
    for gi in range(N_GROUPS):
        blk = biased[gi * GROUP_SIZE:(gi + 1) * GROUP_SIZE]
        m1 = jnp.max(blk, axis=0, keepdims=True)
        i1 = jnp.min(jnp.where(blk == m1, rwg, N_EXPERTS), axis=0, keepdims=True)
        m2 = jnp.max(jnp.where(rwg == i1, ninf, blk), axis=0, keepdims=True)
        gsc.append(m1 + m2)
    gsel = [jnp.zeros((1, tm), F32) for _ in range(N_GROUPS)]
    for _ in range(TOPK_GROUPS):
        m = functools.reduce(jnp.maximum, gsc)
        free = jnp.ones((1, tm), F32)
        for gi in range(N_GROUPS):
            hit = jnp.where(gsc[gi] == m, free, 0.0)
            free = free - hit
            gsel[gi] = gsel[gi] + hit
            gsc[gi] = jnp.where(hit > 0.0, ninf, gsc[gi])
    masked = jnp.concatenate(
        [jnp.where(gsel[gi] > 0.0, biased[gi * GROUP_SIZE:(gi + 1) * GROUP_SIZE], ninf)
         for gi in range(N_GROUPS)], axis=0)
    cur = masked
    idxs, gts = [], []
    for _ in range(TOP_K):
        m = jnp.max(cur, axis=0, keepdims=True)
        ik = jnp.min(jnp.where(cur == m, row, N_EXPERTS), axis=0, keepdims=True)
        hit = row == ik
        idxs.append(ik)
        gts.append(jnp.sum(jnp.where(hit, scores, 0.0), axis=0, keepdims=True))
        cur = jnp.where(hit, ninf, cur)
    idx = jnp.concatenate(idxs, axis=0)
    gates = jnp.concatenate(gts, axis=0)
    idx_ref[...] = idx
    gate_ref[...] = gates / jnp.sum(gates, axis=0, keepdims=True) * ROUTE_SCALE

    sel = jnp.where(cur != masked, 1.0, 0.0)
    tr = lax.broadcasted_iota(jnp.int32, (tm, tm), 0)
    tc = lax.broadcasted_iota(jnp.int32, (tm, tm), 1)
    before = jnp.where(tr < tc, 1.0, 0.0).astype(BF16)
    pos = _dot(sel.astype(BF16), before) + carry_ref[...]
    rank_ref[...] = jnp.concatenate(
        [jnp.sum(jnp.where(row == idxs[k], pos, 0.0), axis=0, keepdims=True) for k in range(TOP_K)],
        axis=0).astype(jnp.int32)
    carry = carry_ref[...] + jnp.sum(sel, axis=1, keepdims=True)
    carry_ref[...] = carry
    cnt_ref[...] = jnp.broadcast_to(carry, cnt_ref.shape)


def _postmix(x_p, x_s, oa_p, oa_s, os_, ls_, ada8, chunk, g_oa, g_ob, wo, g_pm, g_pf, rwt, rb, wsg, wsu, wsd):
    t_all = x_p.shape[0] + x_s.shape[0]
    tm = TM_POST
    nt1 = x_p.shape[0] // tm
    row = lambda i: (i, 0)
    col = lambda i: (0, i)
    const = lambda i: (0, 0)
    hw = N_HEADS * HEAD_DIM
    pat_specs = [pl.BlockSpec((tm // dil, dil * hw), row) for dil in DILATIONS]
    in_specs = (_group_specs(tm, D_MODEL, nt1) + _group_specs(tm, hw, nt1)
                + pat_specs + pat_specs
                + [pl.BlockSpec((1, 6, D_MODEL), lambda i: (i * tm // chunk, 0, 0)),
                   pl.BlockSpec((1, hw), const), pl.BlockSpec((1, hw), const),
                   pl.BlockSpec(wo.shape, const),
                   pl.BlockSpec((1, D_MODEL), const), pl.BlockSpec((1, D_MODEL), const),
                   pl.BlockSpec(rwt.shape, const), pl.BlockSpec((N_EXPERTS, 1), const),
                   pl.BlockSpec(wsg.shape, const), pl.BlockSpec(wsu.shape, const), pl.BlockSpec(wsd.shape, const)])
    out_shape = [jax.ShapeDtypeStruct((t_all, D_MODEL), F32),
                 jax.ShapeDtypeStruct((t_all, D_MODEL // 2), jnp.int32),
                 jax.ShapeDtypeStruct((t_all, D_MODEL), F32),
                 jax.ShapeDtypeStruct((TOP_K, t_all), jnp.int32),
                 jax.ShapeDtypeStruct((TOP_K, t_all), F32),
                 jax.ShapeDtypeStruct((TOP_K, t_all), jnp.int32),
                 jax.ShapeDtypeStruct((N_EXPERTS, LANES), F32)]
    out_specs = [pl.BlockSpec((tm, D_MODEL), row), pl.BlockSpec((tm, D_MODEL // 2), row), pl.BlockSpec((tm, D_MODEL), row),
                 pl.BlockSpec((TOP_K, tm), col), pl.BlockSpec((TOP_K, tm), col), pl.BlockSpec((TOP_K, tm), col),
                 pl.BlockSpec((N_EXPERTS, LANES), const)]
    return pl.pallas_call(
        functools.partial(_postmix_kernel, nt1=nt1),
        out_shape=out_shape,
        grid=(t_all // tm,),
        in_specs=in_specs,
        out_specs=out_specs,
        scratch_shapes=[pltpu.VMEM((N_EXPERTS, 1), F32),
                        pltpu.VMEM((2 * len(DILATIONS) * hw // LANES, tm, LANES), F32)],
        compiler_params=_cparams(("arbitrary",)),
        name="postmix_router",
    )(x_p, x_s, oa_p, oa_s, *os_, *ls_, ada8, g_oa, g_ob, wo, g_pm, g_pf, rwt, rb, wsg, wsu, wsd)


def _dest_kernel(idx_ref, rank_ref, start_ref, dest_ref):
    tm = idx_ref.shape[1]
    row = lax.broadcasted_iota(jnp.int32, (N_EXPERTS, tm), 0)
    start = start_ref[...]
    base = [jnp.sum(jnp.where(row == idx_ref[k:k + 1, :], start, 0), axis=0, keepdims=True) for k in range(TOP_K)]
    dest_ref[...] = jnp.concatenate(base, axis=0) + rank_ref[...]


def _dest(idx_t, rank_t, pad_start):
    t_all = idx_t.shape[1]
    tm = TM_DEST
    col = lambda i: (0, i)
    return pl.pallas_call(
        _dest_kernel,
        out_shape=jax.ShapeDtypeStruct((TOP_K, t_all), jnp.int32),
        grid=(t_all // tm,),
        in_specs=[pl.BlockSpec((TOP_K, tm), col), pl.BlockSpec((TOP_K, tm), col),
                  pl.BlockSpec((N_EXPERTS, 1), lambda i: (0, 0))],
        out_specs=pl.BlockSpec((TOP_K, tm), col),
        compiler_params=_cparams(("parallel",)),
        name="slot_index",
    )(idx_t, rank_t, pad_start.reshape(N_EXPERTS, 1))


def _sc_mesh():
    return plsc.VectorSubcoreMesh(core_axis_name="core", subcore_axis_name="subcore")


def _dispatch_rows(rows, dest, n_slots):
    t_all, width = rows.shape
    win = SC_WINDOW
    info = plsc.get_sparse_core_info()
    n_workers = info.num_cores * info.num_subcores
    per_worker = t_all // n_workers
    assert per_worker % win == 0

    @functools.partial(
        pl.kernel, out_type=jax.ShapeDtypeStruct((n_slots, width), rows.dtype), mesh=_sc_mesh(),
        scratch_types=[pltpu.VMEM((TOP_K, win), jnp.int32), pltpu.VMEM((win, width), rows.dtype),
                       pltpu.SemaphoreType.DMA])
    def scatter_kernel(x_hbm, i_hbm, o_hbm, idx_v, rows_v, sem):
        base = (lax.axis_index("subcore") * info.num_cores + lax.axis_index("core")) * per_worker

        @pl.loop(0, per_worker // win)
        def _(w):
            off = base + w * win
            for k in range(TOP_K):
                pltpu.sync_copy(i_hbm.at[pl.ds(k * t_all + off, win)], idx_v.at[k])
            pltpu.sync_copy(x_hbm.at[pl.ds(off, win)], rows_v)
            copies = [pltpu.async_copy(rows_v, o_hbm.at[idx_v.at[k]], sem) for k in range(TOP_K)]
            for c in copies:
                c.wait()

    return scatter_kernel(rows, dest)


def _gather_rows(slots, index):
    n = index.shape[0]
    width = slots.shape[1]
    win = SC_WINDOW
    info = plsc.get_sparse_core_info()
    n_workers = info.num_cores * info.num_subcores
    per_worker = n // n_workers
    assert per_worker % win == 0

    @functools.partial(
        pl.kernel, out_type=jax.ShapeDtypeStruct((n, width), slots.dtype), mesh=_sc_mesh(),
        scratch_types=[pltpu.VMEM((win,), jnp.int32), pltpu.VMEM((win, width), slots.dtype),
                       pltpu.SemaphoreType.DMA])
    def gather_kernel(y_hbm, i_hbm, o_hbm, idx_v, rows_v, sem):
        base = (lax.axis_index("subcore") * info.num_cores + lax.axis_index("core")) * per_worker

        @pl.loop(0, per_worker // win)
        def _(w):
            off = base + w * win
            pltpu.sync_copy(i_hbm.at[pl.ds(off, win)], idx_v)
            pltpu.async_copy(y_hbm.at[idx_v], rows_v, sem).wait()
            pltpu.sync_copy(rows_v, o_hbm.at[pl.ds(off, win)])

    return gather_kernel(slots, index)


def _expert_kernel(be_ref, bv_ref, nx_ref, nu_ref, x_hbm, wg_hbm, wu_hbm, wd_hbm, y_hbm,
                   x_buf, y_buf, wg_f, wu_f, wd_f, wgu_s, wd_s, sems, x_sems, y_sems):
    i = pl.program_id(0)
    n_used = nu_ref[0]
    used = i < n_used
    expert = be_ref[i]
    new_expert = (i == 0) | (expert != be_ref[jnp.maximum(i - 1, 0)])
    blk = x_buf.shape[1]
    slot = lax.rem(i, RING)

    def x_copy(b):
        s = lax.rem(b, RING)
        return pltpu.make_async_copy(x_hbm.at[pl.ds(pl.multiple_of(b * blk, blk), blk)], x_buf.at[s], x_sems.at[s])

    def y_copy(b):
        s = lax.rem(b, RING)
        return pltpu.make_async_copy(y_buf.at[s], y_hbm.at[pl.ds(pl.multiple_of(b * blk, blk), blk)], y_sems.at[s])

    def weight_copies(e):
        return (pltpu.make_async_copy(wg_hbm.at[e], wg_f, sems.at[0]),
                pltpu.make_async_copy(wu_hbm.at[e], wu_f, sems.at[1]),
                pltpu.make_async_copy(wd_hbm.at[e], wd_f, sems.at[2]))

    @pl.when(used & (i == 0))
    def _():
        for c in weight_copies(expert):
            c.start()
        for b in range(RING - 1):
            @pl.when(b < n_used)
            def _():
                x_copy(b).start()

    @pl.when(i + RING - 1 < n_used)
    def _():
        x_copy(i + RING - 1).start()

    @pl.when(used & new_expert)
    def _():
        for c in weight_copies(expert):
            c.wait()
        wgu_s[:, 0:D_EXPERT] = wg_f[...].astype(BF16)
        wgu_s[:, D_EXPERT:2 * D_EXPERT] = wu_f[...].astype(BF16)
        wd_s[...] = wd_f[...].astype(BF16)
        nxt = nx_ref[i]

        @pl.when(nxt >= 0)
        def _():
            for c in weight_copies(nxt):
                c.start()

    @pl.when(used)
    def _():
        x_copy(i).wait()

        @pl.when(i >= RING)
        def _():
            y_copy(i - RING).wait()

        def ffn(rows):
            live = lax.broadcasted_iota(jnp.int32, (rows, D_MODEL // 2), 0) < bv_ref[i]
            xa, xb = _unpack_rows(jnp.where(live, x_buf[slot, 0:rows, :], 0))
            half = D_MODEL // 2
            gu = _dot(xa.astype(BF16), wgu_s[0:half, :]) + _dot(xb.astype(BF16), wgu_s[half:D_MODEL, :])
            g, u = gu[:, 0:D_EXPERT], gu[:, D_EXPERT:2 * D_EXPERT]
            y_buf[slot, 0:rows, :] = _pack_rows(_dot((g * _sigmoid(g) * u).astype(BF16), wd_s[...]))

        @pl.when(bv_ref[i] > blk // 2)
        def _():
            ffn(blk)

        @pl.when(bv_ref[i] <= blk // 2)
        def _():
            ffn(blk // 2)
            y_buf[slot, blk // 2:blk, :] = jnp.zeros((blk // 2, D_MODEL // 2), jnp.int32)

        y_copy(i).start()

    @pl.when(i == n_used - 1)
    def _():
        for back in range(RING):
            @pl.when(i - back >= 0)
            def _():
                y_copy(i - back).wait()


def _experts(block_expert, block_valid, block_next, n_used, x_slots, wg, wu, wd):
    p_rows = x_slots.shape[0]
    blk = EXPERT_BLOCK
    hbm = pl.BlockSpec(memory_space=pl.ANY)
    grid_spec = pltpu.PrefetchScalarGridSpec(
        num_scalar_prefetch=4,
        grid=(p_rows // blk,),
        in_specs=[hbm, hbm, hbm, hbm],
        out_specs=hbm,
        scratch_shapes=[pltpu.VMEM((RING, blk, D_MODEL // 2), jnp.int32),
                        pltpu.VMEM((RING, blk, D_MODEL // 2), jnp.int32),
                        pltpu.VMEM((D_MODEL, D_EXPERT), F32), pltpu.VMEM((D_MODEL, D_EXPERT), F32),
                        pltpu.VMEM((D_EXPERT, D_MODEL), F32),
                        pltpu.VMEM((D_MODEL, 2 * D_EXPERT), BF16), pltpu.VMEM((D_EXPERT, D_MODEL), BF16),
                        pltpu.SemaphoreType.DMA((3,)), pltpu.SemaphoreType.DMA((RING,)),
                        pltpu.SemaphoreType.DMA((RING,))])
    return pl.pallas_call(
        _expert_kernel,
        out_shape=jax.ShapeDtypeStruct((p_rows, D_MODEL // 2), jnp.int32),
        grid_spec=grid_spec,
        compiler_params=_cparams(("arbitrary",)),
        name="expert_ffn",
    )(block_expert, block_valid, block_next, n_used, x_slots, wg, wu, wd)


def _combine_kernel(yg_ref, gate_ref, sh_ref, x1_ref, ada_ref, g_ref, y_ref):
    gates = gate_ref[...]
    half = D_MODEL // 2
    acc_a = sh_ref[:, 0:half]
    acc_b = sh_ref[:, half:D_MODEL]
    for k in range(TOP_K):
        ya, yb = _unpack_rows(yg_ref[k])
        acc_a = acc_a + gates[:, k:k + 1] * ya
        acc_b = acc_b + gates[:, k:k + 1] * yb
    gt2 = ada_ref[0, 5:6, :]
    y_ref[...] = x1_ref[...] + gt2 * _rms(jnp.concatenate([acc_a, acc_b], axis=1), g_ref[...])


def _combine(yg, gates_t, sh, x1, ada8, chunk, g_post, row_off):
    n_rows = yg.shape[1]
    tm = TM_COMB
    off = row_off // tm
    row = lambda i: (off + i, 0)
    return pl.pallas_call(
        _combine_kernel,
        out_shape=jax.ShapeDtypeStruct((n_rows, D_MODEL), F32),
        grid=(n_rows // tm,),
        in_specs=[pl.BlockSpec((TOP_K, tm, D_MODEL // 2), lambda i: (0, i, 0)),
                  pl.BlockSpec((tm, TOP_K), row),
                  pl.BlockSpec((tm, D_MODEL), row),
                  pl.BlockSpec((tm, D_MODEL), row),
                  pl.BlockSpec((1, 6, D_MODEL), lambda i: ((off + i) * tm // chunk, 0, 0)),
                  pl.BlockSpec((1, D_MODEL), lambda i: (0, 0))],
        out_specs=pl.BlockSpec((tm, D_MODEL), lambda i: (i, 0)),
        compiler_params=_cparams(("parallel",)),
        name="moe_combine",
    )(yg, gates_t, sh, x1, ada8, g_post)


def _rope_partner(w):
    half = QK_ROPE // 2
    return jnp.concatenate([-w[..., half:], w[..., :half]], axis=-1)


def _prep_weights(w_in, w_uq, w_ukv):
    d = w_in.shape[0]
    zeros = lambda r, c: jnp.zeros((r, c), F32)
    kr = w_in[:, 384:416]
    r_main = jnp.concatenate([zeros(d, QK_NOPE), kr, zeros(d, HEAD_PAD - QK_NOPE - QK_ROPE)], axis=1)
    r_part = jnp.concatenate([zeros(d, QK_NOPE), _rope_partner(kr), zeros(d, HEAD_PAD - QK_NOPE - QK_ROPE)], axis=1)
    win2 = jnp.concatenate([w_in[:, :384], r_main, r_part, w_in[:, 416:]], axis=1).astype(BF16)

    wq = w_uq.reshape(Q_LORA, N_HEADS, QK_NOPE + QK_ROPE)
    zq = jnp.zeros((Q_LORA, N_HEADS, HEAD_PAD - QK_NOPE - QK_ROPE), F32)
    q_main = jnp.concatenate([wq, zq], axis=-1).reshape(Q_LORA, N_HEADS * HEAD_PAD)
    q_part = jnp.concatenate([jnp.zeros((Q_LORA, N_HEADS, QK_NOPE), F32), _rope_partner(wq[..., QK_NOPE:]), zq],
                             axis=-1).reshape(Q_LORA, N_HEADS * HEAD_PAD)
    wuq2 = jnp.concatenate([q_main, q_part], axis=1).astype(BF16)

    wkv = w_ukv.reshape(KV_LORA, N_HEADS, QK_NOPE + HEAD_DIM)
    k_pad = jnp.concatenate([wkv[..., :QK_NOPE], jnp.zeros((KV_LORA, N_HEADS, HEAD_PAD - QK_NOPE), F32)],
                            axis=-1).reshape(KV_LORA, N_HEADS * HEAD_PAD)
    v_cols = wkv[..., QK_NOPE:].reshape(KV_LORA, N_HEADS * HEAD_DIM)
    wukv2 = jnp.concatenate([k_pad, v_cols], axis=1).astype(BF16)
    return win2, wuq2, wukv2


def _rope_table(s_max):
    half = QK_ROPE // 2
    inv = ROPE_BASE ** (-jnp.arange(half, dtype=F32) / half)
    ang = jnp.arange(s_max, dtype=F32)[:, None] * inv[None, :]
    cos = jnp.concatenate([jnp.cos(ang), jnp.cos(ang)], axis=1)
    sin = jnp.concatenate([jnp.sin(ang), jnp.sin(ang)], axis=1)
    scale = (QK_NOPE + QK_ROPE) ** -0.5 * math.log2(math.e)
    pad = jnp.zeros((s_max, HEAD_PAD - QK_NOPE - QK_ROPE), F32)
    zn = jnp.zeros((s_max, QK_NOPE), F32)
    cosq = jnp.concatenate([jnp.full((s_max, QK_NOPE), scale, F32), cos * scale, pad], axis=1)
    sinq = jnp.concatenate([zn, sin * scale, pad], axis=1)
    cosk = jnp.concatenate([zn, cos, pad], axis=1)
    sink = jnp.concatenate([zn, sin, pad], axis=1)
    return jnp.concatenate([cosq, sinq, cosk, sink], axis=1)


def _layer(x_prompt, x_sample, c_prompt, c_sample, w_ada, b_ada, g_pre_mix, w_in, g_qa, w_uq, g_kva, w_ukv,
           g_out_a, g_out_b, w_o, g_post_mix, g_pre_ffn, router_w, router_bias, w_exp_gate, w_exp_up, w_exp_down,
           w_sh_gate, w_sh_up, w_sh_down, g_post_ffn):
    b1, s1, d = x_prompt.shape
    b2, s2, _ = x_sample.shape
    n1, n2 = b1 * s1, b2 * s2
    t_all = n1 + n2
    chunk = math.gcd(s1, s2)
    assert n1 % s2 == 0 and chunk % max(TM_PRE, TM_POST, TM_COMB) == 0

    x_p, x_s = x_prompt.reshape(n1, d), x_sample.reshape(n2, d)
    c_all = jnp.concatenate([c_prompt, c_sample, jnp.zeros((8 - b1 - b2, d), F32)], axis=0)
    ada = _ada(c_all, w_ada, b_ada)
    chunk_batch = jnp.concatenate([jnp.repeat(jnp.arange(b1), s1 // chunk),
                                   b1 + jnp.repeat(jnp.arange(b2), s2 // chunk)])
    ada8 = ada.reshape(8, 6, d)[chunk_batch]

    win2, wuq2, wukv2 = _prep_weights(w_in, w_uq, w_ukv)
    tab = _rope_table(max(s1, s2))
    r2 = lambda g: g.reshape(1, -1)
    qa, ka, va, *dil_qkv = _premix(x_p, x_s, ada8, chunk, r2(g_pre_mix), win2, r2(g_qa), wuq2, r2(g_kva), wukv2, tab,
                                   n1, s1, s2)

    oa_p, oa_s = _mla(qa, ka, va, 0, b1, s1), _mla(qa, ka, va, n1, b2, s2)
    os_, ls_ = [], []
    for p, dil in enumerate(DILATIONS):
        o, lse = _dilated(*dil_qkv[3 * p:3 * p + 3], dil, n1, s1, s2)
        os_.append(o)
        ls_.append(lse)

    x1, h2p, sh, idx_t, gate_t, rank_t, cnt = _postmix(
        x_p, x_s, oa_p, oa_s, os_, ls_, ada8, chunk, r2(g_out_a), r2(g_out_b), w_o.astype(BF16), r2(g_post_mix), r2(g_pre_ffn),
        router_w.T, router_bias.reshape(N_EXPERTS, 1), w_sh_gate.astype(BF16), w_sh_up.astype(BF16),
        w_sh_down.astype(BF16))

    blk = EXPERT_BLOCK
    n_assign = t_all * TOP_K
    n_blocks = -(-(n_assign + N_EXPERTS * (blk - 1)) // blk)
    counts = cnt[:, 0].astype(jnp.int32)
    padded = (counts + blk - 1) // blk * blk
    pad_end = jnp.cumsum(padded)
    pad_start = pad_end - padded
    dest = _dest(idx_t, rank_t, pad_start)
    block_first = jnp.arange(n_blocks, dtype=jnp.int32) * blk
    block_expert = jnp.minimum(jnp.sum((pad_end[None, :] <= block_first[:, None]).astype(jnp.int32), axis=1),
                               N_EXPERTS - 1)
    n_used = (pad_end[-1] // blk).astype(jnp.int32).reshape(1)
    block_valid = jnp.clip(counts[block_expert] - (block_first - pad_start[block_expert]), 0, blk)
    eid = jnp.arange(N_EXPERTS, dtype=jnp.int32)
    used_from = lax.cummin(jnp.where(counts > 0, eid, N_EXPERTS), axis=0, reverse=True)
    next_used = jnp.concatenate([used_from[1:], jnp.full((1,), N_EXPERTS, jnp.int32)])
    block_next = jnp.where(next_used < N_EXPERTS, next_used, -1)[block_expert]
    dest_flat = dest.reshape(n_assign)
    x_slots = _dispatch_rows(h2p, dest_flat, n_blocks * blk)
    y_slots = _experts(block_expert, block_valid, block_next, n_used, x_slots, w_exp_gate, w_exp_up, w_exp_down)
    outs = []
    for off, n_g in ((0, n1), (n1, n2)):
        yg = _gather_rows(y_slots, dest[:, off:off + n_g].reshape(TOP_K * n_g)).reshape(TOP_K, n_g, d // 2)
        outs.append(_combine(yg, gate_t.T, sh, x1, ada8, chunk, r2(g_post_ffn), off))
    return outs[0].reshape(b1, s1, d), outs[1].reshape(b2, s2, d)


def kernel(x_prompt, x_sample, c_prompt, c_sample, w_ada, b_ada, g_pre_mix, w_in, g_qa, w_uq, g_kva, w_ukv, g_out_a, g_out_b, w_o, g_post_mix, g_pre_ffn, router_w, router_bias, w_exp_gate, w_exp_up, w_exp_down, w_sh_gate, w_sh_up, w_sh_down, g_post_ffn):
    layer = [p[0] for p in (w_ada, b_ada, g_pre_mix, w_in, g_qa, w_uq, g_kva, w_ukv, g_out_a, g_out_b, w_o,
                            g_post_mix, g_pre_ffn, router_w, router_bias, w_exp_gate, w_exp_up, w_exp_down,
                            w_sh_gate, w_sh_up, w_sh_down, g_post_ffn)]
    return _layer(x_prompt, x_sample, c_prompt, c_sample, *layer)
```

```python
import functools
import math

import jax
import jax.numpy as jnp
from jax import lax
from jax.experimental import pallas as pl
from jax.experimental.pallas import tpu as pltpu
from jax.experimental.pallas import tpu_sc as plsc

F32 = jnp.float32
BF16 = jnp.bfloat16

D_MODEL = 1024
HEAD_DIM = 64
N_HEADS = 8
Q_LORA = 256
KV_LORA = 128
QK_NOPE = 64
QK_ROPE = 32
ROPE_BASE = 10000.0
DIL_PATTERNS = ((128, 1), (512, 4), (2048, 16))
DILATIONS = tuple(d for _, d in DIL_PATTERNS)
DIL_HALF = 64
assert all(w // (2 * d) == DIL_HALF for w, d in DIL_PATTERNS) and DILATIONS[0] == 1
N_EXPERTS = 256
TOP_K = 8
N_GROUPS = 8
GROUP_SIZE = N_EXPERTS // N_GROUPS
TOPK_GROUPS = 4
D_EXPERT = 256
ROUTE_SCALE = 2.5
EPS = 1e-6
NEG = -1e30

LANES = 128
HEAD_PAD = 128

TM_PRE = 512
TM_POST = 512
TM_COMB = 512
TQ_MLA = 256
TK_MLA = 512
TM_DEST = 2048
SC_WINDOW = 128
TQ_DIL = 512
DIL_SUB = 128
EXPERT_BLOCK = 512
RING = 3
VMEM_LIMIT = 56 * 1024 * 1024


def _cparams(sem):
    return pltpu.CompilerParams(dimension_semantics=sem, vmem_limit_bytes=VMEM_LIMIT)


def _rms(x, g):
    return x * lax.rsqrt(jnp.mean(x * x, axis=-1, keepdims=True) + EPS) * g


def _sigmoid(x):
    return 1.0 / (1.0 + jnp.exp(-x))


def _dot(a, b):
    return jnp.dot(a, b, preferred_element_type=F32)


def _group_specs(tm, width, nt1):
    return [pl.BlockSpec((tm, width), lambda i: (jnp.minimum(i, nt1 - 1), 0)),
            pl.BlockSpec((tm, width), lambda i: (jnp.maximum(i - nt1, 0), 0))]


def _group_tile(first_ref, second_ref, nt1):
    return jnp.where(pl.program_id(0) < nt1, first_ref[...], second_ref[...])


def _pack_rows(x):
    n = x.shape[1] // 2
    bits = lax.bitcast_convert_type(x.astype(BF16).astype(F32), jnp.int32)
    return bits[:, :n] | lax.shift_right_logical(bits[:, n:], 16)


def _unpack_rows(u):
    hi = lax.bitcast_convert_type(u & jnp.int32(-65536), F32)
    lo = lax.bitcast_convert_type(lax.shift_left(u, 16), F32)
    return hi, lo


def _dot_nt(a, b, precision=None):
    return lax.dot_general(a, b, (((1,), (1,)), ((), ())), preferred_element_type=F32, precision=precision)


def _ada_kernel(c_ref, w_ref, b_ref, o_ref):
    c = c_ref[...]
    s = c * _sigmoid(c)
    o_ref[...] = jnp.dot(s, w_ref[...], preferred_element_type=F32, precision=lax.Precision.HIGHEST) + b_ref[...]


def _ada(c_all, w_ada, b_ada):
    nb, d = c_all.shape
    n_out = w_ada.shape[1]
    tn = 1024
    return pl.pallas_call(
        _ada_kernel,
        out_shape=jax.ShapeDtypeStruct((nb, n_out), F32),
        grid=(n_out // tn,),
        in_specs=[pl.BlockSpec((nb, d), lambda j: (0, 0)),
                  pl.BlockSpec((d, tn), lambda j: (0, j)),
                  pl.BlockSpec((1, tn), lambda j: (0, j))],
        out_specs=pl.BlockSpec((nb, tn), lambda j: (0, j)),
        compiler_params=_cparams(("arbitrary",)),
        name="ada",
    )(c_all, w_ada, b_ada.reshape(1, n_out))


def _premix_kernel(xp_ref, xs_ref, ada_ref, g_ref, win_ref, gqa_ref, wuq_ref, gkva_ref, wukv_ref, tab_ref,
                   qa_ref, ka_ref, va_ref, *rest, nt1):
    dil_refs, zs_ref = rest[:-1], rest[-1]
    x = _group_tile(xp_ref, xs_ref, nt1)
    sh1 = ada_ref[0, 0:1, :]
    sc1 = ada_ref[0, 1:2, :]
    h = _rms(x, g_ref[...]) * (1.0 + sc1) + sh1
    z = _dot(h.astype(BF16), win_ref[...])
    tab = tab_ref[...]
    cosq, sinq = tab[:, 0:128], tab[:, 128:256]
    cosk, sink = tab[:, 256:384], tab[:, 384:512]
    nh = N_HEADS * HEAD_PAD

    cq = _rms(z[:, 0:Q_LORA], gqa_ref[...]).astype(BF16)
    qq = _dot(cq, wuq_ref[...])
    for h_i in range(N_HEADS):
        lo = h_i * HEAD_PAD
        qa_ref[:, lo:lo + HEAD_PAD] = (qq[:, lo:lo + HEAD_PAD] * cosq
                                       + qq[:, nh + lo:nh + lo + HEAD_PAD] * sinq).astype(BF16)

    ckv = _rms(z[:, 256:384], gkva_ref[...]).astype(BF16)
    kk = _dot(ckv, wukv_ref[...])
    rr = z[:, 384:512] * cosk + z[:, 512:640] * sink
    for h_i in range(N_HEADS):
        lo = h_i * HEAD_PAD
        ka_ref[:, lo:lo + HEAD_PAD] = (kk[:, lo:lo + HEAD_PAD] + rr).astype(BF16)
    low = lax.broadcasted_iota(jnp.int32, (x.shape[0], LANES), 1) < HEAD_DIM
    for j in range(N_HEADS // 2):
        vpair = kk[:, nh + j * LANES:nh + (j + 1) * LANES]
        va_ref[:, (2 * j) * LANES:(2 * j + 1) * LANES] = jnp.where(low, vpair, 1.0).astype(BF16)
        va_ref[:, (2 * j + 1) * LANES:(2 * j + 2) * LANES] = jnp.where(low, 1.0, vpair).astype(BF16)

    tm = x.shape[0]
    wd = N_HEADS * HEAD_DIM
    n_slab = 3 * wd // LANES
    for c in range(n_slab):
        scale = HEAD_DIM ** -0.5 * math.log2(math.e) if c < wd // LANES else 1.0
        zs_ref[c] = z[:, 640 + c * LANES:640 + (c + 1) * LANES] * scale
    for dil, refs in zip(DILATIONS, (dil_refs[0:3], dil_refs[3:6], dil_refs[6:9])):
        n = tm // dil
        for r in range(dil):
            for c in range(n_slab):
                rows = zs_ref[c] if dil == 1 else zs_ref.at[c][pl.ds(r, n, stride=dil), :]
                col = r * wd + (c % (wd // LANES)) * LANES
                refs[c // (wd // LANES)][:, col:col + LANES] = rows.astype(BF16)


def _premix(x_p, x_s, ada8, chunk, g_pre, win2, g_qa, wuq2, g_kva, wukv2, tab, n1, s1, s2):
    t_all = x_p.shape[0] + x_s.shape[0]
    tm = TM_PRE
    nt1 = n1 // tm
    tb1, tb2 = s1 // tm, s2 // tm

    def tab_map(i):
        return (jnp.where(i < nt1, i % tb1, (i - nt1) % tb2), 0)

    row = lambda i: (i, 0)
    const = lambda i: (0, 0)
    wd = N_HEADS * HEAD_DIM
    outs = [jax.ShapeDtypeStruct((t_all, N_HEADS * HEAD_PAD), BF16)] * 3
    out_specs = [pl.BlockSpec((tm, N_HEADS * HEAD_PAD), row)] * 3
    for dil in DILATIONS:
        outs += [jax.ShapeDtypeStruct((t_all // dil, dil * wd), BF16)] * 3
        out_specs += [pl.BlockSpec((tm // dil, dil * wd), row)] * 3
    return pl.pallas_call(
        functools.partial(_premix_kernel, nt1=nt1),
        out_shape=outs,
        grid=(t_all // tm,),
        in_specs=_group_specs(tm, D_MODEL, nt1) + [
                  pl.BlockSpec((1, 6, D_MODEL), lambda i: (i * tm // chunk, 0, 0)),
                  pl.BlockSpec((1, D_MODEL), const),
                  pl.BlockSpec(win2.shape, const),
                  pl.BlockSpec((1, Q_LORA), const),
                  pl.BlockSpec(wuq2.shape, const),
                  pl.BlockSpec((1, KV_LORA), const),
                  pl.BlockSpec(wukv2.shape, const),
                  pl.BlockSpec((tm, 512), tab_map)],
        out_specs=out_specs,
        scratch_shapes=[pltpu.VMEM((3 * wd // LANES, tm, LANES), F32)],
        compiler_params=_cparams(("parallel",)),
        name="premix",
    )(x_p, x_s, ada8, g_pre, win2, g_qa, wuq2, g_kva, wukv2, tab)


def _mla_kernel(q_ref, k_ref, v_ref, o_ref, s_ref, m_ref, *, tk):
    tq, seq = s_ref.shape[1], s_ref.shape[2]
    lane = lax.broadcasted_iota(jnp.int32, (tq, LANES), 1)

    def scores(hh):
        s = _dot_nt(q_ref[:, hh * HEAD_PAD:(hh + 1) * HEAD_PAD], k_ref[:, hh * HEAD_PAD:(hh + 1) * HEAD_PAD])
        s_ref[hh] = s
        m_ref[hh] = jnp.max(s, axis=-1, keepdims=True)

    def values(hh):
        acc = jnp.zeros((tq, LANES), F32)
        m = m_ref[hh]
        for c0 in range(0, seq, tk):
            p = jnp.exp2(s_ref[hh, :, c0:c0 + tk] - m)
            acc = acc + _dot(p.astype(BF16), v_ref[c0:c0 + tk, hh * LANES:(hh + 1) * LANES])
        return acc / pltpu.roll(acc, HEAD_DIM, axis=1)

    scores(0)
    scores(1)
    o0 = values(0)
    o1 = values(1)
    o_ref[...] = jnp.where(lane < HEAD_DIM, o0, o1).astype(o_ref.dtype)


def _mla(qa, ka, va, row_off, batch, seq):
    tq = TQ_MLA
    nq = seq // tq
    qoff = row_off // tq
    soff = row_off // seq
    return pl.pallas_call(
        functools.partial(_mla_kernel, tk=TK_MLA),
        out_shape=jax.ShapeDtypeStruct((batch * seq, N_HEADS * HEAD_DIM), BF16),
        grid=(batch, N_HEADS // 2, nq),
        in_specs=[pl.BlockSpec((tq, 2 * HEAD_PAD), lambda b, j, qi: (qoff + b * nq + qi, j)),
                  pl.BlockSpec((seq, 2 * HEAD_PAD), lambda b, j, qi: (soff + b, j)),
                  pl.BlockSpec((seq, 2 * LANES), lambda b, j, qi: (soff + b, j))],
        out_specs=pl.BlockSpec((tq, LANES), lambda b, j, qi: (b * nq + qi, j)),
        scratch_shapes=[pltpu.VMEM((2, tq, seq), F32), pltpu.VMEM((2, tq, 1), F32)],
        compiler_params=_cparams(("parallel", "parallel", "parallel")),
        name="mla_attention",
    )(qa, ka, va)


def _dil_kernel(q_ref, kp_ref, kc_ref, kn_ref, vp_ref, vc_ref, vn_ref, bias_ref, o_ref, lse_ref, *,
                n1_rows, len1, len2):
    tq = q_ref.shape[0]
    sub = bias_ref.shape[2]
    w = bias_ref.shape[3]
    row0 = pl.program_id(0) * tq
    in1 = row0 < n1_rows
    seq_len = jnp.where(in1, len1, len2)
    q0 = jnp.where(in1, row0 % len1, (row0 - n1_rows) % len2)

    kw = jnp.concatenate([kp_ref[...], kc_ref[...], kn_ref[...]], axis=0)
    vw = jnp.concatenate([vp_ref[...], vc_ref[...], vn_ref[...]], axis=0)
    low = lax.broadcasted_iota(jnp.int32, (sub, LANES), 1) < HEAD_DIM
    low_w = lax.broadcasted_iota(jnp.int32, (w, LANES), 1) < HEAD_DIM
    chains = [(t, h) for t in range(tq // sub) for h in range(N_HEADS)]
    lanes = lambda h: slice((h // 2) * LANES, (h // 2 + 1) * LANES)
    variant = [(q0 + t * sub == 0).astype(jnp.int32) + 2 * (q0 + (t + 1) * sub == seq_len).astype(jnp.int32)
               for t in range(tq // sub)]
    qs = [q_ref[t * sub:(t + 1) * sub, lanes(h)] for t, h in chains]
    qs = [jnp.where(low, q, 0) if h % 2 == 0 else jnp.where(low, 0, q) for q, (t, h) in zip(qs, chains)]
    vs = [vw[t * sub:t * sub + w, lanes(h)] for t, h in chains]
    vs = [jnp.where(low_w, v, 1) if h % 2 == 0 else jnp.where(low_w, 1, v) for v, (t, h) in zip(vs, chains)]
    ss = [_dot_nt(q.astype(BF16), kw[t * sub:t * sub + w, lanes(h)]) + bias_ref[variant[t], h]
          for q, (t, h) in zip(qs, chains)]
    ms = [jnp.max(s, axis=-1, keepdims=True) for s in ss]
    ps = [jnp.exp2(s - m).astype(BF16) for s, m in zip(ss, ms)]
    accs = [_dot(p, v.astype(BF16)) for p, v in zip(ps, vs)]
    ls = [pltpu.roll(acc, HEAD_DIM, axis=1) for acc in accs]
    outs = [acc / l for acc, l in zip(accs, ls)]
    lses = [m * math.log(2.0) + jnp.log(l) for m, l in zip(ms, ls)]
    for c in range(0, len(chains), 2):
        t, h = chains[c]
        o_ref[t * sub:(t + 1) * sub, lanes(h)] = jnp.where(low, outs[c], outs[c + 1]).astype(o_ref.dtype)
        lse_ref[t * sub:(t + 1) * sub, lanes(h)] = jnp.where(low, lses[c], lses[c + 1])


def _dil_bias(dil, tq):
    half = DIL_HALF
    col = jnp.arange(tq + 2 * half)[None, :]
    dist = jnp.abs(col - half - jnp.arange(tq)[:, None])
    slopes = 2.0 ** (-8.0 * jnp.arange(1, N_HEADS + 1, dtype=F32) / N_HEADS)
    alibi = -math.log2(math.e) * slopes[:, None, None] * (dil * dist).astype(F32)[None]
    band = dist <= half
    first, last = col >= half, col < tq + half
    masks = [band, band & first, band & last, band & first & last]
    return jnp.stack([jnp.where(mk[None], alibi, NEG) for mk in masks])


def _dilated(qv, kv, vv, dil, n1, s1, s2):
    rows = qv.shape[0]
    tq = min(TQ_DIL, min(s1, s2) // dil)
    nt = rows // tq
    wk = N_HEADS * HEAD_DIM
    cur = lambda i, r: (i, r)
    per = tq // DIL_HALF
    prv = lambda i, r: (jnp.maximum(i * per - 1, 0), r)
    nxt = lambda i, r: (jnp.minimum((i + 1) * per, nt * per - 1), r)
    kspecs = [pl.BlockSpec((DIL_HALF, wk), prv), pl.BlockSpec((tq, wk), cur), pl.BlockSpec((DIL_HALF, wk), nxt)]
    bias = _dil_bias(dil, DIL_SUB)
    assert tq % DIL_SUB == 0 and (min(s1, s2) // dil) % tq == 0
    return pl.pallas_call(
        functools.partial(_dil_kernel, n1_rows=n1 // dil, len1=s1 // dil, len2=s2 // dil),
        out_shape=[jax.ShapeDtypeStruct((rows, dil * wk), BF16), jax.ShapeDtypeStruct((rows, dil * wk), F32)],
        grid=(nt, dil),
        in_specs=[pl.BlockSpec((tq, wk), cur)] + kspecs + kspecs
                 + [pl.BlockSpec(bias.shape, lambda i, r: (0, 0, 0, 0))],
        out_specs=[pl.BlockSpec((tq, wk), cur), pl.BlockSpec((tq, wk), cur)],
        compiler_params=_cparams(("parallel", "parallel")),
        name="dilated_attention_d%d" % dil,
    )(qv, kv, kv, kv, vv, vv, vv, bias)


def _postmix_kernel(xp_ref, xs_ref, oap_ref, oas_ref, o1_ref, o2_ref, o3_ref, l1_ref, l2_ref, l3_ref, ada_ref, goa_ref, gob_ref,
                    wo_ref, gpm_ref, gpf_ref, rwh_ref, rwl_ref, rb_ref, wsg_ref, wsu_ref, wsd_ref, before_ref,
                    x1_ref, h2_ref, sh_ref, idx_ref, gate_ref, rank_ref, cnt_ref, carry_ref, nat_ref, *, nt1):
    tm = xp_ref.shape[0]

    @pl.when(pl.program_id(0) == 0)
    def _():
        carry_ref[...] = jnp.zeros_like(carry_ref)

    gt1, sh2, sc2 = ada_ref[0, 2:3, :], ada_ref[0, 3:4, :], ada_ref[0, 4:5, :]

    wd = N_HEADS * HEAD_DIM
    n_slab = wd // LANES

    def row_order(src_ref, dil, base):
        if dil == 1:
            return src_ref[...].astype(F32)
        n = tm // dil
        for r in range(dil):
            for c in range(n_slab):
                col = r * wd + c * LANES
                nat_ref.at[base + c][pl.ds(r, n, stride=dil), :] = src_ref[:, col:col + LANES].astype(F32)
        return jnp.concatenate([nat_ref[base + c] for c in range(n_slab)], axis=1)

    o_pat = [row_order(ref, dil, (2 * p) * n_slab) for p, (ref, dil) in enumerate(zip((o1_ref, o2_ref, o3_ref), DILATIONS))]
    la, lb, lc = [row_order(ref, dil, (2 * p + 1) * n_slab)
                  for p, (ref, dil) in enumerate(zip((l1_ref, l2_ref, l3_ref), DILATIONS))]

    mx = jnp.maximum(jnp.maximum(la, lb), lc)
    ea, eb, ec = jnp.exp(la - mx), jnp.exp(lb - mx), jnp.exp(lc - mx)
    ob = (ea * o_pat[0] + eb * o_pat[1] + ec * o_pat[2]) / (ea + eb + ec)

    na = _rms(_group_tile(oap_ref, oas_ref, nt1).astype(F32), goa_ref[...]).astype(BF16)
    nb = _rms(ob, gob_ref[...]).astype(BF16)
    half_w = N_HEADS * HEAD_DIM
    o = _dot(na, wo_ref[0:half_w, :]) + _dot(nb, wo_ref[half_w:2 * half_w, :])
    x1 = _group_tile(xp_ref, xs_ref, nt1) + gt1 * _rms(o, gpm_ref[...])
    x1_ref[...] = x1
    h2 = _rms(x1, gpf_ref[...]) * (1.0 + sc2) + sh2
    h2b = h2.astype(BF16)
    h2_ref[...] = _pack_rows(h2)

    g = _dot(h2b, wsg_ref[...])
    u = _dot(h2b, wsu_ref[...])
    sh_ref[...] = _dot((g * _sigmoid(g) * u).astype(BF16), wsd_ref[...])

    h_lo = (h2 - h2b.astype(F32)).astype(BF16)
    logits = _dot_nt(rwh_ref[...], h2b) + (_dot_nt(rwh_ref[...], h_lo) + _dot_nt(rwl_ref[...], h2b))
    scores = _sigmoid(logits)
    biased = scores + rb_ref[...]
    ninf = -jnp.inf
    row = lax.broadcasted_iota(jnp.int32, (N_EXPERTS, tm), 0)
    rwg = lax.broadcasted_iota(jnp.int32, (GROUP_SIZE, tm), 0)
    gsc = []
    for gi in range(N_GROUPS):
        blk = biased[gi * GROUP_SIZE:(gi + 1) * GROUP_SIZE]
        m1 = jnp.max(blk, axis=0, keepdims=True)
        i1 = jnp.min(jnp.where(blk == m1, rwg, N_EXPERTS), axis=0, keepdims=True)
        m2 = jnp.max(jnp.where(rwg == i1, ninf, blk), axis=0, keepdims=True)
        gsc.append(m1 + m2)
    gsel = [jnp.zeros((1, tm), F32) for _ in range(N_GROUPS)]
    for _ in range(TOPK_GROUPS):
        m = functools.reduce(jnp.maximum, gsc)
        free = jnp.ones((1, tm), F32)
        for gi in range(N_GROUPS):
            hit = jnp.where(gsc[gi] == m, free, 0.0)
            free = free - hit
            gsel[gi] = gsel[gi] + hit
            gsc[gi] = jnp.where(hit > 0.0, ninf, gsc[gi])
    masked = jnp.concatenate(
        [jnp.where(gsel[gi] > 0.0, biased[gi * GROUP_SIZE:(gi + 1) * GROUP_SIZE], ninf)
         for gi in range(N_GROUPS)], axis=0)
    cur = masked
    idxs, gts = [], []
    for _ in range(TOP_K):
        m = jnp.max(cur, axis=0, keepdims=True)
        ik = jnp.min(jnp.where(cur == m, row, N_EXPERTS), axis=0, keepdims=True)
        hit = row == ik
        idxs.append(ik)
        gts.append(jnp.sum(jnp.where(hit, scores, 0.0), axis=0, keepdims=True))
        cur = jnp.where(hit, ninf, cur)
    idx = jnp.concatenate(idxs, axis=0)
    gates = jnp.concatenate(gts, axis=0)
    idx_ref[...] = idx
    gate_ref[...] = gates / jnp.sum(gates, axis=0, keepdims=True) * ROUTE_SCALE

    sel = jnp.where(cur != masked, 1.0, 0.0)
    pos = _dot(sel.astype(BF16), before_ref[...]) + carry_ref[...]
    rank_ref[...] = jnp.concatenate(
        [jnp.sum(jnp.where(row == idxs[k], pos, 0.0), axis=0, keepdims=True) for k in range(TOP_K)],
        axis=0).astype(jnp.int32)
    carry = carry_ref[...] + jnp.sum(sel, axis=1, keepdims=True)
    carry_ref[...] = carry
    cnt_ref[...] = jnp.broadcast_to(carry, cnt_ref.shape)


def _postmix(x_p, x_s, oa_p, oa_s, os_, ls_, ada8, chunk, g_oa, g_ob, wo, g_pm, g_pf, rwt, rb, wsg, wsu, wsd):
    t_all = x_p.shape[0] + x_s.shape[0]
    tm = TM_POST
    nt1 = x_p.shape[0] // tm
    rw_hi = rwt.astype(BF16)
    rw_lo = (rwt - rw_hi.astype(F32)).astype(BF16)
    before = jnp.triu(jnp.ones((tm, tm), BF16), 1)
    row = lambda i: (i, 0)
    col = lambda i: (0, i)
    const = lambda i: (0, 0)
    hw = N_HEADS * HEAD_DIM
    pat_specs = [pl.BlockSpec((tm // dil, dil * hw), row) for dil in DILATIONS]
    in_specs = (_group_specs(tm, D_MODEL, nt1) + _group_specs(tm, hw, nt1)
                + pat_specs + pat_specs
                + [pl.BlockSpec((1, 6, D_MODEL), lambda i: (i * tm // chunk, 0, 0)),
                   pl.BlockSpec((1, hw), const), pl.BlockSpec((1, hw), const),
                   pl.BlockSpec(wo.shape, const),
                   pl.BlockSpec((1, D_MODEL), const), pl.BlockSpec((1, D_MODEL), const),
                   pl.BlockSpec(rwt.shape, const), pl.BlockSpec(rwt.shape, const), pl.BlockSpec((N_EXPERTS, 1), const),
                   pl.BlockSpec(wsg.shape, const), pl.BlockSpec(wsu.shape, const), pl.BlockSpec(wsd.shape, const),
                   pl.BlockSpec((tm, tm), const)])
    out_shape = [jax.ShapeDtypeStruct((t_all, D_MODEL), F32),
                 jax.ShapeDtypeStruct((t_all, D_MODEL // 2), jnp.int32),
                 jax.ShapeDtypeStruct((t_all, D_MODEL), F32),
                 jax.ShapeDtypeStruct((TOP_K, t_all), jnp.int32),
                 jax.ShapeDtypeStruct((TOP_K, t_all), F32),
                 jax.ShapeDtypeStruct((TOP_K, t_all), jnp.int32),
                 jax.ShapeDtypeStruct((N_EXPERTS, LANES), F32)]
    out_specs = [pl.BlockSpec((tm, D_MODEL), row), pl.BlockSpec((tm, D_MODEL // 2), row), pl.BlockSpec((tm, D_MODEL), row),
                 pl.BlockSpec((TOP_K, tm), col), pl.BlockSpec((TOP_K, tm), col), pl.BlockSpec((TOP_K, tm), col),
                 pl.BlockSpec((N_EXPERTS, LANES), const)]
    return pl.pallas_call(
        functools.partial(_postmix_kernel, nt1=nt1),
        out_shape=out_shape,
        grid=(t_all // tm,),
        in_specs=in_specs,
        out_specs=out_specs,
        scratch_shapes=[pltpu.VMEM((N_EXPERTS, 1), F32),
                        pltpu.VMEM((2 * len(DILATIONS) * hw // LANES, tm, LANES), F32)],
        compiler_params=_cparams(("arbitrary",)),
        name="postmix_router",
    )(x_p, x_s, oa_p, oa_s, *os_, *ls_, ada8, g_oa, g_ob, wo, g_pm, g_pf, rw_hi, rw_lo, rb, wsg, wsu, wsd, before)


def _dest_kernel(idx_ref, rank_ref, start_ref, dest_ref):
    tm = idx_ref.shape[1]
    row = lax.broadcasted_iota(jnp.int32, (N_EXPERTS, tm), 0)
    start = start_ref[...]
    base = [jnp.sum(jnp.where(row == idx_ref[k:k + 1, :], start, 0), axis=0, keepdims=True) for k in range(TOP_K)]
    dest_ref[...] = jnp.concatenate(base, axis=0) + rank_ref[...]


def _dest(idx_t, rank_t, pad_start):
    t_all = idx_t.shape[1]
    tm = TM_DEST
    col = lambda i: (0, i)
    return pl.pallas_call(
        _dest_kernel,
        out_shape=jax.ShapeDtypeStruct((TOP_K, t_all), jnp.int32),
        grid=(t_all // tm,),
        in_specs=[pl.BlockSpec((TOP_K, tm), col), pl.BlockSpec((TOP_K, tm), col),
                  pl.BlockSpec((N_EXPERTS, 1), lambda i: (0, 0))],
        out_specs=pl.BlockSpec((TOP_K, tm), col),
        compiler_params=_cparams(("parallel",)),
        name="slot_index",
    )(idx_t, rank_t, pad_start.reshape(N_EXPERTS, 1))


def _sc_mesh():
    return plsc.VectorSubcoreMesh(core_axis_name="core", subcore_axis_name="subcore")


def _dispatch_rows(rows, dest, n_slots):
    t_all, width = rows.shape
    win = SC_WINDOW
    info = plsc.get_sparse_core_info()
    n_workers = info.num_cores * info.num_subcores
    per_worker = t_all // n_workers
    assert per_worker % win == 0

    @functools.partial(
        pl.kernel, out_type=jax.ShapeDtypeStruct((n_slots, width), rows.dtype), mesh=_sc_mesh(),
        scratch_types=[pltpu.VMEM((TOP_K, win), jnp.int32), pltpu.VMEM((win, width), rows.dtype),
                       pltpu.SemaphoreType.DMA])
    def scatter_kernel(x_hbm, i_hbm, o_hbm, idx_v, rows_v, sem):
        base = (lax.axis_index("subcore") * info.num_cores + lax.axis_index("core")) * per_worker

        @pl.loop(0, per_worker // win)
        def _(w):
            off = base + w * win
            for k in range(TOP_K):
                pltpu.sync_copy(i_hbm.at[pl.ds(k * t_all + off, win)], idx_v.at[k])
            pltpu.sync_copy(x_hbm.at[pl.ds(off, win)], rows_v)
            copies = [pltpu.async_copy(rows_v, o_hbm.at[idx_v.at[k]], sem) for k in range(TOP_K)]
            for c in copies:
                c.wait()

    return scatter_kernel(rows, dest)


def _gather_rows(slots, index):
    n = index.shape[0]
    width = slots.shape[1]
    win = SC_WINDOW
    info = plsc.get_sparse_core_info()
    n_workers = info.num_cores * info.num_subcores
    per_worker = n // n_workers
    assert per_worker % win == 0

    @functools.partial(
        pl.kernel, out_type=jax.ShapeDtypeStruct((n, width), slots.dtype), mesh=_sc_mesh(),
        scratch_types=[pltpu.VMEM((win,), jnp.int32), pltpu.VMEM((win, width), slots.dtype),
                       pltpu.SemaphoreType.DMA])
    def gather_kernel(y_hbm, i_hbm, o_hbm, idx_v, rows_v, sem):
        base = (lax.axis_index("subcore") * info.num_cores + lax.axis_index("core")) * per_worker

        @pl.loop(0, per_worker // win)
        def _(w):
            off = base + w * win
            pltpu.sync_copy(i_hbm.at[pl.ds(off, win)], idx_v)
            pltpu.async_copy(y_hbm.at[idx_v], rows_v, sem).wait()
            pltpu.sync_copy(rows_v, o_hbm.at[pl.ds(off, win)])

    return gather_kernel(slots, index)


def _expert_kernel(be_ref, bv_ref, nx_ref, nu_ref, x_hbm, wg_hbm, wu_hbm, wd_hbm, y_hbm,
                   x_buf, y_buf, wg_f, wu_f, wd_f, wgu_s, wd_s, sems, x_sems, y_sems):
    i = pl.program_id(0)
    n_used = nu_ref[0]
    used = i < n_used
    expert = be_ref[i]
    new_expert = (i == 0) | (expert != be_ref[jnp.maximum(i - 1, 0)])
    blk = x_buf.shape[1]
    slot = lax.rem(i, RING)

    def x_copy(b):
        s = lax.rem(b, RING)
        return pltpu.make_async_copy(x_hbm.at[pl.ds(pl.multiple_of(b * blk, blk), blk)], x_buf.at[s], x_sems.at[s])

    def y_copy(b):
        s = lax.rem(b, RING)
        return pltpu.make_async_copy(y_buf.at[s], y_hbm.at[pl.ds(pl.multiple_of(b * blk, blk), blk)], y_sems.at[s])

    def weight_copies(e):
        return (pltpu.make_async_copy(wg_hbm.at[e], wg_f, sems.at[0]),
                pltpu.make_async_copy(wu_hbm.at[e], wu_f, sems.at[1]),
                pltpu.make_async_copy(wd_hbm.at[e], wd_f, sems.at[2]))

    @pl.when(used & (i == 0))
    def _():
        for c in weight_copies(expert):
            c.start()
        for b in range(RING - 1):
            @pl.when(b < n_used)
            def _():
                x_copy(b).start()

    @pl.when(i + RING - 1 < n_used)
    def _():
        x_copy(i + RING - 1).start()

    @pl.when(used & new_expert)
    def _():
        for c in weight_copies(expert):
            c.wait()
        wgu_s[:, 0:D_EXPERT] = wg_f[...].astype(BF16)
        wgu_s[:, D_EXPERT:2 * D_EXPERT] = wu_f[...].astype(BF16)
        wd_s[...] = wd_f[...].astype(BF16)
        nxt = nx_ref[i]

        @pl.when(nxt >= 0)
        def _():
            for c in weight_copies(nxt):
                c.start()

    @pl.when(used)
    def _():
        x_copy(i).wait()

        @pl.when(i >= RING)
        def _():
            y_copy(i - RING).wait()

        def ffn(rows):
            live = lax.broadcasted_iota(jnp.int32, (rows, D_MODEL // 2), 0) < bv_ref[i]
            xa, xb = _unpack_rows(jnp.where(live, x_buf[slot, 0:rows, :], 0))
            half = D_MODEL // 2
            gu = _dot(xa.astype(BF16), wgu_s[0:half, :]) + _dot(xb.astype(BF16), wgu_s[half:D_MODEL, :])
            g, u = gu[:, 0:D_EXPERT], gu[:, D_EXPERT:2 * D_EXPERT]
            y_buf[slot, 0:rows, :] = _pack_rows(_dot((g * _sigmoid(g) * u).astype(BF16), wd_s[...]))

        @pl.when(bv_ref[i] > blk // 2)
        def _():
            ffn(blk)

        @pl.when(bv_ref[i] <= blk // 2)
        def _():
            ffn(blk // 2)
            y_buf[slot, blk // 2:blk, :] = jnp.zeros((blk // 2, D_MODEL // 2), jnp.int32)

        y_copy(i).start()

    @pl.when(i == n_used - 1)
    def _():
        for back in range(RING):
            @pl.when(i - back >= 0)
            def _():
                y_copy(i - back).wait()


def _experts(block_expert, block_valid, block_next, n_used, x_slots, wg, wu, wd):
    p_rows = x_slots.shape[0]
    blk = EXPERT_BLOCK
    hbm = pl.BlockSpec(memory_space=pl.ANY)
    grid_spec = pltpu.PrefetchScalarGridSpec(
        num_scalar_prefetch=4,
        grid=(p_rows // blk,),
        in_specs=[hbm, hbm, hbm, hbm],
        out_specs=hbm,
        scratch_shapes=[pltpu.VMEM((RING, blk, D_MODEL // 2), jnp.int32),
                        pltpu.VMEM((RING, blk, D_MODEL // 2), jnp.int32),
                        pltpu.VMEM((D_MODEL, D_EXPERT), F32), pltpu.VMEM((D_MODEL, D_EXPERT), F32),
                        pltpu.VMEM((D_EXPERT, D_MODEL), F32),
                        pltpu.VMEM((D_MODEL, 2 * D_EXPERT), BF16), pltpu.VMEM((D_EXPERT, D_MODEL), BF16),
                        pltpu.SemaphoreType.DMA((3,)), pltpu.SemaphoreType.DMA((RING,)),
                        pltpu.SemaphoreType.DMA((RING,))])
    return pl.pallas_call(
        _expert_kernel,
        out_shape=jax.ShapeDtypeStruct((p_rows, D_MODEL // 2), jnp.int32),
        grid_spec=grid_spec,
        compiler_params=_cparams(("arbitrary",)),
        name="expert_ffn",
    )(block_expert, block_valid, block_next, n_used, x_slots, wg, wu, wd)


def _combine_kernel(yg_ref, gate_ref, sh_ref, x1_ref, ada_ref, g_ref, y_ref):
    gates = gate_ref[...]
    half = D_MODEL // 2
    acc_a = sh_ref[:, 0:half]
    acc_b = sh_ref[:, half:D_MODEL]
    for k in range(TOP_K):
        ya, yb = _unpack_rows(yg_ref[k])
        acc_a = acc_a + gates[:, k:k + 1] * ya
        acc_b = acc_b + gates[:, k:k + 1] * yb
    gt2 = ada_ref[0, 5:6, :]
    y_ref[...] = x1_ref[...] + gt2 * _rms(jnp.concatenate([acc_a, acc_b], axis=1), g_ref[...])


def _combine(yg, gates_t, sh, x1, ada8, chunk, g_post, row_off):
    n_rows = yg.shape[1]
    tm = TM_COMB
    off = row_off // tm
    row = lambda i: (off + i, 0)
    return pl.pallas_call(
        _combine_kernel,
        out_shape=jax.ShapeDtypeStruct((n_rows, D_MODEL), F32),
        grid=(n_rows // tm,),
        in_specs=[pl.BlockSpec((TOP_K, tm, D_MODEL // 2), lambda i: (0, i, 0)),
                  pl.BlockSpec((tm, TOP_K), row),
                  pl.BlockSpec((tm, D_MODEL), row),
                  pl.BlockSpec((tm, D_MODEL), row),
                  pl.BlockSpec((1, 6, D_MODEL), lambda i: ((off + i) * tm // chunk, 0, 0)),
                  pl.BlockSpec((1, D_MODEL), lambda i: (0, 0))],
        out_specs=pl.BlockSpec((tm, D_MODEL), lambda i: (i, 0)),
        compiler_params=_cparams(("parallel",)),
        name="moe_combine",
    )(yg, gates_t, sh, x1, ada8, g_post)


def _rope_partner(w):
    half = QK_ROPE // 2
    return jnp.concatenate([-w[..., half:], w[..., :half]], axis=-1)


def _prep_weights(w_in, w_uq, w_ukv):
    d = w_in.shape[0]
    zeros = lambda r, c: jnp.zeros((r, c), F32)
    kr = w_in[:, 384:416]
    r_main = jnp.concatenate([zeros(d, QK_NOPE), kr, zeros(d, HEAD_PAD - QK_NOPE - QK_ROPE)], axis=1)
    r_part = jnp.concatenate([zeros(d, QK_NOPE), _rope_partner(kr), zeros(d, HEAD_PAD - QK_NOPE - QK_ROPE)], axis=1)
    win2 = jnp.concatenate([w_in[:, :384], r_main, r_part, w_in[:, 416:]], axis=1).astype(BF16)

    wq = w_uq.reshape(Q_LORA, N_HEADS, QK_NOPE + QK_ROPE)
    zq = jnp.zeros((Q_LORA, N_HEADS, HEAD_PAD - QK_NOPE - QK_ROPE), F32)
    q_main = jnp.concatenate([wq, zq], axis=-1).reshape(Q_LORA, N_HEADS * HEAD_PAD)
    q_part = jnp.concatenate([jnp.zeros((Q_LORA, N_HEADS, QK_NOPE), F32), _rope_partner(wq[..., QK_NOPE:]), zq],
                             axis=-1).reshape(Q_LORA, N_HEADS * HEAD_PAD)
    wuq2 = jnp.concatenate([q_main, q_part], axis=1).astype(BF16)

    wkv = w_ukv.reshape(KV_LORA, N_HEADS, QK_NOPE + HEAD_DIM)
    k_pad = jnp.concatenate([wkv[..., :QK_NOPE], jnp.zeros((KV_LORA, N_HEADS, HEAD_PAD - QK_NOPE), F32)],
                            axis=-1).reshape(KV_LORA, N_HEADS * HEAD_PAD)
    v_cols = wkv[..., QK_NOPE:].reshape(KV_LORA, N_HEADS * HEAD_DIM)
    wukv2 = jnp.concatenate([k_pad, v_cols], axis=1).astype(BF16)
    return win2, wuq2, wukv2


def _rope_table(s_max):
    half = QK_ROPE // 2
    inv = ROPE_BASE ** (-jnp.arange(half, dtype=F32) / half)
    ang = jnp.arange(s_max, dtype=F32)[:, None] * inv[None, :]
    cos = jnp.concatenate([jnp.cos(ang), jnp.cos(ang)], axis=1)
    sin = jnp.concatenate([jnp.sin(ang), jnp.sin(ang)], axis=1)
    scale = (QK_NOPE + QK_ROPE) ** -0.5 * math.log2(math.e)
    pad = jnp.zeros((s_max, HEAD_PAD - QK_NOPE - QK_ROPE), F32)
    zn = jnp.zeros((s_max, QK_NOPE), F32)
    cosq = jnp.concatenate([jnp.full((s_max, QK_NOPE), scale, F32), cos * scale, pad], axis=1)
    sinq = jnp.concatenate([zn, sin * scale, pad], axis=1)
    cosk = jnp.concatenate([zn, cos, pad], axis=1)
    sink = jnp.concatenate([zn, sin, pad], axis=1)
    return jnp.concatenate([cosq, sinq, cosk, sink], axis=1)


def _layer(x_prompt, x_sample, c_prompt, c_sample, w_ada, b_ada, g_pre_mix, w_in, g_qa, w_uq, g_kva, w_ukv,
           g_out_a, g_out_b, w_o, g_post_mix, g_pre_ffn, router_w, router_bias, w_exp_gate, w_exp_up, w_exp_down,
           w_sh_gate, w_sh_up, w_sh_down, g_post_ffn):
    b1, s1, d = x_prompt.shape
    b2, s2, _ = x_sample.shape
    n1, n2 = b1 * s1, b2 * s2
    t_all = n1 + n2
    chunk = math.gcd(s1, s2)
    assert n1 % s2 == 0 and chunk % max(TM_PRE, TM_POST, TM_COMB) == 0

    x_p, x_s = x_prompt.reshape(n1, d), x_sample.reshape(n2, d)
    c_all = jnp.concatenate([c_prompt, c_sample, jnp.zeros((8 - b1 - b2, d), F32)], axis=0)
    ada = _ada(c_all, w_ada, b_ada)
    chunk_batch = jnp.concatenate([jnp.repeat(jnp.arange(b1), s1 // chunk),
                                   b1 + jnp.repeat(jnp.arange(b2), s2 // chunk)])
    ada8 = ada.reshape(8, 6, d)[chunk_batch]

    win2, wuq2, wukv2 = _prep_weights(w_in, w_uq, w_ukv)
    tab = _rope_table(max(s1, s2))
    r2 = lambda g: g.reshape(1, -1)
    qa, ka, va, *dil_qkv = _premix(x_p, x_s, ada8, chunk, r2(g_pre_mix), win2, r2(g_qa), wuq2, r2(g_kva), wukv2, tab,
                                   n1, s1, s2)

    oa_p, oa_s = _mla(qa, ka, va, 0, b1, s1), _mla(qa, ka, va, n1, b2, s2)
    os_, ls_ = [], []
    for p, dil in enumerate(DILATIONS):
        o, lse = _dilated(*dil_qkv[3 * p:3 * p + 3], dil, n1, s1, s2)
        os_.append(o)
        ls_.append(lse)

    x1, h2p, sh, idx_t, gate_t, rank_t, cnt = _postmix(
        x_p, x_s, oa_p, oa_s, os_, ls_, ada8, chunk, r2(g_out_a), r2(g_out_b), w_o.astype(BF16), r2(g_post_mix), r2(g_pre_ffn),
        router_w.T, router_bias.reshape(N_EXPERTS, 1), w_sh_gate.astype(BF16), w_sh_up.astype(BF16),
        w_sh_down.astype(BF16))

    blk = EXPERT_BLOCK
    n_assign = t_all * TOP_K
    n_blocks = -(-(n_assign + N_EXPERTS * (blk - 1)) // blk)
    counts = cnt[:, 0].astype(jnp.int32)
    padded = (counts + blk - 1) // blk * blk
    pad_end = jnp.cumsum(padded)
    pad_start = pad_end - padded
    dest = _dest(idx_t, rank_t, pad_start)
    block_first = jnp.arange(n_blocks, dtype=jnp.int32) * blk
    block_expert = jnp.minimum(jnp.sum((pad_end[None, :] <= block_first[:, None]).astype(jnp.int32), axis=1),
                               N_EXPERTS - 1)
    n_used = (pad_end[-1] // blk).astype(jnp.int32).reshape(1)
    block_valid = jnp.clip(counts[block_expert] - (block_first - pad_start[block_expert]), 0, blk)
    eid = jnp.arange(N_EXPERTS, dtype=jnp.int32)
    used_from = lax.cummin(jnp.where(counts > 0, eid, N_EXPERTS), axis=0, reverse=True)
    next_used = jnp.concatenate([used_from[1:], jnp.full((1,), N_EXPERTS, jnp.int32)])
    block_next = jnp.where(next_used < N_EXPERTS, next_used, -1)[block_expert]
    dest_flat = dest.reshape(n_assign)
    x_slots = _dispatch_rows(h2p, dest_flat, n_blocks * blk)
    y_slots = _experts(block_expert, block_valid, block_next, n_used, x_slots, w_exp_gate, w_exp_up, w_exp_down)
    outs = []
    for off, n_g in ((0, n1), (n1, n2)):
        yg = _gather_rows(y_slots, dest[:, off:off + n_g].reshape(TOP_K * n_g)).reshape(TOP_K, n_g, d // 2)
        outs.append(_combine(yg, gate_t.T, sh, x1, ada8, chunk, r2(g_post_ffn), off))
    return outs[0].reshape(b1, s1, d), outs[1].reshape(b2, s2, d)


def kernel(x_prompt, x_sample, c_prompt, c_sample, w_ada, b_ada, g_pre_mix, w_in, g_qa, w_uq, g_kva, w_ukv, g_out_a, g_out_b, w_o, g_post_mix, g_pre_ffn, router_w, router_bias, w_exp_gate, w_exp_up, w_exp_down, w_sh_gate, w_sh_up, w_sh_down, g_post_ffn):
    layer = [p[0] for p in (w_ada, b_ada, g_pre_mix, w_in, g_qa, w_uq, g_kva, w_ukv, g_out_a, g_out_b, w_o,
                            g_post_mix, g_pre_ffn, router_w, router_bias, w_exp_gate, w_exp_up, w_exp_down,
                            w_sh_gate, w_sh_up, w_sh_down, g_post_ffn)]
    return _layer(x_prompt, x_sample, c_prompt, c_sample, *layer)
```

```python
import functools
import math

import jax
import jax.numpy as jnp
from jax import lax
from jax.experimental import pallas as pl
from jax.experimental.pallas import tpu as pltpu
from jax.experimental.pallas import tpu_sc as plsc

F32 = jnp.float32
BF16 = jnp.bfloat16

D_MODEL = 1024
HEAD_DIM = 64
N_HEADS = 8
Q_LORA = 256
KV_LORA = 128
QK_NOPE = 64
QK_ROPE = 32
ROPE_BASE = 10000.0
DIL_PATTERNS = ((128, 1), (512, 4), (2048, 16))
DILATIONS = tuple(d for _, d in DIL_PATTERNS)
DIL_HALF = 64
assert all(w // (2 * d) == DIL_HALF for w, d in DIL_PATTERNS) and DILATIONS[0] == 1
N_EXPERTS = 256
TOP_K = 8
N_GROUPS = 8
GROUP_SIZE = N_EXPERTS // N_GROUPS
TOPK_GROUPS = 4
D_EXPERT = 256
ROUTE_SCALE = 2.5
EPS = 1e-6
NEG = -1e30

LANES = 128
HEAD_PAD = 128

TM_PRE = 512
TM_POST = 512
TM_COMB = 512
TQ_MLA = 256
TK_MLA = 512
TM_DEST = 2048
SC_WINDOW = 128
TQ_DIL = 512
DIL_SUB = 128
EXPERT_BLOCK = 512
RING = 3
VMEM_LIMIT = 56 * 1024 * 1024


def _cparams(sem):
    return pltpu.CompilerParams(dimension_semantics=sem, vmem_limit_bytes=VMEM_LIMIT)


def _rms(x, g):
    return x * lax.rsqrt(jnp.mean(x * x, axis=-1, keepdims=True) + EPS) * g


def _sigmoid(x):
    return 1.0 / (1.0 + jnp.exp(-x))


def _dot(a, b):
    return jnp.dot(a, b, preferred_element_type=F32)


def _group_specs(tm, width, nt1):
    return [pl.BlockSpec((tm, width), lambda i: (jnp.minimum(i, nt1 - 1), 0)),
            pl.BlockSpec((tm, width), lambda i: (jnp.maximum(i - nt1, 0), 0))]


def _group_tile(first_ref, second_ref, nt1):
    return jnp.where(pl.program_id(0) < nt1, first_ref[...], second_ref[...])


def _pack_rows(x):
    n = x.shape[1] // 2
    bits = lax.bitcast_convert_type(x.astype(BF16).astype(F32), jnp.int32)
    return bits[:, :n] | lax.shift_right_logical(bits[:, n:], 16)


def _unpack_rows(u):
    hi = lax.bitcast_convert_type(u & jnp.int32(-65536), F32)
    lo = lax.bitcast_convert_type(lax.shift_left(u, 16), F32)
    return hi, lo


def _dot_nt(a, b, precision=None):
    return lax.dot_general(a, b, (((1,), (1,)), ((), ())), preferred_element_type=F32, precision=precision)


def _ada_kernel(c_ref, w_ref, b_ref, o_ref):
    c = c_ref[...]
    s = c * _sigmoid(c)
    o_ref[...] = jnp.dot(s, w_ref[...], preferred_element_type=F32, precision=lax.Precision.HIGHEST) + b_ref[...]


def _ada(c_all, w_ada, b_ada):
    nb, d = c_all.shape
    n_out = w_ada.shape[1]
    tn = 1024
    return pl.pallas_call(
        _ada_kernel,
        out_shape=jax.ShapeDtypeStruct((nb, n_out), F32),
        grid=(n_out // tn,),
        in_specs=[pl.BlockSpec((nb, d), lambda j: (0, 0)),
                  pl.BlockSpec((d, tn), lambda j: (0, j)),
                  pl.BlockSpec((1, tn), lambda j: (0, j))],
        out_specs=pl.BlockSpec((nb, tn), lambda j: (0, j)),
        compiler_params=_cparams(("arbitrary",)),
        name="ada",
    )(c_all, w_ada, b_ada.reshape(1, n_out))


def _premix_kernel(xp_ref, xs_ref, ada_ref, g_ref, win_ref, gqa_ref, wuq_ref, gkva_ref, wukv_ref, tab_ref,
                   qa_ref, ka_ref, va_ref, *rest, nt1):
    dil_refs, zs_ref = rest[:-1], rest[-1]
    x = _group_tile(xp_ref, xs_ref, nt1)
    sh1 = ada_ref[0, 0:1, :]
    sc1 = ada_ref[0, 1:2, :]
    h = _rms(x, g_ref[...]) * (1.0 + sc1) + sh1
    z = _dot(h.astype(BF16), win_ref[...])
    tab = tab_ref[...]
    cosq, sinq = tab[:, 0:128], tab[:, 128:256]
    cosk, sink = tab[:, 256:384], tab[:, 384:512]
    nh = N_HEADS * HEAD_PAD

    cq = _rms(z[:, 0:Q_LORA], gqa_ref[...]).astype(BF16)
    qq = _dot(cq, wuq_ref[...])
    for h_i in range(N_HEADS):
        lo = h_i * HEAD_PAD
        qa_ref[:, lo:lo + HEAD_PAD] = (qq[:, lo:lo + HEAD_PAD] * cosq
                                       + qq[:, nh + lo:nh + lo + HEAD_PAD] * sinq).astype(BF16)

    ckv = _rms(z[:, 256:384], gkva_ref[...]).astype(BF16)
    kk = _dot(ckv, wukv_ref[...])
    rr = z[:, 384:512] * cosk + z[:, 512:640] * sink
    for h_i in range(N_HEADS):
        lo = h_i * HEAD_PAD
        ka_ref[:, lo:lo + HEAD_PAD] = (kk[:, lo:lo + HEAD_PAD] + rr).astype(BF16)
    low = lax.broadcasted_iota(jnp.int32, (x.shape[0], LANES), 1) < HEAD_DIM
    for j in range(N_HEADS // 2):
        vpair = kk[:, nh + j * LANES:nh + (j + 1) * LANES]
        va_ref[:, (2 * j) * LANES:(2 * j + 1) * LANES] = jnp.where(low, vpair, 1.0).astype(BF16)
        va_ref[:, (2 * j + 1) * LANES:(2 * j + 2) * LANES] = jnp.where(low, 1.0, vpair).astype(BF16)

    tm = x.shape[0]
    wd = N_HEADS * HEAD_DIM
    n_slab = 3 * wd // LANES
    for c in range(n_slab):
        scale = HEAD_DIM ** -0.5 * math.log2(math.e) if c < wd // LANES else 1.0
        zs_ref[c] = z[:, 640 + c * LANES:640 + (c + 1) * LANES] * scale
    for dil, refs in zip(DILATIONS, (dil_refs[0:3], dil_refs[3:6], dil_refs[6:9])):
        n = tm // dil
        for r in range(dil):
            for c in range(n_slab):
                rows = zs_ref[c] if dil == 1 else zs_ref.at[c][pl.ds(r, n, stride=dil), :]
                col = r * wd + (c % (wd // LANES)) * LANES
                refs[c // (wd // LANES)][:, col:col + LANES] = rows.astype(BF16)


def _premix(x_p, x_s, ada8, chunk, g_pre, win2, g_qa, wuq2, g_kva, wukv2, tab, n1, s1, s2):
    t_all = x_p.shape[0] + x_s.shape[0]
    tm = TM_PRE
    nt1 = n1 // tm
    tb1, tb2 = s1 // tm, s2 // tm

    def tab_map(i):
        return (jnp.where(i < nt1, i % tb1, (i - nt1) % tb2), 0)

    row = lambda i: (i, 0)
    const = lambda i: (0, 0)
    wd = N_HEADS * HEAD_DIM
    outs = [jax.ShapeDtypeStruct((t_all, N_HEADS * HEAD_PAD), BF16)] * 3
    out_specs = [pl.BlockSpec((tm, N_HEADS * HEAD_PAD), row)] * 3
    for dil in DILATIONS:
        outs += [jax.ShapeDtypeStruct((t_all // dil, dil * wd), BF16)] * 3
        out_specs += [pl.BlockSpec((tm // dil, dil * wd), row)] * 3
    return pl.pallas_call(
        functools.partial(_premix_kernel, nt1=nt1),
        out_shape=outs,
        grid=(t_all // tm,),
        in_specs=_group_specs(tm, D_MODEL, nt1) + [
                  pl.BlockSpec((1, 6, D_MODEL), lambda i: (i * tm // chunk, 0, 0)),
                  pl.BlockSpec((1, D_MODEL), const),
                  pl.BlockSpec(win2.shape, const),
                  pl.BlockSpec((1, Q_LORA), const),
                  pl.BlockSpec(wuq2.shape, const),
                  pl.BlockSpec((1, KV_LORA), const),
                  pl.BlockSpec(wukv2.shape, const),
                  pl.BlockSpec((tm, 512), tab_map)],
        out_specs=out_specs,
        scratch_shapes=[pltpu.VMEM((3 * wd // LANES, tm, LANES), F32)],
        compiler_params=_cparams(("parallel",)),
        name="premix",
    )(x_p, x_s, ada8, g_pre, win2, g_qa, wuq2, g_kva, wukv2, tab)


def _mla_kernel(q_ref, k_ref, v_ref, o_ref, s_ref, m_ref, *, tk):
    tq, seq = s_ref.shape[1], s_ref.shape[2]
    lane = lax.broadcasted_iota(jnp.int32, (tq, LANES), 1)

    def scores(hh):
        s = _dot_nt(q_ref[:, hh * HEAD_PAD:(hh + 1) * HEAD_PAD], k_ref[:, hh * HEAD_PAD:(hh + 1) * HEAD_PAD])
        s_ref[hh] = s
        m_ref[hh] = jnp.max(s, axis=-1, keepdims=True)

    def values(hh):
        acc = jnp.zeros((tq, LANES), F32)
        m = m_ref[hh]
        for c0 in range(0, seq, tk):
            p = jnp.exp2(s_ref[hh, :, c0:c0 + tk] - m)
            acc = acc + _dot(p.astype(BF16), v_ref[c0:c0 + tk, hh * LANES:(hh + 1) * LANES])
        return acc / pltpu.roll(acc, HEAD_DIM, axis=1)

    scores(0)
    scores(1)
    o0 = values(0)
    o1 = values(1)
    o_ref[...] = jnp.where(lane < HEAD_DIM, o0, o1).astype(o_ref.dtype)


def _mla(qa, ka, va, row_off, batch, seq):
    tq = TQ_MLA
    nq = seq // tq
    qoff = row_off // tq
    soff = row_off // seq
    return pl.pallas_call(
        functools.partial(_mla_kernel, tk=TK_MLA),
        out_shape=jax.ShapeDtypeStruct((batch * seq, N_HEADS * HEAD_DIM), BF16),
        grid=(batch, N_HEADS // 2, nq),
        in_specs=[pl.BlockSpec((tq, 2 * HEAD_PAD), lambda b, j, qi: (qoff + b * nq + qi, j)),
                  pl.BlockSpec((seq, 2 * HEAD_PAD), lambda b, j, qi: (soff + b, j)),
                  pl.BlockSpec((seq, 2 * LANES), lambda b, j, qi: (soff + b, j))],
        out_specs=pl.BlockSpec((tq, LANES), lambda b, j, qi: (b * nq + qi, j)),
        scratch_shapes=[pltpu.VMEM((2, tq, seq), F32), pltpu.VMEM((2, tq, 1), F32)],
        compiler_params=_cparams(("parallel", "parallel", "parallel")),
        name="mla_attention",
    )(qa, ka, va)


def _dil_kernel(q_ref, kp_ref, kc_ref, kn_ref, vp_ref, vc_ref, vn_ref, bias_ref, o_ref, lse_ref, *,
                n1_rows, len1, len2):
    tq = q_ref.shape[0]
    sub = bias_ref.shape[2]
    w = bias_ref.shape[3]
    row0 = pl.program_id(0) * tq
    in1 = row0 < n1_rows
    seq_len = jnp.where(in1, len1, len2)
    q0 = jnp.where(in1, row0 % len1, (row0 - n1_rows) % len2)

    kw = jnp.concatenate([kp_ref[...], kc_ref[...], kn_ref[...]], axis=0)
    vw = jnp.concatenate([vp_ref[...], vc_ref[...], vn_ref[...]], axis=0)
    low = lax.broadcasted_iota(jnp.int32, (sub, LANES), 1) < HEAD_DIM
    low_w = lax.broadcasted_iota(jnp.int32, (w, LANES), 1) < HEAD_DIM
    chains = [(t, h) for t in range(tq // sub) for h in range(N_HEADS)]
    lanes = lambda h: slice((h // 2) * LANES, (h // 2 + 1) * LANES)
    starts = [(q0 + t * sub) % seq_len for t in range(tq // sub)]
    variant = [(st == 0).astype(jnp.int32) + 2 * (st + sub == seq_len).astype(jnp.int32) for st in starts]
    qs = [q_ref[t * sub:(t + 1) * sub, lanes(h)] for t, h in chains]
    qs = [jnp.where(low, q, 0) if h % 2 == 0 else jnp.where(low, 0, q) for q, (t, h) in zip(qs, chains)]
    vs = [vw[t * sub:t * sub + w, lanes(h)] for t, h in chains]
    vs = [jnp.where(low_w, v, 1) if h % 2 == 0 else jnp.where(low_w, 1, v) for v, (t, h) in zip(vs, chains)]
    ss = [_dot_nt(q.astype(BF16), kw[t * sub:t * sub + w, lanes(h)]) + bias_ref[variant[t], h]
          for q, (t, h) in zip(qs, chains)]
    ms = [jnp.max(s, axis=-1, keepdims=True) for s in ss]
    ps = [jnp.exp2(s - m).astype(BF16) for s, m in zip(ss, ms)]
    accs = [_dot(p, v.astype(BF16)) for p, v in zip(ps, vs)]
    ls = [pltpu.roll(acc, HEAD_DIM, axis=1) for acc in accs]
    outs = [acc / l for acc, l in zip(accs, ls)]
    lses = [m * math.log(2.0) + jnp.log(l) for m, l in zip(ms, ls)]
    for c in range(0, len(chains), 2):
        t, h = chains[c]
        o_ref[t * sub:(t + 1) * sub, lanes(h)] = jnp.where(low, outs[c], outs[c + 1]).astype(o_ref.dtype)
        lse_ref[t * sub:(t + 1) * sub, lanes(h)] = jnp.where(low, lses[c], lses[c + 1])


def _dil_bias(dil, tq):
    half = DIL_HALF
    col = jnp.arange(tq + 2 * half)[None, :]
    dist = jnp.abs(col - half - jnp.arange(tq)[:, None])
    slopes = 2.0 ** (-8.0 * jnp.arange(1, N_HEADS + 1, dtype=F32) / N_HEADS)
    alibi = -math.log2(math.e) * slopes[:, None, None] * (dil * dist).astype(F32)[None]
    band = dist <= half
    first, last = col >= half, col < tq + half
    masks = [band, band & first, band & last, band & first & last]
    return jnp.stack([jnp.where(mk[None], alibi, NEG) for mk in masks])


def _dilated(qv, kv, vv, dil, n1, s1, s2):
    rows = qv.shape[0]
    tq = TQ_DIL
    nt = rows // tq
    wk = N_HEADS * HEAD_DIM
    cur = lambda i, r: (i, r)
    per = tq // DIL_HALF
    prv = lambda i, r: (jnp.maximum(i * per - 1, 0), r)
    nxt = lambda i, r: (jnp.minimum((i + 1) * per, nt * per - 1), r)
    kspecs = [pl.BlockSpec((DIL_HALF, wk), prv), pl.BlockSpec((tq, wk), cur), pl.BlockSpec((DIL_HALF, wk), nxt)]
    bias = _dil_bias(dil, DIL_SUB)
    assert tq % DIL_SUB == 0 and (s1 // dil) % DIL_SUB == 0 and (s2 // dil) % DIL_SUB == 0
    assert (n1 // dil) % tq == 0 and rows % tq == 0
    return pl.pallas_call(
        functools.partial(_dil_kernel, n1_rows=n1 // dil, len1=s1 // dil, len2=s2 // dil),
        out_shape=[jax.ShapeDtypeStruct((rows, dil * wk), BF16), jax.ShapeDtypeStruct((rows, dil * wk), F32)],
        grid=(nt, dil),
        in_specs=[pl.BlockSpec((tq, wk), cur)] + kspecs + kspecs
                 + [pl.BlockSpec(bias.shape, lambda i, r: (0, 0, 0, 0))],
        out_specs=[pl.BlockSpec((tq, wk), cur), pl.BlockSpec((tq, wk), cur)],
        compiler_params=_cparams(("parallel", "parallel")),
        name="dilated_attention_d%d" % dil,
    )(qv, kv, kv, kv, vv, vv, vv, bias)


def _postmix_kernel(xp_ref, xs_ref, oap_ref, oas_ref, o1_ref, o2_ref, o3_ref, l1_ref, l2_ref, l3_ref, ada_ref, goa_ref, gob_ref,
                    wo_ref, gpm_ref, gpf_ref, rwh_ref, rwl_ref, rb_ref, wsg_ref, wsu_ref, wsd_ref, before_ref,
                    x1_ref, h2_ref, sh_ref, idx_ref, gate_ref, rank_ref, cnt_ref, carry_ref, nat_ref, *, nt1):
    tm = xp_ref.shape[0]

    @pl.when(pl.program_id(0) == 0)
    def _():
        carry_ref[...] = jnp.zeros_like(carry_ref)

    gt1, sh2, sc2 = ada_ref[0, 2:3, :], ada_ref[0, 3:4, :], ada_ref[0, 4:5, :]

    wd = N_HEADS * HEAD_DIM
    n_slab = wd // LANES

    def row_order(src_ref, dil, base):
        if dil == 1:
            return src_ref[...].astype(F32)
        n = tm // dil
        for r in range(dil):
            for c in range(n_slab):
                col = r * wd + c * LANES
                nat_ref.at[base + c][pl.ds(r, n, stride=dil), :] = src_ref[:, col:col + LANES].astype(F32)
        return jnp.concatenate([nat_ref[base + c] for c in range(n_slab)], axis=1)

    o_pat = [row_order(ref, dil, (2 * p) * n_slab) for p, (ref, dil) in enumerate(zip((o1_ref, o2_ref, o3_ref), DILATIONS))]
    la, lb, lc = [row_order(ref, dil, (2 * p + 1) * n_slab)
                  for p, (ref, dil) in enumerate(zip((l1_ref, l2_ref, l3_ref), DILATIONS))]

    mx = jnp.maximum(jnp.maximum(la, lb), lc)
    ea, eb, ec = jnp.exp(la - mx), jnp.exp(lb - mx), jnp.exp(lc - mx)
    ob = (ea * o_pat[0] + eb * o_pat[1] + ec * o_pat[2]) / (ea + eb + ec)

    na = _rms(_group_tile(oap_ref, oas_ref, nt1).astype(F32), goa_ref[...]).astype(BF16)
    nb = _rms(ob, gob_ref[...]).astype(BF16)
    half_w = N_HEADS * HEAD_DIM
    o = _dot(na, wo_ref[0:half_w, :]) + _dot(nb, wo_ref[half_w:2 * half_w, :])
    x1 = _group_tile(xp_ref, xs_ref, nt1) + gt1 * _rms(o, gpm_ref[...])
    x1_ref[...] = x1
    h2 = _rms(x1, gpf_ref[...]) * (1.0 + sc2) + sh2
    h2b = h2.astype(BF16)
    h2_ref[...] = _pack_rows(h2)

    g = _dot(h2b, wsg_ref[...])
    u = _dot(h2b, wsu_ref[...])
    sh_ref[...] = _dot((g * _sigmoid(g) * u).astype(BF16), wsd_ref[...])

    h_lo = (h2 - h2b.astype(F32)).astype(BF16)
    logits = _dot_nt(rwh_ref[...], h2b) + (_dot_nt(rwh_ref[...], h_lo) + _dot_nt(rwl_ref[...], h2b))
    scores = _sigmoid(logits)
    biased = scores + rb_ref[...]
    ninf = -jnp.inf
    row = lax.broadcasted_iota(jnp.int32, (N_EXPERTS, tm), 0)
    rwg = lax.broadcasted_iota(jnp.int32, (GROUP_SIZE, tm), 0)
    gsc = []
    for gi in range(N_GROUPS):
        blk = biased[gi * GROUP_SIZE:(gi + 1) * GROUP_SIZE]
        m1 = jnp.max(blk, axis=0, keepdims=True)
        i1 = jnp.min(jnp.where(blk == m1, rwg, N_EXPERTS), axis=0, keepdims=True)
        m2 = jnp.max(jnp.where(rwg == i1, ninf, blk), axis=0, keepdims=True)
        gsc.append(m1 + m2)
    gsel = [jnp.zeros((1, tm), F32) for _ in range(N_GROUPS)]
    for _ in range(TOPK_GROUPS):
        m = functools.reduce(jnp.maximum, gsc)
        free = jnp.ones((1, tm), F32)
        for gi in range(N_GROUPS):
            hit = jnp.where(gsc[gi] == m, free, 0.0)
            free = free - hit
            gsel[gi] = gsel[gi] + hit
            gsc[gi] = jnp.where(hit > 0.0, ninf, gsc[gi])
    masked = jnp.concatenate(
        [jnp.where(gsel[gi] > 0.0, biased[gi * GROUP_SIZE:(gi + 1) * GROUP_SIZE], ninf)
         for gi in range(N_GROUPS)], axis=0)
    cur = masked
    idxs, gts = [], []
    for _ in range(TOP_K):
        m = jnp.max(cur, axis=0, keepdims=True)
        ik = jnp.min(jnp.where(cur == m, row, N_EXPERTS), axis=0, keepdims=True)
        hit = row == ik
        idxs.append(ik)
        gts.append(jnp.sum(jnp.where(hit, scores, 0.0), axis=0, keepdims=True))
        cur = jnp.where(hit, ninf, cur)
    idx = jnp.concatenate(idxs, axis=0)
    gates = jnp.concatenate(gts, axis=0)
    idx_ref[...] = idx
    gate_ref[...] = gates / jnp.sum(gates, axis=0, keepdims=True) * ROUTE_SCALE

    sel = jnp.where(cur != masked, 1.0, 0.0)
    pos = _dot(sel.astype(BF16), before_ref[...]) + carry_ref[...]
    rank_ref[...] = jnp.concatenate(
        [jnp.sum(jnp.where(row == idxs[k], pos, 0.0), axis=0, keepdims=True) for k in range(TOP_K)],
        axis=0).astype(jnp.int32)
    carry = carry_ref[...] + jnp.sum(sel, axis=1, keepdims=True)
    carry_ref[...] = carry
    cnt_ref[...] = jnp.broadcast_to(carry, cnt_ref.shape)


def _postmix(x_p, x_s, oa_p, oa_s, os_, ls_, ada8, chunk, g_oa, g_ob, wo, g_pm, g_pf, rwt, rb, wsg, wsu, wsd):
    t_all = x_p.shape[0] + x_s.shape[0]
    tm = TM_POST
    nt1 = x_p.shape[0] // tm
    rw_hi = rwt.astype(BF16)
    rw_lo = (rwt - rw_hi.astype(F32)).astype(BF16)
    before = jnp.triu(jnp.ones((tm, tm), BF16), 1)
    row = lambda i: (i, 0)
    col = lambda i: (0, i)
    const = lambda i: (0, 0)
    hw = N_HEADS * HEAD_DIM
    pat_specs = [pl.BlockSpec((tm // dil, dil * hw), row) for dil in DILATIONS]
    in_specs = (_group_specs(tm, D_MODEL, nt1) + _group_specs(tm, hw, nt1)
                + pat_specs + pat_specs
                + [pl.BlockSpec((1, 6, D_MODEL), lambda i: (i * tm // chunk, 0, 0)),
                   pl.BlockSpec((1, hw), const), pl.BlockSpec((1, hw), const),
                   pl.BlockSpec(wo.shape, const),
                   pl.BlockSpec((1, D_MODEL), const), pl.BlockSpec((1, D_MODEL), const),
                   pl.BlockSpec(rwt.shape, const), pl.BlockSpec(rwt.shape, const), pl.BlockSpec((N_EXPERTS, 1), const),
                   pl.BlockSpec(wsg.shape, const), pl.BlockSpec(wsu.shape, const), pl.BlockSpec(wsd.shape, const),
                   pl.BlockSpec((tm, tm), const)])
    out_shape = [jax.ShapeDtypeStruct((t_all, D_MODEL), F32),
                 jax.ShapeDtypeStruct((t_all, D_MODEL // 2), jnp.int32),
                 jax.ShapeDtypeStruct((t_all, D_MODEL), F32),
                 jax.ShapeDtypeStruct((TOP_K, t_all), jnp.int32),
                 jax.ShapeDtypeStruct((TOP_K, t_all), F32),
                 jax.ShapeDtypeStruct((TOP_K, t_all), jnp.int32),
                 jax.ShapeDtypeStruct((N_EXPERTS, LANES), F32)]
    out_specs = [pl.BlockSpec((tm, D_MODEL), row), pl.BlockSpec((tm, D_MODEL // 2), row), pl.BlockSpec((tm, D_MODEL), row),
                 pl.BlockSpec((TOP_K, tm), col), pl.BlockSpec((TOP_K, tm), col), pl.BlockSpec((TOP_K, tm), col),
                 pl.BlockSpec((N_EXPERTS, LANES), const)]
    return pl.pallas_call(
        functools.partial(_postmix_kernel, nt1=nt1),
        out_shape=out_shape,
        grid=(t_all // tm,),
        in_specs=in_specs,
        out_specs=out_specs,
        scratch_shapes=[pltpu.VMEM((N_EXPERTS, 1), F32),
                        pltpu.VMEM((2 * len(DILATIONS) * hw // LANES, tm, LANES), F32)],
        compiler_params=_cparams(("arbitrary",)),
        name="postmix_router",
    )(x_p, x_s, oa_p, oa_s, *os_, *ls_, ada8, g_oa, g_ob, wo, g_pm, g_pf, rw_hi, rw_lo, rb, wsg, wsu, wsd, before)


def _dest_kernel(idx_ref, rank_ref, start_ref, dest_ref):
    tm = idx_ref.shape[1]
    row = lax.broadcasted_iota(jnp.int32, (N_EXPERTS, tm), 0)
    start = start_ref[...]
    base = [jnp.sum(jnp.where(row == idx_ref[k:k + 1, :], start, 0), axis=0, keepdims=True) for k in range(TOP_K)]
    dest_ref[...] = jnp.concatenate(base, axis=0) + rank_ref[...]


def _dest(idx_t, rank_t, pad_start):
    t_all = idx_t.shape[1]
    tm = TM_DEST
    col = lambda i: (0, i)
    return pl.pallas_call(
        _dest_kernel,
        out_shape=jax.ShapeDtypeStruct((TOP_K, t_all), jnp.int32),
        grid=(t_all // tm,),
        in_specs=[pl.BlockSpec((TOP_K, tm), col), pl.BlockSpec((TOP_K, tm), col),
                  pl.BlockSpec((N_EXPERTS, 1), lambda i: (0, 0))],
        out_specs=pl.BlockSpec((TOP_K, tm), col),
        compiler_params=_cparams(("parallel",)),
        name="slot_index",
    )(idx_t, rank_t, pad_start.reshape(N_EXPERTS, 1))


def _sc_mesh():
    return plsc.VectorSubcoreMesh(core_axis_name="core", subcore_axis_name="subcore")


def _dispatch_rows(rows, dest, n_slots):
    t_all, width = rows.shape
    win = SC_WINDOW // 2
    info = plsc.get_sparse_core_info()
    n_workers = info.num_cores * info.num_subcores
    per_worker = t_all // n_workers
    n_pairs = per_worker // (2 * win)
    assert per_worker % (2 * win) == 0

    @functools.partial(
        pl.kernel, out_type=jax.ShapeDtypeStruct((n_slots, width), rows.dtype), mesh=_sc_mesh(),
        scratch_types=[pltpu.VMEM((TOP_K, win), jnp.int32), pltpu.VMEM((TOP_K, win), jnp.int32),
                       pltpu.VMEM((win, width), rows.dtype), pltpu.VMEM((win, width), rows.dtype),
                       pltpu.SemaphoreType.DMA, pltpu.SemaphoreType.DMA])
    def scatter_kernel(x_hbm, i_hbm, o_hbm, idx_a, idx_b, rows_a, rows_b, sem_a, sem_b):
        base = (lax.axis_index("subcore") * info.num_cores + lax.axis_index("core")) * per_worker

        def load(w, idx_v, rows_v):
            off = base + w * win
            for k in range(TOP_K):
                pltpu.sync_copy(i_hbm.at[pl.ds(k * t_all + off, win)], idx_v.at[k])
            pltpu.sync_copy(x_hbm.at[pl.ds(off, win)], rows_v)

        def start(idx_v, rows_v, sem):
            for k in range(TOP_K):
                pltpu.async_copy(rows_v, o_hbm.at[idx_v.at[k]], sem)

        def finish(idx_v, rows_v, sem):
            for k in range(TOP_K):
                pltpu.make_async_copy(rows_v, o_hbm.at[idx_v.at[k]], sem).wait()

        load(0, idx_a, rows_a)

        @pl.loop(0, n_pairs)
        def _(j):
            start(idx_a, rows_a, sem_a)
            load(2 * j + 1, idx_b, rows_b)
            finish(idx_a, rows_a, sem_a)
            start(idx_b, rows_b, sem_b)

            @pl.when(j + 1 < n_pairs)
            def _():
                load(2 * j + 2, idx_a, rows_a)

            finish(idx_b, rows_b, sem_b)

    return scatter_kernel(rows, dest)


def _gather_rows(slots, index):
    n = index.shape[0]
    width = slots.shape[1]
    win = SC_WINDOW // 2
    info = plsc.get_sparse_core_info()
    n_workers = info.num_cores * info.num_subcores
    per_worker = n // n_workers
    n_pairs = per_worker // (2 * win)
    assert per_worker % (2 * win) == 0

    @functools.partial(
        pl.kernel, out_type=jax.ShapeDtypeStruct((n, width), slots.dtype), mesh=_sc_mesh(),
        scratch_types=[pltpu.VMEM((win,), jnp.int32), pltpu.VMEM((win,), jnp.int32),
                       pltpu.VMEM((win, width), slots.dtype), pltpu.VMEM((win, width), slots.dtype),
                       pltpu.SemaphoreType.DMA, pltpu.SemaphoreType.DMA])
    def gather_kernel(y_hbm, i_hbm, o_hbm, idx_a, idx_b, rows_a, rows_b, sem_a, sem_b):
        base = (lax.axis_index("subcore") * info.num_cores + lax.axis_index("core")) * per_worker

        def start(w, idx_v, rows_v, sem):
            pltpu.sync_copy(i_hbm.at[pl.ds(base + w * win, win)], idx_v)
            pltpu.async_copy(y_hbm.at[idx_v], rows_v, sem)

        def finish(w, idx_v, rows_v, sem):
            pltpu.make_async_copy(y_hbm.at[idx_v], rows_v, sem).wait()
            pltpu.sync_copy(rows_v, o_hbm.at[pl.ds(base + w * win, win)])

        start(0, idx_a, rows_a, sem_a)

        @pl.loop(0, n_pairs)
        def _(j):
            start(2 * j + 1, idx_b, rows_b, sem_b)
            finish(2 * j, idx_a, rows_a, sem_a)

            @pl.when(j + 1 < n_pairs)
            def _():
                start(2 * j + 2, idx_a, rows_a, sem_a)

            finish(2 * j + 1, idx_b, rows_b, sem_b)

    return gather_kernel(slots, index)


def _expert_kernel(be_ref, bv_ref, nx_ref, nu_ref, x_hbm, wg_hbm, wu_hbm, wd_hbm, y_hbm,
                   x_buf, y_buf, wg_f, wu_f, wd_f, wgu_s, wd_s, sems, x_sems, y_sems):
    i = pl.program_id(0)
    n_used = nu_ref[0]
    used = i < n_used
    expert = be_ref[i]
    new_expert = (i == 0) | (expert != be_ref[jnp.maximum(i - 1, 0)])
    blk = x_buf.shape[1]
    slot = lax.rem(i, RING)

    def x_copy(b):
        s = lax.rem(b, RING)
        return pltpu.make_async_copy(x_hbm.at[pl.ds(pl.multiple_of(b * blk, blk), blk)], x_buf.at[s], x_sems.at[s])

    def y_copy(b):
        s = lax.rem(b, RING)
        return pltpu.make_async_copy(y_buf.at[s], y_hbm.at[pl.ds(pl.multiple_of(b * blk, blk), blk)], y_sems.at[s])

    def weight_copies(e):
        return (pltpu.make_async_copy(wg_hbm.at[e], wg_f, sems.at[0]),
                pltpu.make_async_copy(wu_hbm.at[e], wu_f, sems.at[1]),
                pltpu.make_async_copy(wd_hbm.at[e], wd_f, sems.at[2]))

    @pl.when(used & (i == 0))
    def _():
        for c in weight_copies(expert):
            c.start()
        for b in range(RING - 1):
            @pl.when(b < n_used)
            def _():
                x_copy(b).start()

    @pl.when(i + RING - 1 < n_used)
    def _():
        x_copy(i + RING - 1).start()

    @pl.when(used & new_expert)
    def _():
        for c in weight_copies(expert):
            c.wait()
        wgu_s[:, 0:D_EXPERT] = wg_f[...].astype(BF16)
        wgu_s[:, D_EXPERT:2 * D_EXPERT] = wu_f[...].astype(BF16)
        wd_s[...] = wd_f[...].astype(BF16)
        nxt = nx_ref[i]

        @pl.when(nxt >= 0)
        def _():
            for c in weight_copies(nxt):
                c.start()

    @pl.when(used)
    def _():
        x_copy(i).wait()

        @pl.when(i >= RING)
        def _():
            y_copy(i - RING).wait()

        def ffn(rows):
            live = lax.broadcasted_iota(jnp.int32, (rows, D_MODEL // 2), 0) < bv_ref[i]
            xa, xb = _unpack_rows(jnp.where(live, x_buf[slot, 0:rows, :], 0))
            half = D_MODEL // 2
            gu = _dot(xa.astype(BF16), wgu_s[0:half, :]) + _dot(xb.astype(BF16), wgu_s[half:D_MODEL, :])
            g, u = gu[:, 0:D_EXPERT], gu[:, D_EXPERT:2 * D_EXPERT]
            y_buf[slot, 0:rows, :] = _pack_rows(_dot((g * _sigmoid(g) * u).astype(BF16), wd_s[...]))

        @pl.when(bv_ref[i] > blk // 2)
        def _():
            ffn(blk)

        @pl.when(bv_ref[i] <= blk // 2)
        def _():
            ffn(blk // 2)
            y_buf[slot, blk // 2:blk, :] = jnp.zeros((blk // 2, D_MODEL // 2), jnp.int32)

        y_copy(i).start()

    @pl.when(i == n_used - 1)
    def _():
        for back in range(RING):
            @pl.when(i - back >= 0)
            def _():
                y_copy(i - back).wait()


def _experts(block_expert, block_valid, block_next, n_used, x_slots, wg, wu, wd):
    p_rows = x_slots.shape[0]
    blk = EXPERT_BLOCK
    hbm = pl.BlockSpec(memory_space=pl.ANY)
    grid_spec = pltpu.PrefetchScalarGridSpec(
        num_scalar_prefetch=4,
        grid=(p_rows // blk,),
        in_specs=[hbm, hbm, hbm, hbm],
        out_specs=hbm,
        scratch_shapes=[pltpu.VMEM((RING, blk, D_MODEL // 2), jnp.int32),
                        pltpu.VMEM((RING, blk, D_MODEL // 2), jnp.int32),
                        pltpu.VMEM((D_MODEL, D_EXPERT), F32), pltpu.VMEM((D_MODEL, D_EXPERT), F32),
                        pltpu.VMEM((D_EXPERT, D_MODEL), F32),
                        pltpu.VMEM((D_MODEL, 2 * D_EXPERT), BF16), pltpu.VMEM((D_EXPERT, D_MODEL), BF16),
                        pltpu.SemaphoreType.DMA((3,)), pltpu.SemaphoreType.DMA((RING,)),
                        pltpu.SemaphoreType.DMA((RING,))])
    return pl.pallas_call(
        _expert_kernel,
        out_shape=jax.ShapeDtypeStruct((p_rows, D_MODEL // 2), jnp.int32),
        grid_spec=grid_spec,
        compiler_params=_cparams(("arbitrary",)),
        name="expert_ffn",
    )(block_expert, block_valid, block_next, n_used, x_slots, wg, wu, wd)


def _combine_kernel(yg_ref, gate_ref, sh_ref, x1_ref, ada_ref, g_ref, y_ref):
    gates = gate_ref[...]
    half = D_MODEL // 2
    acc_a = sh_ref[:, 0:half]
    acc_b = sh_ref[:, half:D_MODEL]
    for k in range(TOP_K):
        ya, yb = _unpack_rows(yg_ref[k])
        acc_a = acc_a + gates[:, k:k + 1] * ya
        acc_b = acc_b + gates[:, k:k + 1] * yb
    gt2 = ada_ref[0, 5:6, :]
    y_ref[...] = x1_ref[...] + gt2 * _rms(jnp.concatenate([acc_a, acc_b], axis=1), g_ref[...])


def _combine(yg, gates_t, sh, x1, ada8, chunk, g_post, row_off):
    n_rows = yg.shape[1]
    tm = TM_COMB
    off = row_off // tm
    row = lambda i: (off + i, 0)
    return pl.pallas_call(
        _combine_kernel,
        out_shape=jax.ShapeDtypeStruct((n_rows, D_MODEL), F32),
        grid=(n_rows // tm,),
        in_specs=[pl.BlockSpec((TOP_K, tm, D_MODEL // 2), lambda i: (0, i, 0)),
                  pl.BlockSpec((tm, TOP_K), row),
                  pl.BlockSpec((tm, D_MODEL), row),
                  pl.BlockSpec((tm, D_MODEL), row),
                  pl.BlockSpec((1, 6, D_MODEL), lambda i: ((off + i) * tm // chunk, 0, 0)),
                  pl.BlockSpec((1, D_MODEL), lambda i: (0, 0))],
        out_specs=pl.BlockSpec((tm, D_MODEL), lambda i: (i, 0)),
        compiler_params=_cparams(("parallel",)),
        name="moe_combine",
    )(yg, gates_t, sh, x1, ada8, g_post)


def _rope_partner(w):
    half = QK_ROPE // 2
    return jnp.concatenate([-w[..., half:], w[..., :half]], axis=-1)


def _prep_weights(w_in, w_uq, w_ukv):
    d = w_in.shape[0]
    zeros = lambda r, c: jnp.zeros((r, c), F32)
    kr = w_in[:, 384:416]
    r_main = jnp.concatenate([zeros(d, QK_NOPE), kr, zeros(d, HEAD_PAD - QK_NOPE - QK_ROPE)], axis=1)
    r_part = jnp.concatenate([zeros(d, QK_NOPE), _rope_partner(kr), zeros(d, HEAD_PAD - QK_NOPE - QK_ROPE)], axis=1)
    win2 = jnp.concatenate([w_in[:, :384], r_main, r_part, w_in[:, 416:]], axis=1).astype(BF16)

    wq = w_uq.reshape(Q_LORA, N_HEADS, QK_NOPE + QK_ROPE)
    zq = jnp.zeros((Q_LORA, N_HEADS, HEAD_PAD - QK_NOPE - QK_ROPE), F32)
    q_main = jnp.concatenate([wq, zq], axis=-1).reshape(Q_LORA, N_HEADS * HEAD_PAD)
    q_part = jnp.concatenate([jnp.zeros((Q_LORA, N_HEADS, QK_NOPE), F32), _rope_partner(wq[..., QK_NOPE:]), zq],
                             axis=-1).reshape(Q_LORA, N_HEADS * HEAD_PAD)
    wuq2 = jnp.concatenate([q_main, q_part], axis=1).astype(BF16)

    wkv = w_ukv.reshape(KV_LORA, N_HEADS, QK_NOPE + HEAD_DIM)
    k_pad = jnp.concatenate([wkv[..., :QK_NOPE], jnp.zeros((KV_LORA, N_HEADS, HEAD_PAD - QK_NOPE), F32)],
                            axis=-1).reshape(KV_LORA, N_HEADS * HEAD_PAD)
    v_cols = wkv[..., QK_NOPE:].reshape(KV_LORA, N_HEADS * HEAD_DIM)
    wukv2 = jnp.concatenate([k_pad, v_cols], axis=1).astype(BF16)
    return win2, wuq2, wukv2


def _rope_table(s_max):
    half = QK_ROPE // 2
    inv = ROPE_BASE ** (-jnp.arange(half, dtype=F32) / half)
    ang = jnp.arange(s_max, dtype=F32)[:, None] * inv[None, :]
    cos = jnp.concatenate([jnp.cos(ang), jnp.cos(ang)], axis=1)
    sin = jnp.concatenate([jnp.sin(ang), jnp.sin(ang)], axis=1)
    scale = (QK_NOPE + QK_ROPE) ** -0.5 * math.log2(math.e)
    pad = jnp.zeros((s_max, HEAD_PAD - QK_NOPE - QK_ROPE), F32)
    zn = jnp.zeros((s_max, QK_NOPE), F32)
    cosq = jnp.concatenate([jnp.full((s_max, QK_NOPE), scale, F32), cos * scale, pad], axis=1)
    sinq = jnp.concatenate([zn, sin * scale, pad], axis=1)
    cosk = jnp.concatenate([zn, cos, pad], axis=1)
    sink = jnp.concatenate([zn, sin, pad], axis=1)
    return jnp.concatenate([cosq, sinq, cosk, sink], axis=1)


def _layer(x_prompt, x_sample, c_prompt, c_sample, w_ada, b_ada, g_pre_mix, w_in, g_qa, w_uq, g_kva, w_ukv,
           g_out_a, g_out_b, w_o, g_post_mix, g_pre_ffn, router_w, router_bias, w_exp_gate, w_exp_up, w_exp_down,
           w_sh_gate, w_sh_up, w_sh_down, g_post_ffn):
    b1, s1, d = x_prompt.shape
    b2, s2, _ = x_sample.shape
    n1, n2 = b1 * s1, b2 * s2
    t_all = n1 + n2
    chunk = math.gcd(s1, s2)
    assert n1 % s2 == 0 and chunk % max(TM_PRE, TM_POST, TM_COMB) == 0

    x_p, x_s = x_prompt.reshape(n1, d), x_sample.reshape(n2, d)
    c_all = jnp.concatenate([c_prompt, c_sample, jnp.zeros((8 - b1 - b2, d), F32)], axis=0)
    ada = _ada(c_all, w_ada, b_ada)
    chunk_batch = jnp.concatenate([jnp.repeat(jnp.arange(b1), s1 // chunk),
                                   b1 + jnp.repeat(jnp.arange(b2), s2 // chunk)])
    ada8 = ada.reshape(8, 6, d)[chunk_batch]

    win2, wuq2, wukv2 = _prep_weights(w_in, w_uq, w_ukv)
    tab = _rope_table(max(s1, s2))
    r2 = lambda g: g.reshape(1, -1)
    qa, ka, va, *dil_qkv = _premix(x_p, x_s, ada8, chunk, r2(g_pre_mix), win2, r2(g_qa), wuq2, r2(g_kva), wukv2, tab,
                                   n1, s1, s2)

    oa_p, oa_s = _mla(qa, ka, va, 0, b1, s1), _mla(qa, ka, va, n1, b2, s2)
    os_, ls_ = [], []
    for p, dil in enumerate(DILATIONS):
        o, lse = _dilated(*dil_qkv[3 * p:3 * p + 3], dil, n1, s1, s2)
        os_.append(o)
        ls_.append(lse)

    x1, h2p, sh, idx_t, gate_t, rank_t, cnt = _postmix(
        x_p, x_s, oa_p, oa_s, os_, ls_, ada8, chunk, r2(g_out_a), r2(g_out_b), w_o.astype(BF16), r2(g_post_mix), r2(g_pre_ffn),
        router_w.T, router_bias.reshape(N_EXPERTS, 1), w_sh_gate.astype(BF16), w_sh_up.astype(BF16),
        w_sh_down.astype(BF16))

    blk = EXPERT_BLOCK
    n_assign = t_all * TOP_K
    n_blocks = -(-(n_assign + N_EXPERTS * (blk - 1)) // blk)
    counts = cnt[:, 0].astype(jnp.int32)
    padded = (counts + blk - 1) // blk * blk
    pad_end = jnp.cumsum(padded)
    pad_start = pad_end - padded
    dest = _dest(idx_t, rank_t, pad_start)
    block_first = jnp.arange(n_blocks, dtype=jnp.int32) * blk
    block_expert = jnp.minimum(jnp.sum((pad_end[None, :] <= block_first[:, None]).astype(jnp.int32), axis=1),
                               N_EXPERTS - 1)
    n_used = (pad_end[-1] // blk).astype(jnp.int32).reshape(1)
    block_valid = jnp.clip(counts[block_expert] - (block_first - pad_start[block_expert]), 0, blk)
    eid = jnp.arange(N_EXPERTS, dtype=jnp.int32)
    used_from = lax.cummin(jnp.where(counts > 0, eid, N_EXPERTS), axis=0, reverse=True)
    next_used = jnp.concatenate([used_from[1:], jnp.full((1,), N_EXPERTS, jnp.int32)])
    block_next = jnp.where(next_used < N_EXPERTS, next_used, -1)[block_expert]
    dest_flat = dest.reshape(n_assign)
    x_slots = _dispatch_rows(h2p, dest_flat, n_blocks * blk)
    y_slots = _experts(block_expert, block_valid, block_next, n_used, x_slots, w_exp_gate, w_exp_up, w_exp_down)
    outs = []
    for off, n_g in ((0, n1), (n1, n2)):
        yg = _gather_rows(y_slots, dest[:, off:off + n_g].reshape(TOP_K * n_g)).reshape(TOP_K, n_g, d // 2)
        outs.append(_combine(yg, gate_t.T, sh, x1, ada8, chunk, r2(g_post_ffn), off))
    return outs[0].reshape(b1, s1, d), outs[1].reshape(b2, s2, d)


def kernel(x_prompt, x_sample, c_prompt, c_sample, w_ada, b_ada, g_pre_mix, w_in, g_qa, w_uq, g_kva, w_ukv, g_out_a, g_out_b, w_o, g_post_mix, g_pre_ffn, router_w, router_bias, w_exp_gate, w_exp_up, w_exp_down, w_sh_gate, w_sh_up, w_sh_down, g_post_ffn):
    layer = [p[0] for p in (w_ada, b_ada, g_pre_mix, w_in, g_qa, w_uq, g_kva, w_ukv, g_out_a, g_out_b, w_o,
                            g_post_mix, g_pre_ffn, router_w, router_bias, w_exp_gate, w_exp_up, w_exp_down,
                            w_sh_gate, w_sh_up, w_sh_down, g_post_ffn)]
    return _layer(x_prompt, x_sample, c_prompt, c_sample, *layer)
```

```python
import functools
import math

import jax
import jax.numpy as jnp
from jax import lax
from jax.experimental import pallas as pl
from jax.experimental.pallas import tpu as pltpu
from jax.experimental.pallas import tpu_sc as plsc

F32 = jnp.float32
BF16 = jnp.bfloat16

D_MODEL = 1024
HEAD_DIM = 64
N_HEADS = 8
Q_LORA = 256
KV_LORA = 128
QK_NOPE = 64
QK_ROPE = 32
ROPE_BASE = 10000.0
DIL_PATTERNS = ((128, 1), (512, 4), (2048, 16))
DILATIONS = tuple(d for _, d in DIL_PATTERNS)
DIL_HALF = 64
assert all(w // (2 * d) == DIL_HALF for w, d in DIL_PATTERNS) and DILATIONS[0] == 1
N_EXPERTS = 256
TOP_K = 8
N_GROUPS = 8
GROUP_SIZE = N_EXPERTS // N_GROUPS
TOPK_GROUPS = 4
D_EXPERT = 256
ROUTE_SCALE = 2.5
EPS = 1e-6
NEG = -1e30

LANES = 128
SUBLANES = 8
HEAD_PAD = 128
IN_CQ = 0
IN_CKV = IN_CQ + Q_LORA
IN_ROPE = IN_CKV + KV_LORA
IN_DIL = IN_ROPE + 2 * HEAD_PAD

TM_PRE = 512
TM_POST = 512
TM_COMB = 512
TQ_MLA = 256
TK_MLA = 512
TM_DEST = 2048
SC_WINDOW = 128
TQ_DIL = 512
DIL_SUB = 128
EXPERT_BLOCK = 512
RING = 3
VMEM_LIMIT = 56 * 1024 * 1024


def _cparams(sem):
    return pltpu.CompilerParams(dimension_semantics=sem, vmem_limit_bytes=VMEM_LIMIT)


def _rms(x, g):
    return x * lax.rsqrt(jnp.mean(x * x, axis=-1, keepdims=True) + EPS) * g


def _sigmoid(x):
    return 1.0 / (1.0 + jnp.exp(-x))


def _dot(a, b):
    return jnp.dot(a, b, preferred_element_type=F32)


def _group_specs(tm, width, nt1):
    return [pl.BlockSpec((tm, width), lambda i: (jnp.minimum(i, nt1 - 1), 0)),
            pl.BlockSpec((tm, width), lambda i: (jnp.maximum(i - nt1, 0), 0))]


def _group_tile(first_ref, second_ref, nt1):
    return jnp.where(pl.program_id(0) < nt1, first_ref[...], second_ref[...])


def _pack_rows(x):
    n = x.shape[1] // 2
    bits = lax.bitcast_convert_type(x.astype(BF16).astype(F32), jnp.int32)
    return bits[:, :n] | lax.shift_right_logical(bits[:, n:], 16)


def _unpack_rows(u):
    hi = lax.bitcast_convert_type(u & jnp.int32(-65536), F32)
    lo = lax.bitcast_convert_type(lax.shift_left(u, 16), F32)
    return hi, lo


def _dot_nt(a, b):
    return lax.dot_general(a, b, (((1,), (1,)), ((), ())), preferred_element_type=F32)


def _ada_kernel(c_ref, w_ref, b_ref, o_ref):
    c = c_ref[...]
    s = c * _sigmoid(c)
    o_ref[...] = jnp.dot(s, w_ref[...], preferred_element_type=F32, precision=lax.Precision.HIGHEST) + b_ref[...]


def _ada(c_all, w_ada, b_ada):
    nb, d = c_all.shape
    n_out = w_ada.shape[1]
    tn = D_MODEL
    return pl.pallas_call(
        _ada_kernel,
        out_shape=jax.ShapeDtypeStruct((nb, n_out), F32),
        grid=(n_out // tn,),
        in_specs=[pl.BlockSpec((nb, d), lambda j: (0, 0)),
                  pl.BlockSpec((d, tn), lambda j: (0, j)),
                  pl.BlockSpec((1, tn), lambda j: (0, j))],
        out_specs=pl.BlockSpec((nb, tn), lambda j: (0, j)),
        compiler_params=_cparams(("arbitrary",)),
        name="ada",
    )(c_all, w_ada, b_ada.reshape(1, n_out))


def _premix_kernel(xp_ref, xs_ref, ada_ref, g_ref, win_ref, gqa_ref, wuq_ref, gkva_ref, wukv_ref, tab_ref,
                   qa_ref, ka_ref, va_ref, *rest, nt1):
    dil_refs, zs_ref = rest[:-1], rest[-1]
    x = _group_tile(xp_ref, xs_ref, nt1)
    sh1 = ada_ref[0, 0:1, :]
    sc1 = ada_ref[0, 1:2, :]
    h = _rms(x, g_ref[...]) * (1.0 + sc1) + sh1
    z = _dot(h.astype(BF16), win_ref[...])
    tab = tab_ref[...]
    cosq, sinq = tab[:, 0:128], tab[:, 128:256]
    cosk, sink = tab[:, 256:384], tab[:, 384:512]
    nh = N_HEADS * HEAD_PAD

    cq = _rms(z[:, IN_CQ:IN_CQ + Q_LORA], gqa_ref[...]).astype(BF16)
    qq = _dot(cq, wuq_ref[...])
    for h_i in range(N_HEADS):
        lo = h_i * HEAD_PAD
        qa_ref[:, lo:lo + HEAD_PAD] = (qq[:, lo:lo + HEAD_PAD] * cosq
                                       + qq[:, nh + lo:nh + lo + HEAD_PAD] * sinq).astype(BF16)

    ckv = _rms(z[:, IN_CKV:IN_CKV + KV_LORA], gkva_ref[...]).astype(BF16)
    kk = _dot(ckv, wukv_ref[...])
    rr = (z[:, IN_ROPE:IN_ROPE + HEAD_PAD] * cosk
          + z[:, IN_ROPE + HEAD_PAD:IN_ROPE + 2 * HEAD_PAD] * sink)
    for h_i in range(N_HEADS):
        lo = h_i * HEAD_PAD
        ka_ref[:, lo:lo + HEAD_PAD] = (kk[:, lo:lo + HEAD_PAD] + rr).astype(BF16)
    low = lax.broadcasted_iota(jnp.int32, (x.shape[0], LANES), 1) < HEAD_DIM
    for j in range(N_HEADS // 2):
        vpair = kk[:, nh + j * LANES:nh + (j + 1) * LANES]
        va_ref[:, (2 * j) * LANES:(2 * j + 1) * LANES] = jnp.where(low, vpair, 1.0).astype(BF16)
        va_ref[:, (2 * j + 1) * LANES:(2 * j + 2) * LANES] = jnp.where(low, 1.0, vpair).astype(BF16)

    tm = x.shape[0]
    wd = N_HEADS * HEAD_DIM
    n_slab = 3 * wd // LANES
    for c in range(n_slab):
        scale = HEAD_DIM ** -0.5 * math.log2(math.e) if c < wd // LANES else 1.0
        zs_ref[c] = z[:, IN_DIL + c * LANES:IN_DIL + (c + 1) * LANES] * scale
    for dil, refs in zip(DILATIONS, (dil_refs[0:3], dil_refs[3:6], dil_refs[6:9])):
        n = tm // dil
        for r in range(dil):
            for c in range(n_slab):
                rows = zs_ref[c] if dil == 1 else zs_ref.at[c][pl.ds(r, n, stride=dil), :]
                col = r * wd + (c % (wd // LANES)) * LANES
                refs[c // (wd // LANES)][:, col:col + LANES] = rows.astype(BF16)


def _premix(x_p, x_s, ada8, chunk, g_pre, win2, g_qa, wuq2, g_kva, wukv2, tab, n1, s1, s2):
    t_all = x_p.shape[0] + x_s.shape[0]
    tm = TM_PRE
    nt1 = n1 // tm
    tb1, tb2 = s1 // tm, s2 // tm

    def tab_map(i):
        return (jnp.where(i < nt1, i % tb1, (i - nt1) % tb2), 0)

    row = lambda i: (i, 0)
    const = lambda i: (0, 0)
    wd = N_HEADS * HEAD_DIM
    outs = [jax.ShapeDtypeStruct((t_all, N_HEADS * HEAD_PAD), BF16)] * 3
    out_specs = [pl.BlockSpec((tm, N_HEADS * HEAD_PAD), row)] * 3
    for dil in DILATIONS:
        outs += [jax.ShapeDtypeStruct((t_all // dil, dil * wd), BF16)] * 3
        out_specs += [pl.BlockSpec((tm // dil, dil * wd), row)] * 3
    return pl.pallas_call(
        functools.partial(_premix_kernel, nt1=nt1),
        out_shape=outs,
        grid=(t_all // tm,),
        in_specs=_group_specs(tm, D_MODEL, nt1) + [
                  pl.BlockSpec((1, 6, D_MODEL), lambda i: (i * tm // chunk, 0, 0)),
                  pl.BlockSpec((1, D_MODEL), const),
                  pl.BlockSpec(win2.shape, const),
                  pl.BlockSpec((1, Q_LORA), const),
                  pl.BlockSpec(wuq2.shape, const),
                  pl.BlockSpec((1, KV_LORA), const),
                  pl.BlockSpec(wukv2.shape, const),
                  pl.BlockSpec((tm, 512), tab_map)],
        out_specs=out_specs,
        scratch_shapes=[pltpu.VMEM((3 * wd // LANES, tm, LANES), F32)],
        compiler_params=_cparams(("parallel",)),
        name="premix",
    )(x_p, x_s, ada8, g_pre, win2, g_qa, wuq2, g_kva, wukv2, tab)


def _mla_kernel(q_ref, k_ref, v_ref, o_ref, s_ref, m_ref, *, tk):
    tq, seq = s_ref.shape[1], s_ref.shape[2]
    lane = lax.broadcasted_iota(jnp.int32, (tq, LANES), 1)

    def scores(hh):
        s = _dot_nt(q_ref[:, hh * HEAD_PAD:(hh + 1) * HEAD_PAD], k_ref[:, hh * HEAD_PAD:(hh + 1) * HEAD_PAD])
        s_ref[hh] = s
        m_ref[hh] = jnp.max(s, axis=-1, keepdims=True)

    def values(hh):
        acc = jnp.zeros((tq, LANES), F32)
        m = m_ref[hh]
        for c0 in range(0, seq, tk):
            p = jnp.exp2(s_ref[hh, :, c0:c0 + tk] - m)
            acc = acc + _dot(p.astype(BF16), v_ref[c0:c0 + tk, hh * LANES:(hh + 1) * LANES])
        return acc / pltpu.roll(acc, HEAD_DIM, axis=1)

    scores(0)
    scores(1)
    o0 = values(0)
    o1 = values(1)
    o_ref[...] = jnp.where(lane < HEAD_DIM, o0, o1).astype(o_ref.dtype)


def _mla(qa, ka, va, row_off, batch, seq):
    tq = TQ_MLA
    nq = seq // tq
    qoff = row_off // tq
    soff = row_off // seq
    return pl.pallas_call(
        functools.partial(_mla_kernel, tk=TK_MLA),
        out_shape=jax.ShapeDtypeStruct((batch * seq, N_HEADS * HEAD_DIM), BF16),
        grid=(batch, N_HEADS // 2, nq),
        in_specs=[pl.BlockSpec((tq, 2 * HEAD_PAD), lambda b, j, qi: (qoff + b * nq + qi, j)),
                  pl.BlockSpec((seq, 2 * HEAD_PAD), lambda b, j, qi: (soff + b, j)),
                  pl.BlockSpec((seq, 2 * LANES), lambda b, j, qi: (soff + b, j))],
        out_specs=pl.BlockSpec((tq, LANES), lambda b, j, qi: (b * nq + qi, j)),
        scratch_shapes=[pltpu.VMEM((2, tq, seq), F32), pltpu.VMEM((2, tq, 1), F32)],
        compiler_params=_cparams(("parallel", "parallel", "parallel")),
        name="mla_attention",
    )(qa, ka, va)


def _dil_kernel(q_ref, kp_ref, kc_ref, kn_ref, vp_ref, vc_ref, vn_ref, bias_ref, o_ref, lse_ref, *,
                n1_rows, len1, len2):
    tq = q_ref.shape[0]
    sub = bias_ref.shape[2]
    w = bias_ref.shape[3]
    row0 = pl.program_id(0) * tq
    in1 = row0 < n1_rows
    seq_len = jnp.where(in1, len1, len2)
    q0 = jnp.where(in1, row0 % len1, (row0 - n1_rows) % len2)

    kw = jnp.concatenate([kp_ref[...], kc_ref[...], kn_ref[...]], axis=0)
    vw = jnp.concatenate([vp_ref[...], vc_ref[...], vn_ref[...]], axis=0)
    low = lax.broadcasted_iota(jnp.int32, (sub, LANES), 1) < HEAD_DIM
    low_w = lax.broadcasted_iota(jnp.int32, (w, LANES), 1) < HEAD_DIM
    chains = [(t, h) for t in range(tq // sub) for h in range(N_HEADS)]
    lanes = lambda h: slice((h // 2) * LANES, (h // 2 + 1) * LANES)
    starts = [(q0 + t * sub) % seq_len for t in range(tq // sub)]
    variant = [(st == 0).astype(jnp.int32) + 2 * (st + sub == seq_len).astype(jnp.int32) for st in starts]
    qs = [q_ref[t * sub:(t + 1) * sub, lanes(h)] for t, h in chains]
    qs = [jnp.where(low, q, 0) if h % 2 == 0 else jnp.where(low, 0, q) for q, (t, h) in zip(qs, chains)]
    vs = [vw[t * sub:t * sub + w, lanes(h)] for t, h in chains]
    vs = [jnp.where(low_w, v, 1) if h % 2 == 0 else jnp.where(low_w, 1, v) for v, (t, h) in zip(vs, chains)]
    ss = [_dot_nt(q.astype(BF16), kw[t * sub:t * sub + w, lanes(h)]) + bias_ref[variant[t], h]
          for q, (t, h) in zip(qs, chains)]
    ms = [jnp.max(s, axis=-1, keepdims=True) for s in ss]
    ps = [jnp.exp2(s - m).astype(BF16) for s, m in zip(ss, ms)]
    accs = [_dot(p, v.astype(BF16)) for p, v in zip(ps, vs)]
    ls = [pltpu.roll(acc, HEAD_DIM, axis=1) for acc in accs]
    outs = [acc / l for acc, l in zip(accs, ls)]
    lses = [m * math.log(2.0) + jnp.log(l) for m, l in zip(ms, ls)]
    for c in range(0, len(chains), 2):
        t, h = chains[c]
        o_ref[t * sub:(t + 1) * sub, lanes(h)] = jnp.where(low, outs[c], outs[c + 1]).astype(o_ref.dtype)
        lse_ref[t * sub:(t + 1) * sub, lanes(h)] = jnp.where(low, lses[c], lses[c + 1])


def _dil_bias(dil, tq):
    half = DIL_HALF
    col = jnp.arange(tq + 2 * half)[None, :]
    dist = jnp.abs(col - half - jnp.arange(tq)[:, None])
    slopes = 2.0 ** (-8.0 * jnp.arange(1, N_HEADS + 1, dtype=F32) / N_HEADS)
    alibi = -math.log2(math.e) * slopes[:, None, None] * (dil * dist).astype(F32)[None]
    band = dist <= half
    first, last = col >= half, col < tq + half
    masks = [band, band & first, band & last, band & first & last]
    return jnp.stack([jnp.where(mk[None], alibi, NEG) for mk in masks])


def _dilated(qv, kv, vv, dil, n1, s1, s2):
    rows = qv.shape[0]
    tq = TQ_DIL
    nt = rows // tq
    wk = N_HEADS * HEAD_DIM
    cur = lambda i, r: (i, r)
    per = tq // DIL_HALF
    prv = lambda i, r: (jnp.maximum(i * per - 1, 0), r)
    nxt = lambda i, r: (jnp.minimum((i + 1) * per, nt * per - 1), r)
    kspecs = [pl.BlockSpec((DIL_HALF, wk), prv), pl.BlockSpec((tq, wk), cur), pl.BlockSpec((DIL_HALF, wk), nxt)]
    bias = _dil_bias(dil, DIL_SUB)
    assert tq % DIL_SUB == 0 and (s1 // dil) % DIL_SUB == 0 and (s2 // dil) % DIL_SUB == 0
    assert (n1 // dil) % tq == 0 and rows % tq == 0
    return pl.pallas_call(
        functools.partial(_dil_kernel, n1_rows=n1 // dil, len1=s1 // dil, len2=s2 // dil),
        out_shape=[jax.ShapeDtypeStruct((rows, dil * wk), BF16), jax.ShapeDtypeStruct((rows, dil * wk), F32)],
        grid=(nt, dil),
        in_specs=[pl.BlockSpec((tq, wk), cur)] + kspecs + kspecs
                 + [pl.BlockSpec(bias.shape, lambda i, r: (0, 0, 0, 0))],
        out_specs=[pl.BlockSpec((tq, wk), cur), pl.BlockSpec((tq, wk), cur)],
        compiler_params=_cparams(("parallel", "parallel")),
        name="dilated_attention_d%d" % dil,
    )(qv, kv, kv, kv, vv, vv, vv, bias)


def _postmix_kernel(xp_ref, xs_ref, oap_ref, oas_ref, o1_ref, o2_ref, o3_ref, l1_ref, l2_ref, l3_ref, ada_ref, goa_ref, gob_ref,
                    wo_ref, gpm_ref, gpf_ref, rwh_ref, rwl_ref, rb_ref, before_ref,
                    x1_ref, h2_ref, idx_ref, gate_ref, rank_ref, cnt_ref, carry_ref, nat_ref, *, nt1):
    tm = xp_ref.shape[0]

    @pl.when(pl.program_id(0) == 0)
    def _():
        carry_ref[...] = jnp.zeros_like(carry_ref)

    gt1, sh2, sc2 = ada_ref[0, 2:3, :], ada_ref[0, 3:4, :], ada_ref[0, 4:5, :]

    wd = N_HEADS * HEAD_DIM
    n_slab = wd // LANES

    def row_order(src_ref, dil, base):
        if dil == 1:
            return src_ref[...].astype(F32)
        n = tm // dil
        for r in range(dil):
            for c in range(n_slab):
                col = r * wd + c * LANES
                nat_ref.at[base + c][pl.ds(r, n, stride=dil), :] = src_ref[:, col:col + LANES].astype(F32)
        return jnp.concatenate([nat_ref[base + c] for c in range(n_slab)], axis=1)

    o_pat = [row_order(ref, dil, (2 * p) * n_slab) for p, (ref, dil) in enumerate(zip((o1_ref, o2_ref, o3_ref), DILATIONS))]
    la, lb, lc = [row_order(ref, dil, (2 * p + 1) * n_slab)
                  for p, (ref, dil) in enumerate(zip((l1_ref, l2_ref, l3_ref), DILATIONS))]

    mx = jnp.maximum(jnp.maximum(la, lb), lc)
    ea, eb, ec = jnp.exp(la - mx), jnp.exp(lb - mx), jnp.exp(lc - mx)
    ob = (ea * o_pat[0] + eb * o_pat[1] + ec * o_pat[2]) / (ea + eb + ec)

    na = _rms(_group_tile(oap_ref, oas_ref, nt1).astype(F32), goa_ref[...]).astype(BF16)
    nb = _rms(ob, gob_ref[...]).astype(BF16)
    half_w = N_HEADS * HEAD_DIM
    o = _dot(na, wo_ref[0:half_w, :]) + _dot(nb, wo_ref[half_w:2 * half_w, :])
    x1 = _group_tile(xp_ref, xs_ref, nt1) + gt1 * _rms(o, gpm_ref[...])
    x1_ref[...] = x1
    h2 = _rms(x1, gpf_ref[...]) * (1.0 + sc2) + sh2
    h2b = h2.astype(BF16)
    h2_ref[...] = _pack_rows(h2)


    h_lo = (h2 - h2b.astype(F32)).astype(BF16)
    logits = _dot_nt(rwh_ref[...], h2b) + (_dot_nt(rwh_ref[...], h_lo) + _dot_nt(rwl_ref[...], h2b))
    scores = _sigmoid(logits)
    biased = scores + rb_ref[...]
    ninf = -jnp.inf
    row = lax.broadcasted_iota(jnp.int32, (N_EXPERTS, tm), 0)
    rwg = lax.broadcasted_iota(jnp.int32, (GROUP_SIZE, tm), 0)
    gsc = []
    for gi in range(N_GROUPS):
        blk = biased[gi * GROUP_SIZE:(gi + 1) * GROUP_SIZE]
        m1 = jnp.max(blk, axis=0, keepdims=True)
        i1 = jnp.min(jnp.where(blk == m1, rwg, N_EXPERTS), axis=0, keepdims=True)
        m2 = jnp.max(jnp.where(rwg == i1, ninf, blk), axis=0, keepdims=True)
        gsc.append(m1 + m2)
    gsel = [jnp.zeros((1, tm), F32) for _ in range(N_GROUPS)]
    for _ in range(TOPK_GROUPS):
        m = functools.reduce(jnp.maximum, gsc)
        free = jnp.ones((1, tm), F32)
        for gi in range(N_GROUPS):
            hit = jnp.where(gsc[gi] == m, free, 0.0)
            free = free - hit
            gsel[gi] = gsel[gi] + hit
            gsc[gi] = jnp.where(hit > 0.0, ninf, gsc[gi])
    masked = jnp.concatenate(
        [jnp.where(gsel[gi] > 0.0, biased[gi * GROUP_SIZE:(gi + 1) * GROUP_SIZE], ninf)
         for gi in range(N_GROUPS)], axis=0)
    cur = masked
    idxs, gts = [], []
    for _ in range(TOP_K):
        m = jnp.max(cur, axis=0, keepdims=True)
        ik = jnp.min(jnp.where(cur == m, row, N_EXPERTS), axis=0, keepdims=True)
        hit = row == ik
        idxs.append(ik)
        gts.append(jnp.sum(jnp.where(hit, scores, 0.0), axis=0, keepdims=True))
        cur = jnp.where(hit, ninf, cur)
    idx = jnp.concatenate(idxs, axis=0)
    gates = jnp.concatenate(gts, axis=0)
    idx_ref[...] = idx
    gate_ref[...] = gates / jnp.sum(gates, axis=0, keepdims=True) * ROUTE_SCALE

    sel = jnp.where(cur != masked, 1.0, 0.0)
    pos = _dot(sel.astype(BF16), before_ref[...]) + carry_ref[...]
    rank_ref[...] = jnp.concatenate(
        [jnp.sum(jnp.where(row == idxs[k], pos, 0.0), axis=0, keepdims=True) for k in range(TOP_K)],
        axis=0).astype(jnp.int32)
    carry = carry_ref[...] + jnp.sum(sel, axis=1, keepdims=True)
    carry_ref[...] = carry
    cnt_ref[...] = jnp.broadcast_to(carry, cnt_ref.shape)


def _postmix(x_p, x_s, oa_p, oa_s, os_, ls_, ada8, chunk, g_oa, g_ob, wo, g_pm, g_pf, rwt, rb):
    t_all = x_p.shape[0] + x_s.shape[0]
    tm = TM_POST
    nt1 = x_p.shape[0] // tm
    rw_hi = rwt.astype(BF16)
    rw_lo = (rwt - rw_hi.astype(F32)).astype(BF16)
    before = jnp.triu(jnp.ones((tm, tm), BF16), 1)
    row = lambda i: (i, 0)
    col = lambda i: (0, i)
    const = lambda i: (0, 0)
    hw = N_HEADS * HEAD_DIM
    pat_specs = [pl.BlockSpec((tm // dil, dil * hw), row) for dil in DILATIONS]
    in_specs = (_group_specs(tm, D_MODEL, nt1) + _group_specs(tm, hw, nt1)
                + pat_specs + pat_specs
                + [pl.BlockSpec((1, 6, D_MODEL), lambda i: (i * tm // chunk, 0, 0)),
                   pl.BlockSpec((1, hw), const), pl.BlockSpec((1, hw), const),
                   pl.BlockSpec(wo.shape, const),
                   pl.BlockSpec((1, D_MODEL), const), pl.BlockSpec((1, D_MODEL), const),
                   pl.BlockSpec(rwt.shape, const), pl.BlockSpec(rwt.shape, const), pl.BlockSpec((N_EXPERTS, 1), const),
                   pl.BlockSpec((tm, tm), const)])
    out_shape = [jax.ShapeDtypeStruct((t_all, D_MODEL), F32),
                 jax.ShapeDtypeStruct((t_all, D_MODEL // 2), jnp.int32),
                 jax.ShapeDtypeStruct((TOP_K, t_all), jnp.int32),
                 jax.ShapeDtypeStruct((TOP_K, t_all), F32),
                 jax.ShapeDtypeStruct((TOP_K, t_all), jnp.int32),
                 jax.ShapeDtypeStruct((N_EXPERTS, LANES), F32)]
    out_specs = [pl.BlockSpec((tm, D_MODEL), row), pl.BlockSpec((tm, D_MODEL // 2), row),
                 pl.BlockSpec((TOP_K, tm), col), pl.BlockSpec((TOP_K, tm), col), pl.BlockSpec((TOP_K, tm), col),
                 pl.BlockSpec((N_EXPERTS, LANES), const)]
    return pl.pallas_call(
        functools.partial(_postmix_kernel, nt1=nt1),
        out_shape=out_shape,
        grid=(t_all // tm,),
        in_specs=in_specs,
        out_specs=out_specs,
        scratch_shapes=[pltpu.VMEM((N_EXPERTS, 1), F32),
                        pltpu.VMEM((2 * len(DILATIONS) * hw // LANES, tm, LANES), F32)],
        compiler_params=_cparams(("arbitrary",)),
        name="postmix_router",
    )(x_p, x_s, oa_p, oa_s, *os_, *ls_, ada8, g_oa, g_ob, wo, g_pm, g_pf, rw_hi, rw_lo, rb, before)


def _dest_kernel(idx_ref, rank_ref, start_ref, dest_ref):
    tm = idx_ref.shape[1]
    row = lax.broadcasted_iota(jnp.int32, (N_EXPERTS, tm), 0)
    start = start_ref[...]
    base = [jnp.sum(jnp.where(row == idx_ref[k:k + 1, :], start, 0), axis=0, keepdims=True) for k in range(TOP_K)]
    dest_ref[...] = jnp.concatenate(base, axis=0) + rank_ref[...]


def _dest(idx_t, rank_t, pad_start):
    t_all = idx_t.shape[1]
    tm = TM_DEST
    col = lambda i: (0, i)
    return pl.pallas_call(
        _dest_kernel,
        out_shape=jax.ShapeDtypeStruct((TOP_K, t_all), jnp.int32),
        grid=(t_all // tm,),
        in_specs=[pl.BlockSpec((TOP_K, tm), col), pl.BlockSpec((TOP_K, tm), col),
                  pl.BlockSpec((N_EXPERTS, 1), lambda i: (0, 0))],
        out_specs=pl.BlockSpec((TOP_K, tm), col),
        compiler_params=_cparams(("parallel",)),
        name="slot_index",
    )(idx_t, rank_t, pad_start.reshape(N_EXPERTS, 1))


def _sc_mesh():
    return plsc.VectorSubcoreMesh(core_axis_name="core", subcore_axis_name="subcore")


def _dispatch_rows(rows, dest, n_slots):
    t_all, width = rows.shape
    win = SC_WINDOW
    info = plsc.get_sparse_core_info()
    n_workers = info.num_cores * info.num_subcores
    per_worker = t_all // n_workers
    assert per_worker % win == 0

    @functools.partial(
        pl.kernel, out_type=jax.ShapeDtypeStruct((n_slots, width), rows.dtype), mesh=_sc_mesh(),
        scratch_types=[pltpu.VMEM((TOP_K, win), jnp.int32), pltpu.VMEM((win, width), rows.dtype),
                       pltpu.SemaphoreType.DMA])
    def scatter_kernel(x_hbm, i_hbm, o_hbm, idx_v, rows_v, sem):
        base = (lax.axis_index("subcore") * info.num_cores + lax.axis_index("core")) * per_worker

        @pl.loop(0, per_worker // win)
        def _(w):
            off = base + w * win
            for k in range(TOP_K):
                pltpu.sync_copy(i_hbm.at[pl.ds(k * t_all + off, win)], idx_v.at[k])
            pltpu.sync_copy(x_hbm.at[pl.ds(off, win)], rows_v)
            copies = [pltpu.async_copy(rows_v, o_hbm.at[idx_v.at[k]], sem) for k in range(TOP_K)]
            for c in copies:
                c.wait()

    return scatter_kernel(rows, dest)


def _gather_rows(slots, index):
    n = index.shape[0]
    width = slots.shape[1]
    win = SC_WINDOW // 2
    info = plsc.get_sparse_core_info()
    n_workers = info.num_cores * info.num_subcores
    per_worker = n // n_workers
    n_pairs = per_worker // (2 * win)
    assert per_worker % (2 * win) == 0

    @functools.partial(
        pl.kernel, out_type=jax.ShapeDtypeStruct((n, width), slots.dtype), mesh=_sc_mesh(),
        scratch_types=[pltpu.VMEM((win,), jnp.int32), pltpu.VMEM((win,), jnp.int32),
                       pltpu.VMEM((win, width), slots.dtype), pltpu.VMEM((win, width), slots.dtype),
                       pltpu.SemaphoreType.DMA, pltpu.SemaphoreType.DMA])
    def gather_kernel(y_hbm, i_hbm, o_hbm, idx_a, idx_b, rows_a, rows_b, sem_a, sem_b):
        base = (lax.axis_index("subcore") * info.num_cores + lax.axis_index("core")) * per_worker

        def start(w, idx_v, rows_v, sem):
            pltpu.sync_copy(i_hbm.at[pl.ds(base + w * win, win)], idx_v)
            pltpu.async_copy(y_hbm.at[idx_v], rows_v, sem)

        def finish(w, idx_v, rows_v, sem):
            pltpu.make_async_copy(y_hbm.at[idx_v], rows_v, sem).wait()
            pltpu.sync_copy(rows_v, o_hbm.at[pl.ds(base + w * win, win)])

        start(0, idx_a, rows_a, sem_a)

        @pl.loop(0, n_pairs)
        def _(j):
            start(2 * j + 1, idx_b, rows_b, sem_b)
            finish(2 * j, idx_a, rows_a, sem_a)

            @pl.when(j + 1 < n_pairs)
            def _():
                start(2 * j + 2, idx_a, rows_a, sem_a)

            finish(2 * j + 1, idx_b, rows_b, sem_b)

    return gather_kernel(slots, index)


def _expert_kernel(be_ref, bv_ref, nx_ref, nu_ref, x_hbm, wg_hbm, wu_hbm, wd_hbm, y_hbm,
                   x_buf, y_buf, wg_f, wu_f, wd_f, wgu_s, wd_s, sems, x_sems, y_sems):
    i = pl.program_id(0)
    n_used = nu_ref[0]
    used = i < n_used
    expert = be_ref[i]
    new_expert = (i == 0) | (expert != be_ref[jnp.maximum(i - 1, 0)])
    blk = x_buf.shape[1]
    slot = lax.rem(i, RING)

    def x_copy(b):
        s = lax.rem(b, RING)
        return pltpu.make_async_copy(x_hbm.at[pl.ds(pl.multiple_of(b * blk, blk), blk)], x_buf.at[s], x_sems.at[s])

    def y_copy(b):
        s = lax.rem(b, RING)
        return pltpu.make_async_copy(y_buf.at[s], y_hbm.at[pl.ds(pl.multiple_of(b * blk, blk), blk)], y_sems.at[s])

    def weight_copies(e):
        return (pltpu.make_async_copy(wg_hbm.at[e], wg_f, sems.at[0]),
                pltpu.make_async_copy(wu_hbm.at[e], wu_f, sems.at[1]),
                pltpu.make_async_copy(wd_hbm.at[e], wd_f, sems.at[2]))

    @pl.when(used & (i == 0))
    def _():
        for c in weight_copies(expert):
            c.start()
        for b in range(RING - 1):
            @pl.when(b < n_used)
            def _():
                x_copy(b).start()

    @pl.when(i + RING - 1 < n_used)
    def _():
        x_copy(i + RING - 1).start()

    @pl.when(used & new_expert)
    def _():
        for c in weight_copies(expert):
            c.wait()
        wgu_s[:, 0:D_EXPERT] = wg_f[...].astype(BF16)
        wgu_s[:, D_EXPERT:2 * D_EXPERT] = wu_f[...].astype(BF16)
        wd_s[...] = wd_f[...].astype(BF16)
        nxt = nx_ref[i]

        @pl.when(nxt >= 0)
        def _():
            for c in weight_copies(nxt):
                c.start()

    @pl.when(used)
    def _():
        x_copy(i).wait()

        @pl.when(i >= RING)
        def _():
            y_copy(i - RING).wait()

        def ffn(rows):
            live = lax.broadcasted_iota(jnp.int32, (rows, D_MODEL // 2), 0) < bv_ref[i]
            xa, xb = _unpack_rows(jnp.where(live, x_buf[slot, 0:rows, :], 0))
            half = D_MODEL // 2
            gu = _dot(xa.astype(BF16), wgu_s[0:half, :]) + _dot(xb.astype(BF16), wgu_s[half:D_MODEL, :])
            g, u = gu[:, 0:D_EXPERT], gu[:, D_EXPERT:2 * D_EXPERT]
            y_buf[slot, 0:rows, :] = _pack_rows(_dot((g * _sigmoid(g) * u).astype(BF16), wd_s[...]))

        @pl.when(bv_ref[i] > blk // 2)
        def _():
            ffn(blk)

        @pl.when(bv_ref[i] <= blk // 2)
        def _():
            ffn(blk // 2)
            y_buf[slot, blk // 2:blk, :] = jnp.zeros((blk // 2, D_MODEL // 2), jnp.int32)

        y_copy(i).start()

    @pl.when(i == n_used - 1)
    def _():
        for back in range(RING):
            @pl.when(i - back >= 0)
            def _():
                y_copy(i - back).wait()


def _experts(block_expert, block_valid, block_next, n_used, x_slots, wg, wu, wd):
    p_rows = x_slots.shape[0]
    blk = EXPERT_BLOCK
    hbm = pl.BlockSpec(memory_space=pl.ANY)
    grid_spec = pltpu.PrefetchScalarGridSpec(
        num_scalar_prefetch=4,
        grid=(p_rows // blk,),
        in_specs=[hbm, hbm, hbm, hbm],
        out_specs=hbm,
        scratch_shapes=[pltpu.VMEM((RING, blk, D_MODEL // 2), jnp.int32),
                        pltpu.VMEM((RING, blk, D_MODEL // 2), jnp.int32),
                        pltpu.VMEM((D_MODEL, D_EXPERT), F32), pltpu.VMEM((D_MODEL, D_EXPERT), F32),
                        pltpu.VMEM((D_EXPERT, D_MODEL), F32),
                        pltpu.VMEM((D_MODEL, 2 * D_EXPERT), BF16), pltpu.VMEM((D_EXPERT, D_MODEL), BF16),
                        pltpu.SemaphoreType.DMA((3,)), pltpu.SemaphoreType.DMA((RING,)),
                        pltpu.SemaphoreType.DMA((RING,))])
    return pl.pallas_call(
        _expert_kernel,
        out_shape=jax.ShapeDtypeStruct((p_rows, D_MODEL // 2), jnp.int32),
        grid_spec=grid_spec,
        compiler_params=_cparams(("arbitrary",)),
        name="expert_ffn",
    )(block_expert, block_valid, block_next, n_used, x_slots, wg, wu, wd)


def _combine_kernel(yg_ref, gate_ref, h2_ref, x1_ref, ada_ref, g_ref, wsg_ref, wsu_ref, wsd_ref, y_ref):
    gates = gate_ref[...]
    half = D_MODEL // 2
    ha, hb = [v.astype(BF16) for v in _unpack_rows(h2_ref[...])]
    g = _dot(ha, wsg_ref[0:half, :]) + _dot(hb, wsg_ref[half:D_MODEL, :])
    u = _dot(ha, wsu_ref[0:half, :]) + _dot(hb, wsu_ref[half:D_MODEL, :])
    shared = _dot((g * _sigmoid(g) * u).astype(BF16), wsd_ref[...])
    acc_a = shared[:, 0:half]
    acc_b = shared[:, half:D_MODEL]
    for k in range(TOP_K):
        ya, yb = _unpack_rows(yg_ref[k])
        acc_a = acc_a + gates[:, k:k + 1] * ya
        acc_b = acc_b + gates[:, k:k + 1] * yb
    gt2 = ada_ref[0, 5:6, :]
    y_ref[...] = x1_ref[...] + gt2 * _rms(jnp.concatenate([acc_a, acc_b], axis=1), g_ref[...])


def _combine(yg, gates_t, h2p, x1, ada8, chunk, g_post, wsg, wsu, wsd, row_off):
    n_rows = yg.shape[1]
    tm = TM_COMB
    off = row_off // tm
    row = lambda i: (off + i, 0)
    const = lambda i: (0, 0)
    return pl.pallas_call(
        _combine_kernel,
        out_shape=jax.ShapeDtypeStruct((n_rows, D_MODEL), F32),
        grid=(n_rows // tm,),
        in_specs=[pl.BlockSpec((TOP_K, tm, D_MODEL // 2), lambda i: (0, i, 0)),
                  pl.BlockSpec((tm, TOP_K), row),
                  pl.BlockSpec((tm, D_MODEL // 2), row),
                  pl.BlockSpec((tm, D_MODEL), row),
                  pl.BlockSpec((1, 6, D_MODEL), lambda i: ((off + i) * tm // chunk, 0, 0)),
                  pl.BlockSpec((1, D_MODEL), const),
                  pl.BlockSpec(wsg.shape, const), pl.BlockSpec(wsu.shape, const), pl.BlockSpec(wsd.shape, const)],
        out_specs=pl.BlockSpec((tm, D_MODEL), lambda i: (i, 0)),
        compiler_params=_cparams(("parallel",)),
        name="moe_combine",
    )(yg, gates_t, h2p, x1, ada8, g_post, wsg, wsu, wsd)


def _rope_partner(w):
    half = QK_ROPE // 2
    return jnp.concatenate([-w[..., half:], w[..., :half]], axis=-1)


def _prep_weights(w_in, w_uq, w_ukv):
    d = w_in.shape[0]
    zeros = lambda r, c: jnp.zeros((r, c), F32)
    lat = Q_LORA + KV_LORA
    kr = w_in[:, lat:lat + QK_ROPE]
    r_main = jnp.concatenate([zeros(d, QK_NOPE), kr, zeros(d, HEAD_PAD - QK_NOPE - QK_ROPE)], axis=1)
    r_part = jnp.concatenate([zeros(d, QK_NOPE), _rope_partner(kr), zeros(d, HEAD_PAD - QK_NOPE - QK_ROPE)], axis=1)
    win2 = jnp.concatenate([w_in[:, :lat], r_main, r_part, w_in[:, lat + QK_ROPE:]], axis=1).astype(BF16)

    wq = w_uq.reshape(Q_LORA, N_HEADS, QK_NOPE + QK_ROPE)
    zq = jnp.zeros((Q_LORA, N_HEADS, HEAD_PAD - QK_NOPE - QK_ROPE), F32)
    q_main = jnp.concatenate([wq, zq], axis=-1).reshape(Q_LORA, N_HEADS * HEAD_PAD)
    q_part = jnp.concatenate([jnp.zeros((Q_LORA, N_HEADS, QK_NOPE), F32), _rope_partner(wq[..., QK_NOPE:]), zq],
                             axis=-1).reshape(Q_LORA, N_HEADS * HEAD_PAD)
    wuq2 = jnp.concatenate([q_main, q_part], axis=1).astype(BF16)

    wkv = w_ukv.reshape(KV_LORA, N_HEADS, QK_NOPE + HEAD_DIM)
    k_pad = jnp.concatenate([wkv[..., :QK_NOPE], jnp.zeros((KV_LORA, N_HEADS, HEAD_PAD - QK_NOPE), F32)],
                            axis=-1).reshape(KV_LORA, N_HEADS * HEAD_PAD)
    v_cols = wkv[..., QK_NOPE:].reshape(KV_LORA, N_HEADS * HEAD_DIM)
    wukv2 = jnp.concatenate([k_pad, v_cols], axis=1).astype(BF16)
    return win2, wuq2, wukv2


def _rope_table(s_max):
    half = QK_ROPE // 2
    inv = ROPE_BASE ** (-jnp.arange(half, dtype=F32) / half)
    ang = jnp.arange(s_max, dtype=F32)[:, None] * inv[None, :]
    cos = jnp.concatenate([jnp.cos(ang), jnp.cos(ang)], axis=1)
    sin = jnp.concatenate([jnp.sin(ang), jnp.sin(ang)], axis=1)
    scale = (QK_NOPE + QK_ROPE) ** -0.5 * math.log2(math.e)
    pad = jnp.zeros((s_max, HEAD_PAD - QK_NOPE - QK_ROPE), F32)
    zn = jnp.zeros((s_max, QK_NOPE), F32)
    cosq = jnp.concatenate([jnp.full((s_max, QK_NOPE), scale, F32), cos * scale, pad], axis=1)
    sinq = jnp.concatenate([zn, sin * scale, pad], axis=1)
    cosk = jnp.concatenate([zn, cos, pad], axis=1)
    sink = jnp.concatenate([zn, sin, pad], axis=1)
    return jnp.concatenate([cosq, sinq, cosk, sink], axis=1)


def _layer(x_prompt, x_sample, c_prompt, c_sample, w_ada, b_ada, g_pre_mix, w_in, g_qa, w_uq, g_kva, w_ukv,
           g_out_a, g_out_b, w_o, g_post_mix, g_pre_ffn, router_w, router_bias, w_exp_gate, w_exp_up, w_exp_down,
           w_sh_gate, w_sh_up, w_sh_down, g_post_ffn):
    b1, s1, d = x_prompt.shape
    b2, s2, _ = x_sample.shape
    n1, n2 = b1 * s1, b2 * s2
    t_all = n1 + n2
    chunk = math.gcd(s1, s2)
    assert n1 % s2 == 0 and chunk % max(TM_PRE, TM_POST, TM_COMB) == 0

    x_p, x_s = x_prompt.reshape(n1, d), x_sample.reshape(n2, d)
    nb = -(-(b1 + b2) // SUBLANES) * SUBLANES
    c_all = jnp.concatenate([c_prompt, c_sample, jnp.zeros((nb - b1 - b2, d), F32)], axis=0)
    ada = _ada(c_all, w_ada, b_ada)
    chunk_batch = jnp.concatenate([jnp.repeat(jnp.arange(b1), s1 // chunk),
                                   b1 + jnp.repeat(jnp.arange(b2), s2 // chunk)])
    ada8 = ada.reshape(nb, 6, d)[chunk_batch]

    win2, wuq2, wukv2 = _prep_weights(w_in, w_uq, w_ukv)
    tab = _rope_table(max(s1, s2))
    r2 = lambda g: g.reshape(1, -1)
    qa, ka, va, *dil_qkv = _premix(x_p, x_s, ada8, chunk, r2(g_pre_mix), win2, r2(g_qa), wuq2, r2(g_kva), wukv2, tab,
                                   n1, s1, s2)

    oa_p, oa_s = _mla(qa, ka, va, 0, b1, s1), _mla(qa, ka, va, n1, b2, s2)
    os_, ls_ = [], []
    for p, dil in enumerate(DILATIONS):
        o, lse = _dilated(*dil_qkv[3 * p:3 * p + 3], dil, n1, s1, s2)
        os_.append(o)
        ls_.append(lse)

    x1, h2p, idx_t, gate_t, rank_t, cnt = _postmix(
        x_p, x_s, oa_p, oa_s, os_, ls_, ada8, chunk, r2(g_out_a), r2(g_out_b), w_o.astype(BF16), r2(g_post_mix), r2(g_pre_ffn),
        router_w.T, router_bias.reshape(N_EXPERTS, 1))
    w_shared = [w.astype(BF16) for w in (w_sh_gate, w_sh_up, w_sh_down)]

    blk = EXPERT_BLOCK
    n_assign = t_all * TOP_K
    n_blocks = -(-(n_assign + N_EXPERTS * (blk - 1)) // blk)
    counts = cnt[:, 0].astype(jnp.int32)
    padded = (counts + blk - 1) // blk * blk
    pad_end = jnp.cumsum(padded)
    pad_start = pad_end - padded
    dest = _dest(idx_t, rank_t, pad_start)
    block_first = jnp.arange(n_blocks, dtype=jnp.int32) * blk
    block_expert = jnp.minimum(jnp.sum((pad_end[None, :] <= block_first[:, None]).astype(jnp.int32), axis=1),
                               N_EXPERTS - 1)
    n_used = (pad_end[-1] // blk).astype(jnp.int32).reshape(1)
    block_valid = jnp.clip(counts[block_expert] - (block_first - pad_start[block_expert]), 0, blk)
    eid = jnp.arange(N_EXPERTS, dtype=jnp.int32)
    used_from = lax.cummin(jnp.where(counts > 0, eid, N_EXPERTS), axis=0, reverse=True)
    next_used = jnp.concatenate([used_from[1:], jnp.full((1,), N_EXPERTS, jnp.int32)])
    block_next = jnp.where(next_used < N_EXPERTS, next_used, -1)[block_expert]
    dest_flat = dest.reshape(n_assign)
    x_slots = _dispatch_rows(h2p, dest_flat, n_blocks * blk)
    y_slots = _experts(block_expert, block_valid, block_next, n_used, x_slots, w_exp_gate, w_exp_up, w_exp_down)
    outs = []
    for off, n_g in ((0, n1), (n1, n2)):
        yg = _gather_rows(y_slots, dest[:, off:off + n_g].reshape(TOP_K * n_g)).reshape(TOP_K, n_g, d // 2)
        outs.append(_combine(yg, gate_t.T, h2p, x1, ada8, chunk, r2(g_post_ffn), *w_shared, off))
    return outs[0].reshape(b1, s1, d), outs[1].reshape(b2, s2, d)


def kernel(x_prompt, x_sample, c_prompt, c_sample, w_ada, b_ada, g_pre_mix, w_in, g_qa, w_uq, g_kva, w_ukv, g_out_a, g_out_b, w_o, g_post_mix, g_pre_ffn, router_w, router_bias, w_exp_gate, w_exp_up, w_exp_down, w_sh_gate, w_sh_up, w_sh_down, g_post_ffn):
    params = (w_ada, b_ada, g_pre_mix, w_in, g_qa, w_uq, g_kva, w_ukv, g_out_a, g_out_b, w_o, g_post_mix, g_pre_ffn,
              router_w, router_bias, w_exp_gate, w_exp_up, w_exp_down, w_sh_gate, w_sh_up, w_sh_down, g_post_ffn)
    assert all(p.shape[0] == 1 for p in params)
    return _layer(x_prompt, x_sample, c_prompt, c_sample, *[p[0] for p in params])
```

```python
import functools
import math

import jax
import jax.numpy as jnp
from jax import lax
from jax.experimental import pallas as pl
from jax.experimental.pallas import tpu as pltpu
from jax.experimental.pallas import tpu_sc as plsc

F32 = jnp.float32
BF16 = jnp.bfloat16

D_MODEL = 1024
HEAD_DIM = 64
N_HEADS = 8
Q_LORA = 256
KV_LORA = 128
QK_NOPE = 64
QK_ROPE = 32
ROPE_BASE = 10000.0
DIL_PATTERNS = ((128, 1), (512, 4), (2048, 16))
DILATIONS = tuple(d for _, d in DIL_PATTERNS)
DIL_HALF = 64
assert all(w // (2 * d) == DIL_HALF for w, d in DIL_PATTERNS) and DILATIONS[0] == 1
N_EXPERTS = 256
TOP_K = 8
N_GROUPS = 8
GROUP_SIZE = N_EXPERTS // N_GROUPS
TOPK_GROUPS = 4
D_EXPERT = 256
ROUTE_SCALE = 2.5
EPS = 1e-6
NEG = -1e30

LANES = 128
SUBLANES = 8
HEAD_PAD = 128
IN_CQ = 0
IN_CKV = IN_CQ + Q_LORA
IN_ROPE = IN_CKV + KV_LORA
IN_DIL = IN_ROPE + 2 * HEAD_PAD

TM_PRE = 512
TM_POST = 512
TM_COMB = 512
TQ_MLA = 256
TK_MLA = 512
TM_DEST = 2048
SC_WINDOW = 128
TQ_DIL = 512
DIL_SUB = 128
EXPERT_BLOCK = 512
EXPERT_PARTS = 4
RING = 4
VMEM_LIMIT = 56 * 1024 * 1024


def _cparams(sem):
    return pltpu.CompilerParams(dimension_semantics=sem, vmem_limit_bytes=VMEM_LIMIT)


def _rms(x, g):
    return x * lax.rsqrt(jnp.mean(x * x, axis=-1, keepdims=True) + EPS) * g


def _sigmoid(x):
    return 1.0 / (1.0 + jnp.exp(-x))


def _dot(a, b):
    return jnp.dot(a, b, preferred_element_type=F32)


def _group_specs(tm, width, nt1):
    return [pl.BlockSpec((tm, width), lambda i: (jnp.minimum(i, nt1 - 1), 0)),
            pl.BlockSpec((tm, width), lambda i: (jnp.maximum(i - nt1, 0), 0))]


def _group_tile(first_ref, second_ref, nt1):
    return jnp.where(pl.program_id(0) < nt1, first_ref[...], second_ref[...])


def _pack_rows(x):
    n = x.shape[1] // 2
    bits = lax.bitcast_convert_type(x.astype(BF16).astype(F32), jnp.int32)
    return bits[:, :n] | lax.shift_right_logical(bits[:, n:], 16)


def _unpack_rows(u):
    hi = lax.bitcast_convert_type(u & jnp.int32(-65536), F32)
    lo = lax.bitcast_convert_type(lax.shift_left(u, 16), F32)
    return hi, lo


def _dot_nt(a, b):
    return lax.dot_general(a, b, (((1,), (1,)), ((), ())), preferred_element_type=F32)


def _ada_kernel(c_ref, w_ref, b_ref, o_ref):
    c = c_ref[...]
    s = c * _sigmoid(c)
    o_ref[...] = jnp.dot(s, w_ref[...], preferred_element_type=F32, precision=lax.Precision.HIGHEST) + b_ref[...]


def _ada(c_all, w_ada, b_ada):
    nb, d = c_all.shape
    n_out = w_ada.shape[1]
    tn = D_MODEL
    return pl.pallas_call(
        _ada_kernel,
        out_shape=jax.ShapeDtypeStruct((nb, n_out), F32),
        grid=(n_out // tn,),
        in_specs=[pl.BlockSpec((nb, d), lambda j: (0, 0)),
                  pl.BlockSpec((d, tn), lambda j: (0, j)),
                  pl.BlockSpec((1, tn), lambda j: (0, j))],
        out_specs=pl.BlockSpec((nb, tn), lambda j: (0, j)),
        compiler_params=_cparams(("arbitrary",)),
        name="ada",
    )(c_all, w_ada, b_ada.reshape(1, n_out))


def _premix_kernel(xp_ref, xs_ref, ada_ref, g_ref, win_ref, gqa_ref, wuq_ref, gkva_ref, wukv_ref, tab_ref,
                   qa_ref, ka_ref, va_ref, *rest, nt1):
    dil_refs, zs_ref = rest[:-1], rest[-1]
    x = _group_tile(xp_ref, xs_ref, nt1)
    sh1 = ada_ref[0, 0:1, :]
    sc1 = ada_ref[0, 1:2, :]
    h = _rms(x, g_ref[...]) * (1.0 + sc1) + sh1
    z = _dot(h.astype(BF16), win_ref[...])
    tab = tab_ref[...]
    cosq, sinq = tab[:, 0:128], tab[:, 128:256]
    cosk, sink = tab[:, 256:384], tab[:, 384:512]
    nh = N_HEADS * HEAD_PAD

    cq = _rms(z[:, IN_CQ:IN_CQ + Q_LORA], gqa_ref[...]).astype(BF16)
    qq = _dot(cq, wuq_ref[...])
    for h_i in range(N_HEADS):
        lo = h_i * HEAD_PAD
        qa_ref[:, lo:lo + HEAD_PAD] = (qq[:, lo:lo + HEAD_PAD] * cosq
                                       + qq[:, nh + lo:nh + lo + HEAD_PAD] * sinq).astype(BF16)

    ckv = _rms(z[:, IN_CKV:IN_CKV + KV_LORA], gkva_ref[...]).astype(BF16)
    kk = _dot(ckv, wukv_ref[...])
    rr = (z[:, IN_ROPE:IN_ROPE + HEAD_PAD] * cosk
          + z[:, IN_ROPE + HEAD_PAD:IN_ROPE + 2 * HEAD_PAD] * sink)
    for h_i in range(N_HEADS):
        lo = h_i * HEAD_PAD
        ka_ref[:, lo:lo + HEAD_PAD] = (kk[:, lo:lo + HEAD_PAD] + rr).astype(BF16)
    low = lax.broadcasted_iota(jnp.int32, (x.shape[0], LANES), 1) < HEAD_DIM
    for j in range(N_HEADS // 2):
        vpair = kk[:, nh + j * LANES:nh + (j + 1) * LANES]
        va_ref[:, (2 * j) * LANES:(2 * j + 1) * LANES] = jnp.where(low, vpair, 1.0).astype(BF16)
        va_ref[:, (2 * j + 1) * LANES:(2 * j + 2) * LANES] = jnp.where(low, 1.0, vpair).astype(BF16)

    tm = x.shape[0]
    wd = N_HEADS * HEAD_DIM
    n_slab = 3 * wd // LANES
    for c in range(n_slab):
        scale = HEAD_DIM ** -0.5 * math.log2(math.e) if c < wd // LANES else 1.0
        zs_ref[c] = z[:, IN_DIL + c * LANES:IN_DIL + (c + 1) * LANES] * scale
    for dil, refs in zip(DILATIONS, (dil_refs[0:3], dil_refs[3:6], dil_refs[6:9])):
        n = tm // dil
        for r in range(dil):
            for c in range(n_slab):
                rows = zs_ref[c] if dil == 1 else zs_ref.at[c][pl.ds(r, n, stride=dil), :]
                col = r * wd + (c % (wd // LANES)) * LANES
                refs[c // (wd // LANES)][:, col:col + LANES] = rows.astype(BF16)


def _premix(x_p, x_s, ada8, chunk, g_pre, win2, g_qa, wuq2, g_kva, wukv2, tab, n1, s1, s2):
    t_all = x_p.shape[0] + x_s.shape[0]
    tm = TM_PRE
    nt1 = n1 // tm
    tb1, tb2 = s1 // tm, s2 // tm

    def tab_map(i):
        return (jnp.where(i < nt1, i % tb1, (i - nt1) % tb2), 0)

    row = lambda i: (i, 0)
    const = lambda i: (0, 0)
    wd = N_HEADS * HEAD_DIM
    outs = [jax.ShapeDtypeStruct((t_all, N_HEADS * HEAD_PAD), BF16)] * 3
    out_specs = [pl.BlockSpec((tm, N_HEADS * HEAD_PAD), row)] * 3
    for dil in DILATIONS:
        outs += [jax.ShapeDtypeStruct((t_all // dil, dil * wd), BF16)] * 3
        out_specs += [pl.BlockSpec((tm // dil, dil * wd), row)] * 3
    return pl.pallas_call(
        functools.partial(_premix_kernel, nt1=nt1),
        out_shape=outs,
        grid=(t_all // tm,),
        in_specs=_group_specs(tm, D_MODEL, nt1) + [
                  pl.BlockSpec((1, 6, D_MODEL), lambda i: (i * tm // chunk, 0, 0)),
                  pl.BlockSpec((1, D_MODEL), const),
                  pl.BlockSpec(win2.shape, const),
                  pl.BlockSpec((1, Q_LORA), const),
                  pl.BlockSpec(wuq2.shape, const),
                  pl.BlockSpec((1, KV_LORA), const),
                  pl.BlockSpec(wukv2.shape, const),
                  pl.BlockSpec((tm, 512), tab_map)],
        out_specs=out_specs,
        scratch_shapes=[pltpu.VMEM((3 * wd // LANES, tm, LANES), F32)],
        compiler_params=_cparams(("parallel",)),
        name="premix",
    )(x_p, x_s, ada8, g_pre, win2, g_qa, wuq2, g_kva, wukv2, tab)


def _mla_kernel(q_ref, k_ref, v_ref, o_ref, s_ref, m_ref, *, tk):
    tq, seq = s_ref.shape[1], s_ref.shape[2]
    lane = lax.broadcasted_iota(jnp.int32, (tq, LANES), 1)

    def scores(hh):
        s = _dot_nt(q_ref[:, hh * HEAD_PAD:(hh + 1) * HEAD_PAD], k_ref[:, hh * HEAD_PAD:(hh + 1) * HEAD_PAD])
        s_ref[hh] = s
        m_ref[hh] = jnp.max(s, axis=-1, keepdims=True)

    def values(hh):
        acc = jnp.zeros((tq, LANES), F32)
        m = m_ref[hh]
        for c0 in range(0, seq, tk):
            p = jnp.exp2(s_ref[hh, :, c0:c0 + tk] - m)
            acc = acc + _dot(p.astype(BF16), v_ref[c0:c0 + tk, hh * LANES:(hh + 1) * LANES])
        return acc / pltpu.roll(acc, HEAD_DIM, axis=1)

    scores(0)
    scores(1)
    o0 = values(0)
    o1 = values(1)
    o_ref[...] = jnp.where(lane < HEAD_DIM, o0, o1).astype(o_ref.dtype)


def _mla(qa, ka, va, row_off, batch, seq):
    tq = TQ_MLA
    nq = seq // tq
    qoff = row_off // tq
    soff = row_off // seq
    return pl.pallas_call(
        functools.partial(_mla_kernel, tk=TK_MLA),
        out_shape=jax.ShapeDtypeStruct((batch * seq, N_HEADS * HEAD_DIM), BF16),
        grid=(batch, N_HEADS // 2, nq),
        in_specs=[pl.BlockSpec((tq, 2 * HEAD_PAD), lambda b, j, qi: (qoff + b * nq + qi, j)),
                  pl.BlockSpec((seq, 2 * HEAD_PAD), lambda b, j, qi: (soff + b, j)),
                  pl.BlockSpec((seq, 2 * LANES), lambda b, j, qi: (soff + b, j))],
        out_specs=pl.BlockSpec((tq, LANES), lambda b, j, qi: (b * nq + qi, j)),
        scratch_shapes=[pltpu.VMEM((2, tq, seq), F32), pltpu.VMEM((2, tq, 1), F32)],
        compiler_params=_cparams(("parallel", "parallel", "parallel")),
        name="mla_attention",
    )(qa, ka, va)


def _dil_kernel(q_ref, kp_ref, kc_ref, kn_ref, vp_ref, vc_ref, vn_ref, bias_ref, o_ref, lse_ref, *,
                n1_rows, len1, len2):
    tq = q_ref.shape[0]
    sub = bias_ref.shape[2]
    w = bias_ref.shape[3]
    row0 = pl.program_id(0) * tq
    in1 = row0 < n1_rows
    seq_len = jnp.where(in1, len1, len2)
    q0 = jnp.where(in1, row0 % len1, (row0 - n1_rows) % len2)

    kw = jnp.concatenate([kp_ref[...], kc_ref[...], kn_ref[...]], axis=0)
    vw = jnp.concatenate([vp_ref[...], vc_ref[...], vn_ref[...]], axis=0)
    low = lax.broadcasted_iota(jnp.int32, (sub, LANES), 1) < HEAD_DIM
    low_w = lax.broadcasted_iota(jnp.int32, (w, LANES), 1) < HEAD_DIM
    chains = [(t, h) for t in range(tq // sub) for h in range(N_HEADS)]
    lanes = lambda h: slice((h // 2) * LANES, (h // 2 + 1) * LANES)
    starts = [(q0 + t * sub) % seq_len for t in range(tq // sub)]
    variant = [(st == 0).astype(jnp.int32) + 2 * (st + sub == seq_len).astype(jnp.int32) for st in starts]
    qs = [q_ref[t * sub:(t + 1) * sub, lanes(h)] for t, h in chains]
    qs = [jnp.where(low, q, 0) if h % 2 == 0 else jnp.where(low, 0, q) for q, (t, h) in zip(qs, chains)]
    vs = [vw[t * sub:t * sub + w, lanes(h)] for t, h in chains]
    vs = [jnp.where(low_w, v, 1) if h % 2 == 0 else jnp.where(low_w, 1, v) for v, (t, h) in zip(vs, chains)]
    ss = [_dot_nt(q.astype(BF16), kw[t * sub:t * sub + w, lanes(h)]) + bias_ref[variant[t], h]
          for q, (t, h) in zip(qs, chains)]
    ms = [jnp.max(s, axis=-1, keepdims=True) for s in ss]
    ps = [jnp.exp2(s - m).astype(BF16) for s, m in zip(ss, ms)]
    accs = [_dot(p, v.astype(BF16)) for p, v in zip(ps, vs)]
    ls = [pltpu.roll(acc, HEAD_DIM, axis=1) for acc in accs]
    outs = [acc / l for acc, l in zip(accs, ls)]
    lses = [m * math.log(2.0) + jnp.log(l) for m, l in zip(ms, ls)]
    for c in range(0, len(chains), 2):
        t, h = chains[c]
        o_ref[t * sub:(t + 1) * sub, lanes(h)] = jnp.where(low, outs[c], outs[c + 1]).astype(o_ref.dtype)
        lse_ref[t * sub:(t + 1) * sub, lanes(h)] = jnp.where(low, lses[c], lses[c + 1])


def _dil_bias(dil, tq):
    half = DIL_HALF
    col = jnp.arange(tq + 2 * half)[None, :]
    dist = jnp.abs(col - half - jnp.arange(tq)[:, None])
    slopes = 2.0 ** (-8.0 * jnp.arange(1, N_HEADS + 1, dtype=F32) / N_HEADS)
    alibi = -math.log2(math.e) * slopes[:, None, None] * (dil * dist).astype(F32)[None]
    band = dist <= half
    first, last = col >= half, col < tq + half
    masks = [band, band & first, band & last, band & first & last]
    return jnp.stack([jnp.where(mk[None], alibi, NEG) for mk in masks])


def _dilated(qv, kv, vv, dil, n1, s1, s2):
    rows = qv.shape[0]
    tq = TQ_DIL
    nt = rows // tq
    wk = N_HEADS * HEAD_DIM
    cur = lambda i, r: (i, r)
    per = tq // DIL_HALF
    prv = lambda i, r: (jnp.maximum(i * per - 1, 0), r)
    nxt = lambda i, r: (jnp.minimum((i + 1) * per, nt * per - 1), r)
    kspecs = [pl.BlockSpec((DIL_HALF, wk), prv), pl.BlockSpec((tq, wk), cur), pl.BlockSpec((DIL_HALF, wk), nxt)]
    bias = _dil_bias(dil, DIL_SUB)
    assert tq % DIL_SUB == 0 and (s1 // dil) % DIL_SUB == 0 and (s2 // dil) % DIL_SUB == 0
    assert (n1 // dil) % tq == 0 and rows % tq == 0
    return pl.pallas_call(
        functools.partial(_dil_kernel, n1_rows=n1 // dil, len1=s1 // dil, len2=s2 // dil),
        out_shape=[jax.ShapeDtypeStruct((rows, dil * wk), BF16), jax.ShapeDtypeStruct((rows, dil * wk), F32)],
        grid=(nt, dil),
        in_specs=[pl.BlockSpec((tq, wk), cur)] + kspecs + kspecs
                 + [pl.BlockSpec(bias.shape, lambda i, r: (0, 0, 0, 0))],
        out_specs=[pl.BlockSpec((tq, wk), cur), pl.BlockSpec((tq, wk), cur)],
        compiler_params=_cparams(("parallel", "parallel")),
        name="dilated_attention_d%d" % dil,
    )(qv, kv, kv, kv, vv, vv, vv, bias)


def _postmix_kernel(xp_ref, xs_ref, oap_ref, oas_ref, o1_ref, o2_ref, o3_ref, l1_ref, l2_ref, l3_ref, ada_ref, goa_ref, gob_ref,
                    wo_ref, gpm_ref, gpf_ref, rwh_ref, rwl_ref, rb_ref, before_ref,
                    x1_ref, h2_ref, idx_ref, gate_ref, rank_ref, cnt_ref, carry_ref, nat_ref, *, nt1):
    tm = xp_ref.shape[0]

    @pl.when(pl.program_id(0) == 0)
    def _():
        carry_ref[...] = jnp.zeros_like(carry_ref)

    gt1, sh2, sc2 = ada_ref[0, 2:3, :], ada_ref[0, 3:4, :], ada_ref[0, 4:5, :]

    wd = N_HEADS * HEAD_DIM
    n_slab = wd // LANES

    def row_order(src_ref, dil, base):
        if dil == 1:
            return src_ref[...].astype(F32)
        n = tm // dil
        for r in range(dil):
            for c in range(n_slab):
                col = r * wd + c * LANES
                nat_ref.at[base + c][pl.ds(r, n, stride=dil), :] = src_ref[:, col:col + LANES].astype(F32)
        return jnp.concatenate([nat_ref[base + c] for c in range(n_slab)], axis=1)

    o_pat = [row_order(ref, dil, (2 * p) * n_slab) for p, (ref, dil) in enumerate(zip((o1_ref, o2_ref, o3_ref), DILATIONS))]
    la, lb, lc = [row_order(ref, dil, (2 * p + 1) * n_slab)
                  for p, (ref, dil) in enumerate(zip((l1_ref, l2_ref, l3_ref), DILATIONS))]

    mx = jnp.maximum(jnp.maximum(la, lb), lc)
    ea, eb, ec = jnp.exp(la - mx), jnp.exp(lb - mx), jnp.exp(lc - mx)
    ob = (ea * o_pat[0] + eb * o_pat[1] + ec * o_pat[2]) / (ea + eb + ec)

    na = _rms(_group_tile(oap_ref, oas_ref, nt1).astype(F32), goa_ref[...]).astype(BF16)
    nb = _rms(ob, gob_ref[...]).astype(BF16)
    half_w = N_HEADS * HEAD_DIM
    o = _dot(na, wo_ref[0:half_w, :]) + _dot(nb, wo_ref[half_w:2 * half_w, :])
    x1 = _group_tile(xp_ref, xs_ref, nt1) + gt1 * _rms(o, gpm_ref[...])
    x1_ref[...] = x1
    h2 = _rms(x1, gpf_ref[...]) * (1.0 + sc2) + sh2
    h2b = h2.astype(BF16)
    h2_ref[...] = _pack_rows(h2)


    h_lo = (h2 - h2b.astype(F32)).astype(BF16)
    logits = _dot_nt(rwh_ref[...], h2b) + (_dot_nt(rwh_ref[...], h_lo) + _dot_nt(rwl_ref[...], h2b))
    scores = _sigmoid(logits)
    biased = scores + rb_ref[...]
    ninf = -jnp.inf
    row = lax.broadcasted_iota(jnp.int32, (N_EXPERTS, tm), 0)
    rwg = lax.broadcasted_iota(jnp.int32, (GROUP_SIZE, tm), 0)
    gsc = []
    for gi in range(N_GROUPS):
        blk = biased[gi * GROUP_SIZE:(gi + 1) * GROUP_SIZE]
        m1 = jnp.max(blk, axis=0, keepdims=True)
        i1 = jnp.min(jnp.where(blk == m1, rwg, N_EXPERTS), axis=0, keepdims=True)
        m2 = jnp.max(jnp.where(rwg == i1, ninf, blk), axis=0, keepdims=True)
        gsc.append(m1 + m2)
    gsel = [jnp.zeros((1, tm), F32) for _ in range(N_GROUPS)]
    for _ in range(TOPK_GROUPS):
        m = functools.reduce(jnp.maximum, gsc)
        free = jnp.ones((1, tm), F32)
        for gi in range(N_GROUPS):
            hit = jnp.where(gsc[gi] == m, free, 0.0)
            free = free - hit
            gsel[gi] = gsel[gi] + hit
            gsc[gi] = jnp.where(hit > 0.0, ninf, gsc[gi])
    masked = jnp.concatenate(
        [jnp.where(gsel[gi] > 0.0, biased[gi * GROUP_SIZE:(gi + 1) * GROUP_SIZE], ninf)
         for gi in range(N_GROUPS)], axis=0)
    cur = masked
    idxs, gts = [], []
    for _ in range(TOP_K):
        m = jnp.max(cur, axis=0, keepdims=True)
        ik = jnp.min(jnp.where(cur == m, row, N_EXPERTS), axis=0, keepdims=True)
        hit = row == ik
        idxs.append(ik)
        gts.append(jnp.sum(jnp.where(hit, scores, 0.0), axis=0, keepdims=True))
        cur = jnp.where(hit, ninf, cur)
    idx = jnp.concatenate(idxs, axis=0)
    gates = jnp.concatenate(gts, axis=0)
    idx_ref[...] = idx
    gate_ref[...] = gates / jnp.sum(gates, axis=0, keepdims=True) * ROUTE_SCALE

    sel = jnp.where(cur != masked, 1.0, 0.0)
    pos = _dot(sel.astype(BF16), before_ref[...]) + carry_ref[...]
    rank_ref[...] = jnp.concatenate(
        [jnp.sum(jnp.where(row == idxs[k], pos, 0.0), axis=0, keepdims=True) for k in range(TOP_K)],
        axis=0).astype(jnp.int32)
    carry = carry_ref[...] + jnp.sum(sel, axis=1, keepdims=True)
    carry_ref[...] = carry
    cnt_ref[...] = jnp.broadcast_to(carry, cnt_ref.shape)


def _postmix(x_p, x_s, oa_p, oa_s, os_, ls_, ada8, chunk, g_oa, g_ob, wo, g_pm, g_pf, rwt, rb):
    t_all = x_p.shape[0] + x_s.shape[0]
    tm = TM_POST
    nt1 = x_p.shape[0] // tm
    rw_hi = rwt.astype(BF16)
    rw_lo = (rwt - rw_hi.astype(F32)).astype(BF16)
    before = jnp.triu(jnp.ones((tm, tm), BF16), 1)
    row = lambda i: (i, 0)
    col = lambda i: (0, i)
    const = lambda i: (0, 0)
    hw = N_HEADS * HEAD_DIM
    pat_specs = [pl.BlockSpec((tm // dil, dil * hw), row) for dil in DILATIONS]
    in_specs = (_group_specs(tm, D_MODEL, nt1) + _group_specs(tm, hw, nt1)
                + pat_specs + pat_specs
                + [pl.BlockSpec((1, 6, D_MODEL), lambda i: (i * tm // chunk, 0, 0)),
                   pl.BlockSpec((1, hw), const), pl.BlockSpec((1, hw), const),
                   pl.BlockSpec(wo.shape, const),
                   pl.BlockSpec((1, D_MODEL), const), pl.BlockSpec((1, D_MODEL), const),
                   pl.BlockSpec(rwt.shape, const), pl.BlockSpec(rwt.shape, const), pl.BlockSpec((N_EXPERTS, 1), const),
                   pl.BlockSpec((tm, tm), const)])
    out_shape = [jax.ShapeDtypeStruct((t_all, D_MODEL), F32),
                 jax.ShapeDtypeStruct((t_all, D_MODEL // 2), jnp.int32),
                 jax.ShapeDtypeStruct((TOP_K, t_all), jnp.int32),
                 jax.ShapeDtypeStruct((TOP_K, t_all), F32),
                 jax.ShapeDtypeStruct((TOP_K, t_all), jnp.int32),
                 jax.ShapeDtypeStruct((N_EXPERTS, LANES), F32)]
    out_specs = [pl.BlockSpec((tm, D_MODEL), row), pl.BlockSpec((tm, D_MODEL // 2), row),
                 pl.BlockSpec((TOP_K, tm), col), pl.BlockSpec((TOP_K, tm), col), pl.BlockSpec((TOP_K, tm), col),
                 pl.BlockSpec((N_EXPERTS, LANES), const)]
    return pl.pallas_call(
        functools.partial(_postmix_kernel, nt1=nt1),
        out_shape=out_shape,
        grid=(t_all // tm,),
        in_specs=in_specs,
        out_specs=out_specs,
        scratch_shapes=[pltpu.VMEM((N_EXPERTS, 1), F32),
                        pltpu.VMEM((2 * len(DILATIONS) * hw // LANES, tm, LANES), F32)],
        compiler_params=_cparams(("arbitrary",)),
        name="postmix_router",
    )(x_p, x_s, oa_p, oa_s, *os_, *ls_, ada8, g_oa, g_ob, wo, g_pm, g_pf, rw_hi, rw_lo, rb, before)


def _dest_kernel(idx_ref, rank_ref, start_ref, dest_ref):
    tm = idx_ref.shape[1]
    row = lax.broadcasted_iota(jnp.int32, (N_EXPERTS, tm), 0)
    start = start_ref[...]
    base = [jnp.sum(jnp.where(row == idx_ref[k:k + 1, :], start, 0), axis=0, keepdims=True) for k in range(TOP_K)]
    dest_ref[...] = jnp.concatenate(base, axis=0) + rank_ref[...]


def _dest(idx_t, rank_t, pad_start):
    t_all = idx_t.shape[1]
    tm = TM_DEST
    col = lambda i: (0, i)
    return pl.pallas_call(
        _dest_kernel,
        out_shape=jax.ShapeDtypeStruct((TOP_K, t_all), jnp.int32),
        grid=(t_all // tm,),
        in_specs=[pl.BlockSpec((TOP_K, tm), col), pl.BlockSpec((TOP_K, tm), col),
                  pl.BlockSpec((N_EXPERTS, 1), lambda i: (0, 0))],
        out_specs=pl.BlockSpec((TOP_K, tm), col),
        compiler_params=_cparams(("parallel",)),
        name="slot_index",
    )(idx_t, rank_t, pad_start.reshape(N_EXPERTS, 1))


def _sc_mesh():
    return plsc.VectorSubcoreMesh(core_axis_name="core", subcore_axis_name="subcore")


def _dispatch_rows(rows, dest, n_slots):
    t_all, width = rows.shape
    win = SC_WINDOW
    info = plsc.get_sparse_core_info()
    n_workers = info.num_cores * info.num_subcores
    per_worker = t_all // n_workers
    assert per_worker % win == 0

    @functools.partial(
        pl.kernel, out_type=jax.ShapeDtypeStruct((n_slots, width), rows.dtype), mesh=_sc_mesh(),
        scratch_types=[pltpu.VMEM((TOP_K, win), jnp.int32), pltpu.VMEM((win, width), rows.dtype),
                       pltpu.SemaphoreType.DMA])
    def scatter_kernel(x_hbm, i_hbm, o_hbm, idx_v, rows_v, sem):
        base = (lax.axis_index("subcore") * info.num_cores + lax.axis_index("core")) * per_worker

        @pl.loop(0, per_worker // win)
        def _(w):
            off = base + w * win
            for k in range(TOP_K):
                pltpu.sync_copy(i_hbm.at[pl.ds(k * t_all + off, win)], idx_v.at[k])
            pltpu.sync_copy(x_hbm.at[pl.ds(off, win)], rows_v)
            copies = [pltpu.async_copy(rows_v, o_hbm.at[idx_v.at[k]], sem) for k in range(TOP_K)]
            for c in copies:
                c.wait()

    return scatter_kernel(rows, dest)


def _gather_rows(slots, index):
    n = index.shape[0]
    width = slots.shape[1]
    win = SC_WINDOW // 2
    info = plsc.get_sparse_core_info()
    n_workers = info.num_cores * info.num_subcores
    per_worker = n // n_workers
    n_pairs = per_worker // (2 * win)
    assert per_worker % (2 * win) == 0

    @functools.partial(
        pl.kernel, out_type=jax.ShapeDtypeStruct((n, width), slots.dtype), mesh=_sc_mesh(),
        scratch_types=[pltpu.VMEM((win,), jnp.int32), pltpu.VMEM((win,), jnp.int32),
                       pltpu.VMEM((win, width), slots.dtype), pltpu.VMEM((win, width), slots.dtype),
                       pltpu.SemaphoreType.DMA, pltpu.SemaphoreType.DMA])
    def gather_kernel(y_hbm, i_hbm, o_hbm, idx_a, idx_b, rows_a, rows_b, sem_a, sem_b):
        base = (lax.axis_index("subcore") * info.num_cores + lax.axis_index("core")) * per_worker

        def start(w, idx_v, rows_v, sem):
            pltpu.sync_copy(i_hbm.at[pl.ds(base + w * win, win)], idx_v)
            pltpu.async_copy(y_hbm.at[idx_v], rows_v, sem)

        def finish(w, idx_v, rows_v, sem):
            pltpu.make_async_copy(y_hbm.at[idx_v], rows_v, sem).wait()
            pltpu.sync_copy(rows_v, o_hbm.at[pl.ds(base + w * win, win)])

        start(0, idx_a, rows_a, sem_a)

        @pl.loop(0, n_pairs)
        def _(j):
            start(2 * j + 1, idx_b, rows_b, sem_b)
            finish(2 * j, idx_a, rows_a, sem_a)

            @pl.when(j + 1 < n_pairs)
            def _():
                start(2 * j + 2, idx_a, rows_a, sem_a)

            finish(2 * j + 1, idx_b, rows_b, sem_b)

    return gather_kernel(slots, index)


def _expert_kernel(be_ref, bv_ref, nx_ref, nu_ref, x_hbm, wg_hbm, wu_hbm, wd_hbm, y_hbm,
                   x_buf, y_buf, wg_f, wu_f, wd_f, wgu_s, wd_s, sems, x_sems, y_sems):
    i = pl.program_id(0)
    n_used = nu_ref[0]
    used = i < n_used
    expert = be_ref[i]
    new_expert = (i == 0) | (expert != be_ref[jnp.maximum(i - 1, 0)])
    blk = x_buf.shape[1]
    slot = lax.rem(i, RING)

    def x_copy(b):
        s = lax.rem(b, RING)
        return pltpu.make_async_copy(x_hbm.at[pl.ds(pl.multiple_of(b * blk, blk), blk)], x_buf.at[s], x_sems.at[s])

    def y_copy(b):
        s = lax.rem(b, RING)
        return pltpu.make_async_copy(y_buf.at[s], y_hbm.at[pl.ds(pl.multiple_of(b * blk, blk), blk)], y_sems.at[s])

    def weight_copies(e):
        return (pltpu.make_async_copy(wg_hbm.at[e], wg_f, sems.at[0]),
                pltpu.make_async_copy(wu_hbm.at[e], wu_f, sems.at[1]),
                pltpu.make_async_copy(wd_hbm.at[e], wd_f, sems.at[2]))

    @pl.when(used & (i == 0))
    def _():
        for c in weight_copies(expert):
            c.start()
        for b in range(RING - 1):
            @pl.when(b < n_used)
            def _():
                x_copy(b).start()

    @pl.when(i + RING - 1 < n_used)
    def _():
        x_copy(i + RING - 1).start()

    @pl.when(used & new_expert)
    def _():
        for c in weight_copies(expert):
            c.wait()
        wgu_s[:, 0:D_EXPERT] = wg_f[...].astype(BF16)
        wgu_s[:, D_EXPERT:2 * D_EXPERT] = wu_f[...].astype(BF16)
        wd_s[...] = wd_f[...].astype(BF16)
        nxt = nx_ref[i]

        @pl.when(nxt >= 0)
        def _():
            for c in weight_copies(nxt):
                c.start()

    @pl.when(used)
    def _():
        x_copy(i).wait()

        @pl.when(i >= RING)
        def _():
            y_copy(i - RING).wait()

        def ffn(rows):
            live = lax.broadcasted_iota(jnp.int32, (rows, D_MODEL // 2), 0) < bv_ref[i]
            xa, xb = _unpack_rows(jnp.where(live, x_buf[slot, 0:rows, :], 0))
            half = D_MODEL // 2
            gu = _dot(xa.astype(BF16), wgu_s[0:half, :]) + _dot(xb.astype(BF16), wgu_s[half:D_MODEL, :])
            g, u = gu[:, 0:D_EXPERT], gu[:, D_EXPERT:2 * D_EXPERT]
            y_buf[slot, 0:rows, :] = _pack_rows(_dot((g * _sigmoid(g) * u).astype(BF16), wd_s[...]))

        part = blk // EXPERT_PARTS
        for p in range(1, EXPERT_PARTS + 1):
            rows = p * part

            @pl.when((bv_ref[i] > rows - part) & (bv_ref[i] <= rows))
            def _(rows=rows):
                ffn(rows)
                if rows < blk:
                    y_buf[slot, rows:blk, :] = jnp.zeros((blk - rows, D_MODEL // 2), jnp.int32)

        y_copy(i).start()

    @pl.when(i == n_used - 1)
    def _():
        for back in range(RING):
            @pl.when(i - back >= 0)
            def _():
                y_copy(i - back).wait()


def _experts(block_expert, block_valid, block_next, n_used, x_slots, wg, wu, wd):
    p_rows = x_slots.shape[0]
    blk = EXPERT_BLOCK
    hbm = pl.BlockSpec(memory_space=pl.ANY)
    grid_spec = pltpu.PrefetchScalarGridSpec(
        num_scalar_prefetch=4,
        grid=(p_rows // blk,),
        in_specs=[hbm, hbm, hbm, hbm],
        out_specs=hbm,
        scratch_shapes=[pltpu.VMEM((RING, blk, D_MODEL // 2), jnp.int32),
                        pltpu.VMEM((RING, blk, D_MODEL // 2), jnp.int32),
                        pltpu.VMEM((D_MODEL, D_EXPERT), F32), pltpu.VMEM((D_MODEL, D_EXPERT), F32),
                        pltpu.VMEM((D_EXPERT, D_MODEL), F32),
                        pltpu.VMEM((D_MODEL, 2 * D_EXPERT), BF16), pltpu.VMEM((D_EXPERT, D_MODEL), BF16),
                        pltpu.SemaphoreType.DMA((3,)), pltpu.SemaphoreType.DMA((RING,)),
                        pltpu.SemaphoreType.DMA((RING,))])
    return pl.pallas_call(
        _expert_kernel,
        out_shape=jax.ShapeDtypeStruct((p_rows, D_MODEL // 2), jnp.int32),
        grid_spec=grid_spec,
        compiler_params=_cparams(("arbitrary",)),
        name="expert_ffn",
    )(block_expert, block_valid, block_next, n_used, x_slots, wg, wu, wd)


def _combine_kernel(yg_ref, gate_ref, h2_ref, x1_ref, ada_ref, g_ref, wsg_ref, wsu_ref, wsd_ref, y_ref):
    gates = gate_ref[...]
    half = D_MODEL // 2
    ha, hb = [v.astype(BF16) for v in _unpack_rows(h2_ref[...])]
    g = _dot(ha, wsg_ref[0:half, :]) + _dot(hb, wsg_ref[half:D_MODEL, :])
    u = _dot(ha, wsu_ref[0:half, :]) + _dot(hb, wsu_ref[half:D_MODEL, :])
    shared = _dot((g * _sigmoid(g) * u).astype(BF16), wsd_ref[...])
    acc_a = shared[:, 0:half]
    acc_b = shared[:, half:D_MODEL]
    for k in range(TOP_K):
        ya, yb = _unpack_rows(yg_ref[k])
        acc_a = acc_a + gates[:, k:k + 1] * ya
        acc_b = acc_b + gates[:, k:k + 1] * yb
    gt2 = ada_ref[0, 5:6, :]
    y_ref[...] = x1_ref[...] + gt2 * _rms(jnp.concatenate([acc_a, acc_b], axis=1), g_ref[...])


def _combine(yg, gates_t, h2p, x1, ada8, chunk, g_post, wsg, wsu, wsd, row_off):
    n_rows = yg.shape[1]
    tm = TM_COMB
    off = row_off // tm
    row = lambda i: (off + i, 0)
    const = lambda i: (0, 0)
    return pl.pallas_call(
        _combine_kernel,
        out_shape=jax.ShapeDtypeStruct((n_rows, D_MODEL), F32),
        grid=(n_rows // tm,),
        in_specs=[pl.BlockSpec((TOP_K, tm, D_MODEL // 2), lambda i: (0, i, 0)),
                  pl.BlockSpec((tm, TOP_K), row),
                  pl.BlockSpec((tm, D_MODEL // 2), row),
                  pl.BlockSpec((tm, D_MODEL), row),
                  pl.BlockSpec((1, 6, D_MODEL), lambda i: ((off + i) * tm // chunk, 0, 0)),
                  pl.BlockSpec((1, D_MODEL), const),
                  pl.BlockSpec(wsg.shape, const), pl.BlockSpec(wsu.shape, const), pl.BlockSpec(wsd.shape, const)],
        out_specs=pl.BlockSpec((tm, D_MODEL), lambda i: (i, 0)),
        compiler_params=_cparams(("parallel",)),
        name="moe_combine",
    )(yg, gates_t, h2p, x1, ada8, g_post, wsg, wsu, wsd)


def _rope_partner(w):
    half = QK_ROPE // 2
    return jnp.concatenate([-w[..., half:], w[..., :half]], axis=-1)


def _prep_weights(w_in, w_uq, w_ukv):
    d = w_in.shape[0]
    zeros = lambda r, c: jnp.zeros((r, c), F32)
    lat = Q_LORA + KV_LORA
    kr = w_in[:, lat:lat + QK_ROPE]
    r_main = jnp.concatenate([zeros(d, QK_NOPE), kr, zeros(d, HEAD_PAD - QK_NOPE - QK_ROPE)], axis=1)
    r_part = jnp.concatenate([zeros(d, QK_NOPE), _rope_partner(kr), zeros(d, HEAD_PAD - QK_NOPE - QK_ROPE)], axis=1)
    win2 = jnp.concatenate([w_in[:, :lat], r_main, r_part, w_in[:, lat + QK_ROPE:]], axis=1).astype(BF16)

    wq = w_uq.reshape(Q_LORA, N_HEADS, QK_NOPE + QK_ROPE)
    zq = jnp.zeros((Q_LORA, N_HEADS, HEAD_PAD - QK_NOPE - QK_ROPE), F32)
    q_main = jnp.concatenate([wq, zq], axis=-1).reshape(Q_LORA, N_HEADS * HEAD_PAD)
    q_part = jnp.concatenate([jnp.zeros((Q_LORA, N_HEADS, QK_NOPE), F32), _rope_partner(wq[..., QK_NOPE:]), zq],
                             axis=-1).reshape(Q_LORA, N_HEADS * HEAD_PAD)
    wuq2 = jnp.concatenate([q_main, q_part], axis=1).astype(BF16)

    wkv = w_ukv.reshape(KV_LORA, N_HEADS, QK_NOPE + HEAD_DIM)
    k_pad = jnp.concatenate([wkv[..., :QK_NOPE], jnp.zeros((KV_LORA, N_HEADS, HEAD_PAD - QK_NOPE), F32)],
                            axis=-1).reshape(KV_LORA, N_HEADS * HEAD_PAD)
    v_cols = wkv[..., QK_NOPE:].reshape(KV_LORA, N_HEADS * HEAD_DIM)
    wukv2 = jnp.concatenate([k_pad, v_cols], axis=1).astype(BF16)
    return win2, wuq2, wukv2


def _rope_table(s_max):
    half = QK_ROPE // 2
    inv = ROPE_BASE ** (-jnp.arange(half, dtype=F32) / half)
    ang = jnp.arange(s_max, dtype=F32)[:, None] * inv[None, :]
    cos = jnp.concatenate([jnp.cos(ang), jnp.cos(ang)], axis=1)
    sin = jnp.concatenate([jnp.sin(ang), jnp.sin(ang)], axis=1)
    scale = (QK_NOPE + QK_ROPE) ** -0.5 * math.log2(math.e)
    pad = jnp.zeros((s_max, HEAD_PAD - QK_NOPE - QK_ROPE), F32)
    zn = jnp.zeros((s_max, QK_NOPE), F32)
    cosq = jnp.concatenate([jnp.full((s_max, QK_NOPE), scale, F32), cos * scale, pad], axis=1)
    sinq = jnp.concatenate([zn, sin * scale, pad], axis=1)
    cosk = jnp.concatenate([zn, cos, pad], axis=1)
    sink = jnp.concatenate([zn, sin, pad], axis=1)
    return jnp.concatenate([cosq, sinq, cosk, sink], axis=1)


def _layer(x_prompt, x_sample, c_prompt, c_sample, w_ada, b_ada, g_pre_mix, w_in, g_qa, w_uq, g_kva, w_ukv,
           g_out_a, g_out_b, w_o, g_post_mix, g_pre_ffn, router_w, router_bias, w_exp_gate, w_exp_up, w_exp_down,
           w_sh_gate, w_sh_up, w_sh_down, g_post_ffn):
    b1, s1, d = x_prompt.shape
    b2, s2, _ = x_sample.shape
    n1, n2 = b1 * s1, b2 * s2
    t_all = n1 + n2
    chunk = math.gcd(s1, s2)
    assert n1 % s2 == 0 and chunk % max(TM_PRE, TM_POST, TM_COMB) == 0

    x_p, x_s = x_prompt.reshape(n1, d), x_sample.reshape(n2, d)
    nb = -(-(b1 + b2) // SUBLANES) * SUBLANES
    c_all = jnp.concatenate([c_prompt, c_sample, jnp.zeros((nb - b1 - b2, d), F32)], axis=0)
    ada = _ada(c_all, w_ada, b_ada)
    chunk_batch = jnp.concatenate([jnp.repeat(jnp.arange(b1), s1 // chunk),
                                   b1 + jnp.repeat(jnp.arange(b2), s2 // chunk)])
    ada8 = ada.reshape(nb, 6, d)[chunk_batch]

    win2, wuq2, wukv2 = _prep_weights(w_in, w_uq, w_ukv)
    tab = _rope_table(max(s1, s2))
    r2 = lambda g: g.reshape(1, -1)
    qa, ka, va, *dil_qkv = _premix(x_p, x_s, ada8, chunk, r2(g_pre_mix), win2, r2(g_qa), wuq2, r2(g_kva), wukv2, tab,
                                   n1, s1, s2)

    oa_p, oa_s = _mla(qa, ka, va, 0, b1, s1), _mla(qa, ka, va, n1, b2, s2)
    os_, ls_ = [], []
    for p, dil in enumerate(DILATIONS):
        o, lse = _dilated(*dil_qkv[3 * p:3 * p + 3], dil, n1, s1, s2)
        os_.append(o)
        ls_.append(lse)

    x1, h2p, idx_t, gate_t, rank_t, cnt = _postmix(
        x_p, x_s, oa_p, oa_s, os_, ls_, ada8, chunk, r2(g_out_a), r2(g_out_b), w_o.astype(BF16), r2(g_post_mix), r2(g_pre_ffn),
        router_w.T, router_bias.reshape(N_EXPERTS, 1))
    w_shared = [w.astype(BF16) for w in (w_sh_gate, w_sh_up, w_sh_down)]

    blk = EXPERT_BLOCK
    n_assign = t_all * TOP_K
    n_blocks = -(-(n_assign + N_EXPERTS * (blk - 1)) // blk)
    counts = cnt[:, 0].astype(jnp.int32)
    padded = (counts + blk - 1) // blk * blk
    pad_end = jnp.cumsum(padded)
    pad_start = pad_end - padded
    dest = _dest(idx_t, rank_t, pad_start)
    block_first = jnp.arange(n_blocks, dtype=jnp.int32) * blk
    block_expert = jnp.minimum(jnp.sum((pad_end[None, :] <= block_first[:, None]).astype(jnp.int32), axis=1),
                               N_EXPERTS - 1)
    n_used = (pad_end[-1] // blk).astype(jnp.int32).reshape(1)
    block_valid = jnp.clip(counts[block_expert] - (block_first - pad_start[block_expert]), 0, blk)
    eid = jnp.arange(N_EXPERTS, dtype=jnp.int32)
    used_from = lax.cummin(jnp.where(counts > 0, eid, N_EXPERTS), axis=0, reverse=True)
    next_used = jnp.concatenate([used_from[1:], jnp.full((1,), N_EXPERTS, jnp.int32)])
    block_next = jnp.where(next_used < N_EXPERTS, next_used, -1)[block_expert]
    dest_flat = dest.reshape(n_assign)
    x_slots = _dispatch_rows(h2p, dest_flat, n_blocks * blk)
    y_slots = _experts(block_expert, block_valid, block_next, n_used, x_slots, w_exp_gate, w_exp_up, w_exp_down)
    outs = []
    for off, n_g in ((0, n1), (n1, n2)):
        yg = _gather_rows(y_slots, dest[:, off:off + n_g].reshape(TOP_K * n_g)).reshape(TOP_K, n_g, d // 2)
        outs.append(_combine(yg, gate_t.T, h2p, x1, ada8, chunk, r2(g_post_ffn), *w_shared, off))
    return outs[0].reshape(b1, s1, d), outs[1].reshape(b2, s2, d)


def kernel(x_prompt, x_sample, c_prompt, c_sample, w_ada, b_ada, g_pre_mix, w_in, g_qa, w_uq, g_kva, w_ukv, g_out_a, g_out_b, w_o, g_post_mix, g_pre_ffn, router_w, router_bias, w_exp_gate, w_exp_up, w_exp_down, w_sh_gate, w_sh_up, w_sh_down, g_post_ffn):
    params = (w_ada, b_ada, g_pre_mix, w_in, g_qa, w_uq, g_kva, w_ukv, g_out_a, g_out_b, w_o, g_post_mix, g_pre_ffn,
              router_w, router_bias, w_exp_gate, w_exp_up, w_exp_down, w_sh_gate, w_sh_up, w_sh_down, g_post_ffn)
    assert all(p.shape[0] == 1 for p in params)
    return _layer(x_prompt, x_sample, c_prompt, c_sample, *[p[0] for p in params])
```
